```python
import jax, jax.numpy as jnp
from jax import lax
import numpy as np

D_MODEL = 1024
BATCH = 2
SEQ = 8192
DEPTH = 2

GRID_W = 64
CTX_LEN = 256

NA_HEADS = 4
NA_HEAD_DIM = 64
NA_W = NA_HEADS * NA_HEAD_DIM
WIN_ROWS = 8
WIN_COLS = 16

LRU_W = 512
LRU_BLOCKS = 8
LRU_BLOCK_W = LRU_W // LRU_BLOCKS
CONV_W = 4
LRU_C = 8.0

MLA_HEADS = 4
Q_LORA = 256
KV_LORA = 128
QK_NOPE = 64
QK_ROPE = 32
V_HEAD = 64
MLA_W = MLA_HEADS * V_HEAD
ROPE_BASE = 10000.0
Q_BLOCK = 128

MIX_W = NA_W + LRU_W + MLA_W
IN_SPLITS = (NA_W, NA_W, NA_W, LRU_W, LRU_W, Q_LORA, KV_LORA, QK_ROPE)
IN_W = sum(IN_SPLITS)
IN_OFFSETS = tuple(int(v) for v in np.cumsum(IN_SPLITS)[:-1])

N_EXPERTS = 16
N_GROUPS = 4
EXPERTS_PER_GROUP = N_EXPERTS // N_GROUPS
GROUP_SCORE_K = 2
TOP_K = 2
D_EXPERT = 512

RMS_EPS = 1e-6
N_MOD = 6

kernel_name = "hybrid_na_rglru_mla_moe_dit"


def rms_norm(x, g):
    xf = x.astype(jnp.float32)
    y = xf * lax.rsqrt(jnp.mean(xf * xf, axis=-1, keepdims=True) + RMS_EPS)
    return (y * g.astype(jnp.float32)).astype(x.dtype)


def modulate(h, shift, scale):
    return h * (1 + scale) + shift


def split_heads(t, n):
    b, l, _ = t.shape
    return t.reshape(b, l, n, -1).transpose(0, 2, 1, 3)


def merge_heads(t):
    b, h, l, d = t.shape
    return t.transpose(0, 2, 1, 3).reshape(b, l, h * d)


def dense_attention(q, k, v, scale):
    s = jnp.einsum('bhqd,bhkd->bhqk', q, k) * scale
    p = jax.nn.softmax(s.astype(jnp.float32), axis=-1).astype(v.dtype)
    return jnp.einsum('bhqk,bhkd->bhqd', p, v)


def blocked_attention(q, k, v, scale):
    b, h, s, d = q.shape
    nb = s // Q_BLOCK
    qb = q.reshape(b, h, nb, Q_BLOCK, d).transpose(2, 0, 1, 3, 4)
    ob = lax.map(lambda qi: dense_attention(qi, k, v, scale), qb)
    return ob.transpose(1, 2, 0, 3, 4).reshape(b, h, s, v.shape[-1])


def neighbourhood_attention(q, k, v, k_ctx, v_ctx, rpb):
    b, h, s, dh = q.shape
    rows = s // GRID_W
    kr = min(WIN_ROWS, rows)
    n_win = kr * WIN_COLS
    scale = dh ** -0.5
    qg = q.reshape(b, h, rows, GRID_W, dh)
    kg = k.reshape(b, h, rows, GRID_W, dh)
    vg = v.reshape(b, h, rows, GRID_W, dh)
    col = jnp.arange(GRID_W)
    col_start = jnp.clip(col - WIN_COLS // 2, 0, GRID_W - WIN_COLS)
    col_idx = col_start[:, None] + jnp.arange(WIN_COLS)[None, :]
    dc = col_idx - col[:, None] + (WIN_COLS - 1)

    def row_block(r):
        rs = jnp.clip(r - kr // 2, 0, rows - kr)
        q_r = lax.dynamic_index_in_dim(qg, r, axis=2, keepdims=False)
        k_win = lax.dynamic_slice_in_dim(kg, rs, kr, axis=2)[:, :, :, col_idx]
        v_win = lax.dynamic_slice_in_dim(vg, rs, kr, axis=2)[:, :, :, col_idx]
        dr = rs + jnp.arange(kr) - r + (WIN_ROWS - 1)
        bias = rpb[:, dr[None, :, None], dc[:, None, :]]
        s_win = jnp.einsum('bhqd,bhiqjd->bhqij', q_r, k_win) * scale + bias.astype(q.dtype)[None]
        s_ctx = jnp.einsum('bhqd,bhcd->bhqc', q_r, k_ctx) * scale
        scores = jnp.concatenate([s_win.reshape(b, h, GRID_W, n_win), s_ctx], axis=-1)
        p = jax.nn.softmax(scores.astype(jnp.float32), axis=-1).astype(v.dtype)
        p_win = p[..., :n_win].reshape(b, h, GRID_W, kr, WIN_COLS)
        return (jnp.einsum('bhqij,bhiqjd->bhqd', p_win, v_win)
                + jnp.einsum('bhqc,bhcd->bhqd', p[..., n_win:], v_ctx))

    out = lax.map(row_block, jnp.arange(rows))
    return out.transpose(1, 2, 0, 3, 4).reshape(b, h, s, dh)


def depthwise_conv_centred(u, w, bias):
    L = u.shape[1]
    left = (CONV_W - 1) // 2
    up = jnp.pad(u, ((0, 0), (left, CONV_W - 1 - left), (0, 0)))
    out = up[:, 0:L] * w[0] + bias
    for j in range(1, CONV_W):
        out = out + up[:, j:j + L] * w[j]
    return out


def rglru_coeffs(u, w_a, b_a, w_i, b_i, lam):
    b, l, _ = u.shape
    uf = u.astype(jnp.float32)
    ub = uf.reshape(b, l, LRU_BLOCKS, LRU_BLOCK_W)
    r = jax.nn.sigmoid(jnp.einsum('blnk,nkj->blnj', ub, w_a.astype(jnp.float32)).reshape(b, l, LRU_W) + b_a)
    i = jax.nn.sigmoid(jnp.einsum('blnk,nkj->blnj', ub, w_i.astype(jnp.float32)).reshape(b, l, LRU_W) + b_i)
    log_a = -LRU_C * r * jax.nn.softplus(-lam.astype(jnp.float32))
    a = jnp.exp(log_a)
    gain = jnp.sqrt(-jnp.expm1(2.0 * log_a))
    return a, gain * (i * uf)


def linear_scan(a, x, h0, reverse):
    if h0 is not None:
        edge = -1 if reverse else 0
        x = x.at[:, edge].add(a[:, edge] * h0)

    def combine(lhs, rhs):
        return (lhs[0] * rhs[0], rhs[0] * lhs[1] + rhs[1])

    _, h = lax.associative_scan(combine, (a, x), reverse=reverse, axis=1)
    return h


def rglru_bidirectional(u_x, u_c, w_a, b_a, w_i, b_i, lam):
    out_x = 0.0
    out_c = 0.0
    for d, reverse in enumerate((False, True)):
        a_c, x_c = rglru_coeffs(u_c, w_a[d], b_a[d], w_i[d], b_i[d], lam[d])
        h_c = linear_scan(a_c, x_c, None, reverse)
        h_end = h_c[:, 0] if reverse else h_c[:, -1]
        a_x, x_x = rglru_coeffs(u_x, w_a[d], b_a[d], w_i[d], b_i[d], lam[d])
        h_x = linear_scan(a_x, x_x, h_end, reverse)
        out_x = out_x + h_x
        out_c = out_c + h_c
    return out_x.astype(u_x.dtype), out_c.astype(u_c.dtype)


def axial_rope_tables(s):
    t = jnp.arange(s, dtype=jnp.int32)
    row = (t // GRID_W).astype(jnp.float32)
    col = (t % GRID_W).astype(jnp.float32)
    axis_dim = QK_ROPE // 2
    inv = ROPE_BASE ** (-jnp.arange(0, axis_dim, 2, dtype=jnp.float32) / axis_dim)
    ang_r = row[:, None] * inv
    ang_c = col[:, None] * inv
    return (jnp.cos(ang_r)[:, None], jnp.sin(ang_r)[:, None], jnp.cos(ang_c)[:, None], jnp.sin(ang_c)[:, None])


def rope_2d(x, tabs):
    cos_r, sin_r, cos_c, sin_c = tabs
    dt = x.dtype

    def rot(xs, cos, sin):
        x1, x2 = jnp.split(xs, 2, axis=-1)
        cos = cos.astype(dt)
        sin = sin.astype(dt)
        return jnp.concatenate([x1 * cos - x2 * sin, x2 * cos + x1 * sin], axis=-1)

    half = x.shape[-1] // 2
    return jnp.concatenate([rot(x[..., :half], cos_r, sin_r), rot(x[..., half:], cos_c, sin_c)], axis=-1)


def mla_q(cq, g_q, w_q, tabs):
    b, l, _ = cq.shape
    q = (rms_norm(cq, g_q) @ w_q).reshape(b, l, MLA_HEADS, QK_NOPE + QK_ROPE)
    q_nope, q_rope = q[..., :QK_NOPE], q[..., QK_NOPE:]
    if tabs is not None:
        q_rope = rope_2d(q_rope, tabs)
    return jnp.concatenate([q_nope, q_rope], axis=-1).transpose(0, 2, 1, 3)


def mla_kv(ckv, kr, g_kv, w_kv, tabs):
    b, l, _ = ckv.shape
    kv = (rms_norm(ckv, g_kv) @ w_kv).reshape(b, l, MLA_HEADS, QK_NOPE + V_HEAD)
    k_nope, v = kv[..., :QK_NOPE], kv[..., QK_NOPE:]
    k_rope = kr[:, :, None, :]
    if tabs is not None:
        k_rope = rope_2d(k_rope, tabs)
    k = jnp.concatenate([k_nope, jnp.broadcast_to(k_rope, (b, l, MLA_HEADS, QK_ROPE))], axis=-1)
    return k.transpose(0, 2, 1, 3), v.transpose(0, 2, 1, 3)


def token_mixer(hx, hc, w_in, w_out, rpb, conv_w, conv_b, lru_w_a, lru_b_a, lru_w_i, lru_b_i, lru_lam,
                g_q, w_q, g_kv, w_kv, tabs, need_ctx):
    qa_x, ka_x, va_x, u_x, gate_x, cq_x, ckv_x, kr_x = jnp.split(hx @ w_in, IN_OFFSETS, axis=-1)
    qa_c, ka_c, va_c, u_c, gate_c, cq_c, ckv_c, kr_c = jnp.split(hc @ w_in, IN_OFFSETS, axis=-1)

    k_ac = split_heads(ka_c, NA_HEADS)
    v_ac = split_heads(va_c, NA_HEADS)
    ya_x = merge_heads(neighbourhood_attention(split_heads(qa_x, NA_HEADS), split_heads(ka_x, NA_HEADS),
                                               split_heads(va_x, NA_HEADS), k_ac, v_ac, rpb))

    hb_x, hb_c = rglru_bidirectional(depthwise_conv_centred(u_x, conv_w, conv_b),
                                     depthwise_conv_centred(u_c, conv_w, conv_b),
                                     lru_w_a, lru_b_a, lru_w_i, lru_b_i, lru_lam)
    yb_x = hb_x * jax.nn.gelu(gate_x)

    mla_scale = (QK_NOPE + QK_ROPE) ** -0.5
    k_cc, v_cc = mla_kv(ckv_c, kr_c, g_kv, w_kv, None)
    k_cx, v_cx = mla_kv(ckv_x, kr_x, g_kv, w_kv, tabs)
    q_cx = mla_q(cq_x, g_q, w_q, tabs)
    yc_x = merge_heads(blocked_attention(q_cx, jnp.concatenate([k_cc, k_cx], axis=2),
                                         jnp.concatenate([v_cc, v_cx], axis=2), mla_scale))

    yx = jnp.concatenate([ya_x, yb_x, yc_x], axis=-1) @ w_out
    if not need_ctx:
        return yx, None

    ya_c = merge_heads(dense_attention(split_heads(qa_c, NA_HEADS), k_ac, v_ac, NA_HEAD_DIM ** -0.5))
    yb_c = hb_c * jax.nn.gelu(gate_c)
    yc_c = merge_heads(dense_attention(mla_q(cq_c, g_q, w_q, None), k_cc, v_cc, mla_scale))
    yc = jnp.concatenate([ya_c, yb_c, yc_c], axis=-1) @ w_out
    return yx, yc


def moe_ffn(h, router_w, router_b, w_gate, w_up, w_down):
    b, l, d = h.shape
    t = h.reshape(b * l, d)
    aff = jax.nn.sigmoid((t @ router_w).astype(jnp.float32))
    sel = aff + router_b.astype(jnp.float32)
    grp_top, _ = lax.top_k(sel.reshape(-1, N_GROUPS, EXPERTS_PER_GROUP), GROUP_SCORE_K)
    best = jnp.argmax(grp_top.sum(-1), axis=-1)
    in_grp = jnp.repeat(jnp.arange(N_GROUPS)[None, :] == best[:, None], EXPERTS_PER_GROUP, axis=1)
    _, idx = lax.top_k(jnp.where(in_grp, sel, -jnp.inf), TOP_K)
    wsel = jnp.take_along_axis(aff, idx, axis=-1)
    wsel = wsel / jnp.sum(wsel, axis=-1, keepdims=True)
    combine = jnp.sum(jax.nn.one_hot(idx, N_EXPERTS, dtype=jnp.float32) * wsel[..., None], axis=1).astype(h.dtype)
    out = jnp.zeros_like(t)
    for e in range(N_EXPERTS):
        he = jax.nn.silu(t @ w_gate[e]) * (t @ w_up[e])
        out = out + combine[:, e:e + 1] * (he @ w_down[e])
    return out.reshape(b, l, d)


def setup_inputs(seed: int = 0) -> dict:
    key = jax.random.key(seed)
    ks = jax.random.split(key, 26)
    f32 = jnp.float32

    def nrm(k, shape, scale):
        return jax.random.normal(k, shape, f32) * scale

    a_decay = jax.random.uniform(ks[16], (DEPTH, 2, LRU_W), f32, 0.9, 0.999)
    a_base = a_decay ** (1.0 / LRU_C)
    return {
        "x": nrm(ks[0], (BATCH, SEQ, D_MODEL), 1.0),
        "c": nrm(ks[1], (BATCH, D_MODEL), 1.0),
        "ctx": nrm(ks[2], (BATCH, CTX_LEN, D_MODEL), 1.0),
        "c_ctx": nrm(ks[3], (D_MODEL,), 1.0),
        "w_ada": nrm(ks[4], (DEPTH, D_MODEL, N_MOD * D_MODEL), 0.3 * D_MODEL ** -0.5),
        "b_ada": nrm(ks[5], (DEPTH, N_MOD * D_MODEL), 0.02),
        "g_norm": 1.0 + nrm(ks[6], (DEPTH, 4, D_MODEL), 0.05),
        "w_in": nrm(ks[7], (DEPTH, D_MODEL, IN_W), D_MODEL ** -0.5),
        "w_out": nrm(ks[8], (DEPTH, MIX_W, D_MODEL), MIX_W ** -0.5),
        "na_rpb": nrm(ks[9], (DEPTH, NA_HEADS, 2 * WIN_ROWS - 1, 2 * WIN_COLS - 1), 0.2),
        "conv_w": nrm(ks[10], (DEPTH, CONV_W, LRU_W), CONV_W ** -0.5),
        "conv_b": nrm(ks[11], (DEPTH, LRU_W), 0.02),
        "lru_w_a": nrm(ks[12], (DEPTH, 2, LRU_BLOCKS, LRU_BLOCK_W, LRU_BLOCK_W), LRU_BLOCK_W ** -0.5),
        "lru_b_a": nrm(ks[13], (DEPTH, 2, LRU_W), 0.02),
        "lru_w_i": nrm(ks[14], (DEPTH, 2, LRU_BLOCKS, LRU_BLOCK_W, LRU_BLOCK_W), LRU_BLOCK_W ** -0.5),
        "lru_b_i": nrm(ks[15], (DEPTH, 2, LRU_W), 0.02),
        "lru_lam": jnp.log(a_base) - jnp.log1p(-a_base),
        "mla_g_q": 1.0 + nrm(ks[17], (DEPTH, Q_LORA), 0.05),
        "mla_w_q": nrm(ks[18], (DEPTH, Q_LORA, MLA_HEADS * (QK_NOPE + QK_ROPE)), Q_LORA ** -0.5),
        "mla_g_kv": 1.0 + nrm(ks[19], (DEPTH, KV_LORA), 0.05),
        "mla_w_kv": nrm(ks[20], (DEPTH, KV_LORA, MLA_HEADS * (QK_NOPE + V_HEAD)), KV_LORA ** -0.5),
        "router_w": nrm(ks[21], (D_MODEL, N_EXPERTS), D_MODEL ** -0.5),
        "router_b": nrm(ks[22], (N_EXPERTS,), 0.01),
        "exp_w_gate": nrm(ks[23], (DEPTH, N_EXPERTS, D_MODEL, D_EXPERT), D_MODEL ** -0.5),
        "exp_w_up": nrm(ks[24], (DEPTH, N_EXPERTS, D_MODEL, D_EXPERT), D_MODEL ** -0.5),
        "exp_w_down": nrm(ks[25], (DEPTH, N_EXPERTS, D_EXPERT, D_MODEL), D_EXPERT ** -0.5),
    }


def reference(x, c, ctx, c_ctx, w_ada, b_ada, g_norm, w_in, w_out, na_rpb, conv_w, conv_b,
              lru_w_a, lru_b_a, lru_w_i, lru_b_i, lru_lam, mla_g_q, mla_w_q, mla_g_kv, mla_w_kv,
              router_w, router_b, exp_w_gate, exp_w_up, exp_w_down):
    b, s, d = x.shape
    tabs = axial_rope_tables(s)
    sc = jax.nn.silu(c)
    scc = jax.nn.silu(c_ctx)
    cx = ctx
    for l in range(DEPTH):
        last = l == DEPTH - 1
        mod_x = (sc @ w_ada[l] + b_ada[l]).reshape(b, N_MOD, 1, d)
        mod_c = (scc @ w_ada[l] + b_ada[l]).reshape(N_MOD, 1, 1, d)
        g = g_norm[l]

        hx = modulate(rms_norm(x, g[0]), mod_x[:, 0], mod_x[:, 1])
        hc = modulate(rms_norm(cx, g[0]), mod_c[0], mod_c[1])
        yx, yc = token_mixer(hx, hc, w_in[l], w_out[l], na_rpb[l], conv_w[l], conv_b[l],
                             lru_w_a[l], lru_b_a[l], lru_w_i[l], lru_b_i[l], lru_lam[l],
                             mla_g_q[l], mla_w_q[l], mla_g_kv[l], mla_w_kv[l], tabs, not last)
        x = x + mod_x[:, 2] * rms_norm(yx, g[1])

        hx = modulate(rms_norm(x, g[2]), mod_x[:, 3], mod_x[:, 4])
        if last:
            x = x + mod_x[:, 5] * rms_norm(moe_ffn(hx, router_w, router_b, exp_w_gate[l], exp_w_up[l], exp_w_down[l]), g[3])
        else:
            cx = cx + mod_c[2] * rms_norm(yc, g[1])
            hc = modulate(rms_norm(cx, g[2]), mod_c[3], mod_c[4])
            n_ctx = cx.shape[1]
            f = moe_ffn(jnp.concatenate([hc, hx], axis=1), router_w, router_b,
                        exp_w_gate[l], exp_w_up[l], exp_w_down[l])
            cx = cx + mod_c[5] * rms_norm(f[:, :n_ctx], g[3])
            x = x + mod_x[:, 5] * rms_norm(f[:, n_ctx:], g[3])
    return x
```

```python
import functools

import numpy as np
import jax
import jax.numpy as jnp
from jax import lax
from jax.experimental import pallas as pl
from jax.experimental.pallas import tpu as pltpu

F32 = jnp.float32
BF16 = jnp.bfloat16

D_MODEL = 1024
GRID_W = 64
NA_HEADS = 4
NA_HEAD_DIM = 64
NA_W = NA_HEADS * NA_HEAD_DIM
WIN_ROWS = 8
WIN_COLS = 16
LRU_W = 512
LRU_BLOCKS = 8
CONV_W = 4
LRU_C = 8.0
MLA_HEADS = 4
Q_LORA = 256
KV_LORA = 128
QK_NOPE = 64
QK_ROPE = 32
V_HEAD = 64
MLA_W = MLA_HEADS * V_HEAD
ROPE_BASE = 10000.0
N_EXPERTS = 16
N_GROUPS = 4
EXPERTS_PER_GROUP = 4
D_EXPERT = 512
RMS_EPS = 1e-6
N_MOD = 6

LANES = 128
TM = 256
HEAD_PAD = 128
NEG_BIAS = -1e30
VMEM_LIMIT = 56 * 1024 * 1024

_C_QA, _C_KA, _C_VA = 0, 256, 512
_C_U = 768
_C_GATE = 1280
_C_CQ = 1792
_C_CKV = 2048
_C_KR = 2176
_C_KRP = 2304
_IN_COLS = 2432


def _params(sem, vmem=VMEM_LIMIT):
    return pltpu.CompilerParams(dimension_semantics=sem, vmem_limit_bytes=vmem)


def _sigmoid(v):
    return 1.0 / (1.0 + jnp.exp(-v))


def _rms(v, g):
    return v * lax.rsqrt(jnp.mean(v * v, axis=-1, keepdims=True) + RMS_EPS) * g


def _dot(a, b):
    return jnp.dot(a, b, preferred_element_type=F32)


def _dot_nt(a, b):
    return lax.dot_general(a, b, (((1,), (1,)), ((), ())), preferred_element_type=F32)


def _ada_body(c_ref, w_ref, b_ref, o_ref):
    c = c_ref[...]
    sc = c * _sigmoid(c)
    o_ref[0] = _dot(sc.astype(BF16), w_ref[0].astype(BF16)) + b_ref[0]


def _ada_tables(c_rows, w_ada, b_ada):
    depth, d, n = w_ada.shape
    tn = 1536
    return pl.pallas_call(
        _ada_body,
        grid=(depth, n // tn),
        in_specs=[pl.BlockSpec((8, d), lambda l, j: (0, 0)),
                  pl.BlockSpec((1, d, tn), lambda l, j: (l, 0, j)),
                  pl.BlockSpec((1, 1, tn), lambda l, j: (l, 0, j))],
        out_specs=pl.BlockSpec((1, 8, tn), lambda l, j: (l, 0, j)),
        out_shape=jax.ShapeDtypeStruct((depth, 8, n), F32),
        compiler_params=_params(("parallel", "parallel")),
        name="ada_tables",
    )(c_rows, w_ada, b_ada.reshape(depth, 1, n))


def _proj_in_body(x_ref, shift_ref, scale_ref, g_ref, w_ref, gq_ref, wq_ref, gkv_ref, wkv_ref,
                  cos_ref, sin_ref, ones_ref,
                  qkv_ref, u_ref, gg_ref, mq_ref, mk_ref, mv_ref, *, mla_scale):
    x = x_ref[...]
    h = _rms(x, g_ref[...]) * (1.0 + scale_ref[0]) + shift_ref[0]
    z = _dot(h.astype(BF16), w_ref[...])
    qkv_ref[:, 0:NA_W] = (z[:, _C_QA:_C_KA] * (NA_HEAD_DIM ** -0.5)).astype(BF16)
    qkv_ref[:, NA_W:3 * NA_W] = z[:, _C_KA:_C_U].astype(BF16)
    u_ref[...] = z[:, _C_U:_C_GATE]
    gg_ref[...] = jax.nn.gelu(z[:, _C_GATE:_C_CQ]).astype(BF16)

    cos = cos_ref[...]
    sin = sin_ref[...]
    cos4 = jnp.concatenate([cos] * MLA_HEADS, axis=-1)
    sin4 = jnp.concatenate([sin] * MLA_HEADS, axis=-1)
    nw = MLA_HEADS * HEAD_PAD

    nq = _rms(z[:, _C_CQ:_C_CKV], gq_ref[...])
    q2 = _dot(nq.astype(BF16), wq_ref[...])
    mq_ref[...] = ((q2[:, :nw] * cos4 + q2[:, nw:] * sin4) * mla_scale).astype(BF16)

    nkv = _rms(z[:, _C_CKV:_C_KR], gkv_ref[...])
    kv2 = _dot(nkv.astype(BF16), wkv_ref[...])
    k_rope = z[:, _C_KR:_C_KRP] * cos + z[:, _C_KRP:_IN_COLS] * sin
    mk_ref[...] = (kv2[:, :nw] + jnp.concatenate([k_rope] * MLA_HEADS, axis=-1)).astype(BF16)
    mv_ref[...] = (kv2[:, nw:] + ones_ref[...]).astype(BF16)


def _proj_in(x, shift, scale, g, w_big, gq, wq2, gkv, wkv2, cos_t, sin_t, ones_row, *, tiles_per_sample, n_lat_tiles,
             n_batch):
    t, d = x.shape
    nt = t // TM
    nw = MLA_HEADS * HEAD_PAD

    def seg(i):
        return jnp.where(i % tiles_per_sample >= n_lat_tiles, n_batch, i // tiles_per_sample)

    row = lambda n: pl.BlockSpec((TM, n), lambda i: (i, 0))
    full = lambda a: pl.BlockSpec(a.shape, lambda i: (0,) * a.ndim)
    modspec = pl.BlockSpec((1, 1, d), lambda i: (seg(i), 0, 0))
    tabspec = pl.BlockSpec((TM, HEAD_PAD), lambda i: (i % tiles_per_sample, 0))
    mla_scale = (QK_NOPE + QK_ROPE) ** -0.5
    return pl.pallas_call(
        functools.partial(_proj_in_body, mla_scale=mla_scale),
        grid=(nt,),
        in_specs=[row(d), modspec, modspec, full(g), full(w_big), full(gq), full(wq2), full(gkv), full(wkv2),
                  tabspec, tabspec, full(ones_row)],
        out_specs=[row(3 * NA_W), row(LRU_W), row(LRU_W), row(nw), row(nw), row(nw)],
        out_shape=[jax.ShapeDtypeStruct((t, 3 * NA_W), BF16),
                   jax.ShapeDtypeStruct((t, LRU_W), F32),
                   jax.ShapeDtypeStruct((t, LRU_W), BF16),
                   jax.ShapeDtypeStruct((t, nw), BF16),
                   jax.ShapeDtypeStruct((t, nw), BF16),
                   jax.ShapeDtypeStruct((t, nw), BF16)],
        compiler_params=_params(("parallel",)),
        name="proj_in",
    )(x, shift, scale, g, w_big, gq, wq2, gkv, wkv2, cos_t, sin_t, ones_row)


def _head_block_mask():
    r = lax.broadcasted_iota(jnp.int32, (NA_W, NA_W), 0) // NA_HEAD_DIM
    c = lax.broadcasted_iota(jnp.int32, (NA_W, NA_W), 1) // NA_HEAD_DIM
    return r == c


def _na_attend(q, parts, mask):
    qbig = jnp.where(mask, jnp.concatenate([q] * NA_HEADS, axis=0), jnp.zeros((), q.dtype))
    scores = []
    for k, _, bias in parts:
        s = _dot_nt(qbig, k)
        if bias is not None:
            s = s + bias
        scores.append(s)
    m = functools.reduce(jnp.maximum, [jnp.max(s, axis=-1, keepdims=True) for s in scores])
    ps = [jnp.exp(s - m) for s in scores]
    l = functools.reduce(jnp.add, [jnp.sum(p, axis=-1, keepdims=True) for p in ps])
    o = functools.reduce(jnp.add, [_dot(p.astype(BF16), v) for p, (_, v, _) in zip(ps, parts)])
    o = jnp.where(mask, o / l, 0.0)
    out = o[0:NA_HEAD_DIM]
    for h in range(1, NA_HEADS):
        out = out + o[h * NA_HEAD_DIM:(h + 1) * NA_HEAD_DIM]
    return out


def _na_body(q_ref, k_ref, v_ref, bias_ref, o_ref, *, rows, rows_per_step, n_lat_steps, seq, ctx):
    i = pl.program_id(1)
    mask = _head_block_mask()
    kc = k_ref[0, seq:seq + ctx, :]
    vc = v_ref[0, seq:seq + ctx, :]

    @pl.when(i < n_lat_steps)
    def _():
        for j in range(rows_per_step):
            r = i * rows_per_step + j
            rs = jnp.clip(r - WIN_ROWS // 2, 0, rows - WIN_ROWS)
            off = rs - r + (WIN_ROWS - 1)
            start = pl.multiple_of(rs * GRID_W, GRID_W)
            kw = k_ref[0, pl.ds(start, WIN_ROWS * GRID_W), :]
            vw = v_ref[0, pl.ds(start, WIN_ROWS * GRID_W), :]
            q = q_ref[0, j * GRID_W:(j + 1) * GRID_W, :]
            out = _na_attend(q, [(kw, vw, bias_ref[off]), (kc, vc, None)], mask)
            o_ref[0, j * GRID_W:(j + 1) * GRID_W, :] = out.astype(o_ref.dtype)

    @pl.when(i >= n_lat_steps)
    def _():
        for j in range(rows_per_step):
            q = q_ref[0, j * GRID_W:(j + 1) * GRID_W, :]
            out = _na_attend(q, [(kc, vc, None)], mask)
            o_ref[0, j * GRID_W:(j + 1) * GRID_W, :] = out.astype(o_ref.dtype)


def _na_attention(qkv, bias, *, seq, ctx):
    b, lt, _ = qkv.shape
    rows = seq // GRID_W
    rows_per_step = TM // GRID_W
    n_steps = lt // TM
    n_lat_steps = seq // TM
    body = functools.partial(_na_body, rows=rows, rows_per_step=rows_per_step, n_lat_steps=n_lat_steps,
                             seq=seq, ctx=ctx)
    return pl.pallas_call(
        body,
        grid=(b, n_steps),
        in_specs=[pl.BlockSpec((1, TM, NA_W), lambda bi, i: (bi, i, 0)),
                  pl.BlockSpec((1, lt, NA_W), lambda bi, i: (bi, 0, 1)),
                  pl.BlockSpec((1, lt, NA_W), lambda bi, i: (bi, 0, 2)),
                  pl.BlockSpec(bias.shape, lambda bi, i: (0, 0, 0))],
        out_specs=pl.BlockSpec((1, TM, NA_W), lambda bi, i: (bi, i, 0)),
        out_shape=jax.ShapeDtypeStruct((b, lt, NA_W), BF16),
        compiler_params=_params(("parallel", "arbitrary")),
        name="na_attention",
    )(qkv, qkv, qkv, bias)


def _na_bias_tables(rpb):
    q = np.arange(GRID_W)
    cs = np.clip(q - WIN_COLS // 2, 0, GRID_W - WIN_COLS)
    c = np.arange(GRID_W)
    inside = (c[None, :] >= cs[:, None]) & (c[None, :] < cs[:, None] + WIN_COLS)
    dc = np.clip(c[None, :] - q[:, None] + (WIN_COLS - 1), 0, 2 * WIN_COLS - 2)
    off = np.arange(WIN_ROWS)
    i = np.arange(WIN_ROWS)
    dr = off[:, None] + i[None, :]
    vals = rpb[:, dr[:, None, :, None], dc[None, :, None, :]]
    vals = jnp.where(inside[None, None, :, None, :], vals.astype(F32), NEG_BIAS)
    vals = vals.transpose(1, 0, 2, 3, 4)
    return vals.reshape(WIN_ROWS, NA_HEADS * GRID_W, WIN_ROWS * GRID_W)


def _softplus(v):
    return jnp.maximum(v, 0.0) + jnp.log1p(jnp.exp(-jnp.abs(v)))


def _lru_body(up_ref, uc_ref, un_ref, cw_ref, cb_ref, wa_ref, ba_ref, wi_ref, bi_ref, lam_ref,
              o_ref, a_buf, x_buf, st_ref, *, n_lat, chunk, n_batch):
    d = pl.program_id(0)
    i = pl.program_id(1)
    c = _lru_chunk(d, i, n_lat)
    prev_ok = jnp.logical_and(c >= 1, c < n_lat)
    next_ok = c <= n_lat - 2

    row = lax.broadcasted_iota(jnp.int32, (chunk, 1), 0)
    cw = cw_ref[...]
    log_decay = _softplus(-lam_ref[0])
    for b in range(n_batch):
        ub = uc_ref[b]
        p_row = jnp.where(prev_ok, up_ref[b, 7:8, :], 0.0)
        n0 = jnp.where(next_ok, un_ref[b, 0:1, :], 0.0)
        n1 = jnp.where(next_ok, un_ref[b, 1:2, :], 0.0)
        um1 = jnp.where(row == 0, p_row, pltpu.roll(ub, 1, 0))
        up1 = jnp.where(row == chunk - 1, n0, pltpu.roll(ub, chunk - 1, 0))
        up2 = jnp.where(row == chunk - 1, n1, jnp.where(row == chunk - 2, n0, pltpu.roll(ub, chunk - 2, 0)))
        cv = um1 * cw[0:1] + cb_ref[...]
        cv = cv + ub * cw[1:2]
        cv = cv + up1 * cw[2:3]
        cv = cv + up2 * cw[3:4]
        cv16 = cv.astype(BF16)
        r = _sigmoid(_dot(cv16, wa_ref[0]) + ba_ref[0])
        gi = _sigmoid(_dot(cv16, wi_ref[0]) + bi_ref[0])
        log_a = (-LRU_C * r) * log_decay
        a = jnp.exp(log_a)
        a_buf[b] = a
        x_buf[b] = jnp.sqrt(1.0 - a * a) * (gi * cv)

    @pl.when(i == 0)
    def _():
        st_ref[...] = jnp.zeros_like(st_ref)

    def step(t, hs):
        tt = jnp.where(d == 0, t, chunk - 1 - t)
        new = []
        for b in range(n_batch):
            h = a_buf[b, pl.ds(tt, 1), :] * hs[b] + x_buf[b, pl.ds(tt, 1), :]
            o_ref[0, b, pl.ds(tt, 1), :] = h
            new.append(h)
        return tuple(new)

    hs = lax.fori_loop(0, chunk, step, tuple(st_ref[b, 0:1, :] for b in range(n_batch)), unroll=8)
    for b in range(n_batch):
        st_ref[b, 0:1, :] = hs[b]


def _lru_chunk(d, i, n_lat):
    return jnp.where(i == 0, n_lat, jnp.where(d == 0, i - 1, n_lat - i))


def _lru_scan(u, conv_w, conv_b, w_a, b_a, w_i, b_i, lam, *, seq, ctx):
    b, lt, w = u.shape
    chunk = ctx
    n_lat = seq // chunk
    n_chunks = lt // chunk
    sub = chunk // 8
    n_sub = lt // 8

    cur = lambda d, i: (0, _lru_chunk(d, i, n_lat), 0)
    prev = lambda d, i: (0, jnp.maximum(_lru_chunk(d, i, n_lat) * sub - 1, 0), 0)
    nxt = lambda d, i: (0, jnp.minimum((_lru_chunk(d, i, n_lat) + 1) * sub, n_sub - 1), 0)
    per_dir = lambda a: pl.BlockSpec((1,) + a.shape[1:], lambda d, i: (d,) + (0,) * (a.ndim - 1))
    full = lambda a: pl.BlockSpec(a.shape, lambda d, i: (0,) * a.ndim)
    body = functools.partial(_lru_body, n_lat=n_lat, chunk=chunk, n_batch=b)
    return pl.pallas_call(
        body,
        grid=(2, n_chunks),
        in_specs=[pl.BlockSpec((b, 8, w), prev), pl.BlockSpec((b, chunk, w), cur), pl.BlockSpec((b, 8, w), nxt),
                  full(conv_w), full(conv_b), per_dir(w_a), per_dir(b_a), per_dir(w_i), per_dir(b_i), per_dir(lam)],
        out_specs=pl.BlockSpec((1, b, chunk, w), lambda d, i: (d, 0, _lru_chunk(d, i, n_lat), 0)),
        out_shape=jax.ShapeDtypeStruct((2, b, lt, w), F32),
        scratch_shapes=[pltpu.VMEM((b, chunk, w), F32), pltpu.VMEM((b, chunk, w), F32), pltpu.VMEM((b, 8, w), F32)],
        compiler_params=_params(("arbitrary", "arbitrary")),
        name="lru_scan",
    )(u, u, u, conv_w, conv_b, w_a, b_a, w_i, b_i, lam)


def _mla_body(q_ref, k_ref, v_ref, o_ref, *, tk, nk):
    tq = q_ref.shape[1]
    outs = []
    for h in range(MLA_HEADS):
        cols = slice(h * HEAD_PAD, (h + 1) * HEAD_PAD)
        q = q_ref[0, :, cols]

        def chunk(ci, carry, cols=cols, q=q):
            m, acc = carry
            start = pl.multiple_of(ci * tk, tk)
            k = k_ref[0, pl.ds(start, tk), cols]
            v = v_ref[0, pl.ds(start, tk), cols]
            s = _dot_nt(q, k)
            m_new = jnp.maximum(m, jnp.max(s, axis=-1, keepdims=True))
            p = jnp.exp(s - m_new)
            acc = jnp.exp(m - m_new) * acc + _dot(p.astype(BF16), v)
            return m_new, acc

        init = (jnp.full((tq, 1), NEG_BIAS, F32), jnp.zeros((tq, HEAD_PAD), F32))
        _, acc = lax.fori_loop(0, nk, chunk, init)
        outs.append(acc[:, :V_HEAD] / acc[:, V_HEAD:V_HEAD + 1])
    o_ref[0] = jnp.concatenate(outs, axis=-1).astype(o_ref.dtype)


def _mla_attention(mq, mk, mv, *, tq, q_block0, n_q, k_block, k_block0, tk):
    b, lt, nw = mq.shape
    body = functools.partial(_mla_body, tk=tk, nk=k_block // tk)
    return pl.pallas_call(
        body,
        grid=(b, n_q),
        in_specs=[pl.BlockSpec((1, tq, nw), lambda bi, i: (bi, q_block0 + i, 0)),
                  pl.BlockSpec((1, k_block, nw), lambda bi, i: (bi, k_block0, 0)),
                  pl.BlockSpec((1, k_block, nw), lambda bi, i: (bi, k_block0, 0))],
        out_specs=pl.BlockSpec((1, tq, MLA_W), lambda bi, i: (bi, i, 0)),
        out_shape=jax.ShapeDtypeStruct((b, n_q * tq, MLA_W), BF16),
        compiler_params=_params(("parallel", "arbitrary")),
        name="mla_attention",
    )(mq, mk, mv)


def _route(logits, rb):
    aff = _sigmoid(logits)
    sel = aff + rb
    sc = [sel[:, j:j + 1] for j in range(N_EXPERTS)]
    ac = [aff[:, j:j + 1] for j in range(N_EXPERTS)]
    gscore = []
    for g in range(N_GROUPS):
        a, b, c, d = sc[4 * g:4 * g + 4]
        hi1, lo1 = jnp.maximum(a, b), jnp.minimum(a, b)
        hi2, lo2 = jnp.maximum(c, d), jnp.minimum(c, d)
        gscore.append(jnp.maximum(hi1, hi2) + jnp.maximum(jnp.minimum(hi1, hi2), jnp.maximum(lo1, lo2)))
    best = jnp.zeros_like(gscore[0], dtype=jnp.int32)
    bval = gscore[0]
    for g in range(1, N_GROUPS):
        upd = gscore[g] > bval
        best = jnp.where(upd, g, best)
        bval = jnp.where(upd, gscore[g], bval)

    def pick(cols, j):
        out = cols[j]
        for g in range(1, N_GROUPS):
            out = jnp.where(best == g, cols[4 * g + j], out)
        return out

    v = [pick(sc, j) for j in range(EXPERTS_PER_GROUP)]
    av = [pick(ac, j) for j in range(EXPERTS_PER_GROUP)]
    chosen = []
    for i in range(EXPERTS_PER_GROUP):
        rank = jnp.zeros_like(best)
        for j in range(EXPERTS_PER_GROUP):
            if j == i:
                continue
            beats = (v[j] > v[i]) if j > i else (v[j] >= v[i])
            rank = rank + beats.astype(jnp.int32)
        chosen.append(rank < 2)
    wsum = functools.reduce(jnp.add, [jnp.where(ch, a, 0.0) for ch, a in zip(chosen, av)])
    wn = [jnp.where(ch, a / wsum, 0.0) for ch, a in zip(chosen, av)]
    lane = lax.broadcasted_iota(jnp.int32, logits.shape, 1)
    comb = jnp.zeros(logits.shape, F32)
    for e in range(N_EXPERTS):
        w_e = jnp.where(best == e // EXPERTS_PER_GROUP, wn[e % EXPERTS_PER_GROUP], 0.0)
        comb = jnp.where(lane == e, w_e, comb)
    return comb


def _proj_out_body(ya_ref, hf_ref, hr_ref, gg_ref, yc_ref, x_ref, wa_ref, wb_ref, wc_ref, g1_ref, g2_ref,
                   gate_ref, shift_ref, scale_ref, rw_ref, rb_ref, xo_ref, h2_ref, comb_ref):
    yb = (hf_ref[0, 0] + hr_ref[0, 0]) * gg_ref[...].astype(F32)
    y = _dot(ya_ref[...], wa_ref[...]) + _dot(yb.astype(BF16), wb_ref[...]) + _dot(yc_ref[...], wc_ref[...])
    x = x_ref[...] + gate_ref[0] * _rms(y, g1_ref[...])
    xo_ref[...] = x
    h2 = (_rms(x, g2_ref[...]) * (1.0 + scale_ref[0]) + shift_ref[0]).astype(BF16)
    h2_ref[...] = h2
    comb_ref[...] = _route(_dot(h2, rw_ref[...]), rb_ref[...])


def _proj_out(ya, hscan, gg, yc, x, w_a, w_b, w_c, g1, g2, gate, shift, scale, rw, rb, *, tiles_per_sample,
              n_lat_tiles, n_batch):
    t, d = x.shape
    nt = t // TM

    def seg(i):
        return jnp.where(i % tiles_per_sample >= n_lat_tiles, n_batch, i // tiles_per_sample)

    row = lambda n: pl.BlockSpec((TM, n), lambda i: (i, 0))
    full = lambda a: pl.BlockSpec(a.shape, lambda i: (0,) * a.ndim)
    modspec = pl.BlockSpec((1, 1, d), lambda i: (seg(i), 0, 0))
    scan_spec = lambda dd: pl.BlockSpec((1, 1, TM, LRU_W),
                                        lambda i: (dd, i // tiles_per_sample, i % tiles_per_sample, 0))
    return pl.pallas_call(
        _proj_out_body,
        grid=(nt,),
        in_specs=[row(NA_W), scan_spec(0), scan_spec(1), row(LRU_W), row(MLA_W), row(d),
                  full(w_a), full(w_b), full(w_c), full(g1), full(g2), modspec, modspec, modspec,
                  full(rw), full(rb)],
        out_specs=[row(d), row(d), row(LANES)],
        out_shape=[jax.ShapeDtypeStruct((t, d), F32),
                   jax.ShapeDtypeStruct((t, d), BF16),
                   jax.ShapeDtypeStruct((t, LANES), F32)],
        compiler_params=_params(("parallel",)),
        name="proj_out",
    )(ya, hscan, hscan, gg, yc, x, w_a, w_b, w_c, g1, g2, gate, shift, scale, rw, rb)


def _moe_dense_body(h_ref, comb_ref, wg_ref, wu_ref, wd_ref, o_ref, acc_ref):
    e = pl.program_id(1)

    @pl.when(e == 0)
    def _():
        acc_ref[...] = jnp.zeros_like(acc_ref)

    h = h_ref[...]
    a = _dot(h, wg_ref[0])
    he = (a * _sigmoid(a)) * _dot(h, wu_ref[0])
    y = _dot(he.astype(BF16), wd_ref[0])
    lane = lax.broadcasted_iota(jnp.int32, comb_ref.shape, 1)
    cw = jnp.sum(jnp.where(lane == e, comb_ref[...], 0.0), axis=-1, keepdims=True)
    acc_ref[...] += cw * y

    @pl.when(e == pl.num_programs(1) - 1)
    def _():
        o_ref[...] = acc_ref[...]


def _moe_dense(h2, comb, w_gate, w_up, w_down, *, tm):
    t, d = h2.shape
    n_e, _, de = w_gate.shape
    return pl.pallas_call(
        _moe_dense_body,
        grid=(t // tm, n_e),
        in_specs=[pl.BlockSpec((tm, d), lambda i, e: (i, 0)),
                  pl.BlockSpec((tm, LANES), lambda i, e: (i, 0)),
                  pl.BlockSpec((1, d, de), lambda i, e: (e, 0, 0)),
                  pl.BlockSpec((1, d, de), lambda i, e: (e, 0, 0)),
                  pl.BlockSpec((1, de, d), lambda i, e: (e, 0, 0))],
        out_specs=pl.BlockSpec((tm, d), lambda i, e: (i, 0)),
        out_shape=jax.ShapeDtypeStruct((t, d), F32),
        scratch_shapes=[pltpu.VMEM((tm, d), F32)],
        compiler_params=_params(("parallel", "arbitrary")),
        name="moe_dense",
    )(h2, comb, w_gate, w_up, w_down)


def _residual_body(x_ref, f_ref, gate_ref, g_ref, o_ref):
    o_ref[...] = x_ref[...] + gate_ref[0] * _rms(f_ref[...], g_ref[...])


def _residual(x, f, gate, g, *, tiles_per_sample, n_lat_tiles, n_batch):
    t, d = x.shape

    def seg(i):
        return jnp.where(i % tiles_per_sample >= n_lat_tiles, n_batch, i // tiles_per_sample)

    row = pl.BlockSpec((TM, d), lambda i: (i, 0))
    return pl.pallas_call(
        _residual_body,
        grid=(t // TM,),
        in_specs=[row, row, pl.BlockSpec((1, 1, d), lambda i: (seg(i), 0, 0)), pl.BlockSpec(g.shape, lambda i: (0, 0))],
        out_specs=row,
        out_shape=jax.ShapeDtypeStruct((t, d), F32),
        compiler_params=_params(("parallel",)),
        name="moe_residual",
    )(x, f, gate, g)


def _rope_tables(seq, ctx):
    t = np.arange(seq)
    axis_dim = QK_ROPE // 2
    inv = jnp.asarray(ROPE_BASE, F32) ** (-jnp.arange(0, axis_dim, 2, dtype=F32) / axis_dim)
    ang_r = jnp.asarray(t // GRID_W, F32)[:, None] * inv
    ang_c = jnp.asarray(t % GRID_W, F32)[:, None] * inv
    cos = jnp.concatenate([jnp.cos(ang_r), jnp.cos(ang_r), jnp.cos(ang_c), jnp.cos(ang_c)], axis=-1)
    sin = jnp.concatenate([-jnp.sin(ang_r), jnp.sin(ang_r), -jnp.sin(ang_c), jnp.sin(ang_c)], axis=-1)
    pad = HEAD_PAD - QK_NOPE - QK_ROPE
    cos = jnp.concatenate([jnp.ones((seq, QK_NOPE), F32), cos, jnp.ones((seq, pad), F32)], axis=-1)
    sin = jnp.concatenate([jnp.zeros((seq, QK_NOPE), F32), sin, jnp.zeros((seq, pad), F32)], axis=-1)
    cos = jnp.concatenate([cos, jnp.ones((ctx, HEAD_PAD), F32)], axis=0)
    sin = jnp.concatenate([sin, jnp.zeros((ctx, HEAD_PAD), F32)], axis=0)
    return cos, sin


_ROPE_SWAP = np.concatenate([np.arange(8, 16), np.arange(0, 8), np.arange(24, 32), np.arange(16, 24)])


def _layout_w_in(w_in):
    d = w_in.shape[0]
    z = lambda n: jnp.zeros((d, n), w_in.dtype)
    kr = w_in[:, _C_KR:_C_KR + QK_ROPE]
    pad = HEAD_PAD - QK_NOPE - QK_ROPE
    return jnp.concatenate([w_in[:, :_C_KR], z(QK_NOPE), kr, z(pad), z(QK_NOPE), kr[:, _ROPE_SWAP], z(pad)],
                           axis=1).astype(BF16)


def _layout_w_q(w_q):
    r = w_q.shape[0]
    wh = w_q.reshape(r, MLA_HEADS, QK_NOPE + QK_ROPE)
    pad = HEAD_PAD - QK_NOPE - QK_ROPE
    full = jnp.concatenate([wh, jnp.zeros((r, MLA_HEADS, pad), w_q.dtype)], axis=-1)
    swap = jnp.concatenate([jnp.zeros((r, MLA_HEADS, QK_NOPE), w_q.dtype), wh[:, :, QK_NOPE + _ROPE_SWAP],
                            jnp.zeros((r, MLA_HEADS, pad), w_q.dtype)], axis=-1)
    return jnp.concatenate([full.reshape(r, -1), swap.reshape(r, -1)], axis=1).astype(BF16)


def _layout_w_kv(w_kv):
    r = w_kv.shape[0]
    wh = w_kv.reshape(r, MLA_HEADS, QK_NOPE + V_HEAD)
    zk = jnp.zeros((r, MLA_HEADS, HEAD_PAD - QK_NOPE), w_kv.dtype)
    zv = jnp.zeros((r, MLA_HEADS, HEAD_PAD - V_HEAD), w_kv.dtype)
    k = jnp.concatenate([wh[:, :, :QK_NOPE], zk], axis=-1).reshape(r, -1)
    v = jnp.concatenate([wh[:, :, QK_NOPE:], zv], axis=-1).reshape(r, -1)
    return jnp.concatenate([k, v], axis=1).astype(BF16)


def _block_diag(w):
    nd, nb, k, _ = w.shape
    eye = jnp.eye(nb, dtype=w.dtype)
    return (w[:, :, :, None, :] * eye[None, :, None, :, None]).reshape(nd, nb * k, nb * k).astype(BF16)


def _pick_chunk(n, candidates):
    for c in candidates:
        if n % c == 0:
            return c
    raise ValueError(f"no chunk size for {n}")


def kernel(x, c, ctx, c_ctx, w_ada, b_ada, g_norm, w_in, w_out, na_rpb, conv_w, conv_b, lru_w_a, lru_b_a, lru_w_i,
           lru_b_i, lru_lam, mla_g_q, mla_w_q, mla_g_kv, mla_w_kv, router_w, router_b, exp_w_gate, exp_w_up,
           exp_w_down):
    b, seq, d = x.shape
    n_ctx = ctx.shape[1]
    depth = w_ada.shape[0]
    lt = seq + n_ctx
    assert d == D_MODEL and n_ctx == TM and seq % TM == 0 and seq // GRID_W >= WIN_ROWS and b + 1 <= 8
    tiles_per_sample = lt // TM
    n_lat_tiles = seq // TM
    seg_kw = dict(tiles_per_sample=tiles_per_sample, n_lat_tiles=n_lat_tiles, n_batch=b)

    xa = jnp.concatenate([x, ctx], axis=1).reshape(b * lt, d)

    c_rows = jnp.concatenate([c, c_ctx[None, :], jnp.zeros((8 - b - 1, d), F32)], axis=0)
    mod = _ada_tables(c_rows, w_ada, b_ada).reshape(depth, 8, N_MOD, 1, d)

    cos_t, sin_t = _rope_tables(seq, n_ctx)
    ones_row = jnp.zeros((MLA_HEADS, HEAD_PAD), F32).at[:, V_HEAD].set(1.0).reshape(1, -1)
    rw = jnp.concatenate([router_w, jnp.zeros((d, LANES - N_EXPERTS), F32)], axis=1).astype(BF16)
    rb = jnp.concatenate([router_b, jnp.zeros((LANES - N_EXPERTS,), F32)])[None, :]
    tk = _pick_chunk(lt, (768, 512, 256))
    tq = _pick_chunk(seq, (512, 256))
    moe_tm = _pick_chunk(b * lt, (768, 512, 256))

    for l in range(depth):
        last = l == depth - 1
        m = lambda k: mod[l, :b + 1, k]
        g = g_norm[l]
        qkv, u, gg, mq, mk, mv = _proj_in(
            xa, m(0), m(1), g[0:1], _layout_w_in(w_in[l]), mla_g_q[l][None, :], _layout_w_q(mla_w_q[l]),
            mla_g_kv[l][None, :], _layout_w_kv(mla_w_kv[l]), cos_t, sin_t, ones_row, **seg_kw)

        ya = _na_attention(qkv.reshape(b, lt, -1), _na_bias_tables(na_rpb[l]), seq=seq, ctx=n_ctx)
        hscan = _lru_scan(u.reshape(b, lt, -1), conv_w[l], conv_b[l][None, :], _block_diag(lru_w_a[l]),
                          lru_b_a[l][:, None, :], _block_diag(lru_w_i[l]), lru_b_i[l][:, None, :],
                          lru_lam[l][:, None, :], seq=seq, ctx=n_ctx)
        mq3, mk3, mv3 = (a.reshape(b, lt, -1) for a in (mq, mk, mv))
        yc = _mla_attention(mq3, mk3, mv3, tq=tq, q_block0=0, n_q=seq // tq, k_block=lt, k_block0=0, tk=tk)
        if last:
            yc = jnp.concatenate([yc, jnp.zeros((b, n_ctx, MLA_W), BF16)], axis=1)
        else:
            yc_ctx = _mla_attention(mq3, mk3, mv3, tq=n_ctx, q_block0=seq // n_ctx, n_q=1, k_block=n_ctx,
                                    k_block0=seq // n_ctx, tk=n_ctx)
            yc = jnp.concatenate([yc, yc_ctx], axis=1)

        wo = w_out[l].astype(BF16)
        xa, h2, comb = _proj_out(
            ya.reshape(b * lt, -1), hscan, gg, yc.reshape(b * lt, -1), xa,
            wo[:NA_W], wo[NA_W:NA_W + LRU_W], wo[NA_W + LRU_W:], g[1:2], g[2:3], m(2), m(3), m(4), rw, rb, **seg_kw)

        f = _moe_dense(h2, comb, exp_w_gate[l].astype(BF16), exp_w_up[l].astype(BF16), exp_w_down[l].astype(BF16),
                       tm=moe_tm)
        xa = _residual(xa, f, m(5), g[3:4], **seg_kw)

    return xa.reshape(b, lt, d)[:, :seq]
```

```python
import functools

import numpy as np
import jax
import jax.numpy as jnp
from jax import lax
from jax.experimental import pallas as pl
from jax.experimental.pallas import tpu as pltpu

F32 = jnp.float32
BF16 = jnp.bfloat16

D_MODEL = 1024
GRID_W = 64
NA_HEADS = 4
NA_HEAD_DIM = 64
NA_W = NA_HEADS * NA_HEAD_DIM
WIN_ROWS = 8
WIN_COLS = 16
LRU_W = 512
LRU_BLOCKS = 8
CONV_W = 4
LRU_C = 8.0
MLA_HEADS = 4
Q_LORA = 256
KV_LORA = 128
QK_NOPE = 64
QK_ROPE = 32
V_HEAD = 64
MLA_W = MLA_HEADS * V_HEAD
ROPE_BASE = 10000.0
N_EXPERTS = 16
N_GROUPS = 4
EXPERTS_PER_GROUP = 4
D_EXPERT = 512
RMS_EPS = 1e-6
N_MOD = 6

LANES = 128
TM = 256
HEAD_PAD = 128
NEG_BIAS = -1e30
VMEM_LIMIT = 56 * 1024 * 1024

_C_QA, _C_KA, _C_VA = 0, 256, 512
_C_U = 768
_C_GATE = 1280
_C_CQ = 1792
_C_CKV = 2048
_C_KR = 2176
_C_KRP = 2304
_IN_COLS = 2432


def _params(sem, vmem=VMEM_LIMIT):
    return pltpu.CompilerParams(dimension_semantics=sem, vmem_limit_bytes=vmem)


def _sigmoid(v):
    return 1.0 / (1.0 + jnp.exp(-v))


def _rms(v, g):
    return v * lax.rsqrt(jnp.mean(v * v, axis=-1, keepdims=True) + RMS_EPS) * g


def _dot(a, b):
    return jnp.dot(a, b, preferred_element_type=F32)


def _dot_nt(a, b):
    return lax.dot_general(a, b, (((1,), (1,)), ((), ())), preferred_element_type=F32)


def _ada_body(c_ref, w_ref, b_ref, o_ref):
    c = c_ref[...]
    sc = c * _sigmoid(c)
    o_ref[0] = _dot(sc.astype(BF16), w_ref[0].astype(BF16)) + b_ref[0]


def _ada_tables(c_rows, w_ada, b_ada):
    depth, d, n = w_ada.shape
    tn = 1536
    return pl.pallas_call(
        _ada_body,
        grid=(depth, n // tn),
        in_specs=[pl.BlockSpec((8, d), lambda l, j: (0, 0)),
                  pl.BlockSpec((1, d, tn), lambda l, j: (l, 0, j)),
                  pl.BlockSpec((1, 1, tn), lambda l, j: (l, 0, j))],
        out_specs=pl.BlockSpec((1, 8, tn), lambda l, j: (l, 0, j)),
        out_shape=jax.ShapeDtypeStruct((depth, 8, n), F32),
        compiler_params=_params(("parallel", "parallel")),
        name="ada_tables",
    )(c_rows, w_ada, b_ada.reshape(depth, 1, n))


def _proj_in_body(x_ref, shift_ref, scale_ref, g_ref, w_ref, gq_ref, wq_ref, gkv_ref, wkv_ref,
                  cos_ref, sin_ref, ones_ref,
                  qkv_ref, u_ref, gg_ref, mq_ref, mk_ref, mv_ref, *, mla_scale):
    x = x_ref[...]
    h = _rms(x, g_ref[...]) * (1.0 + scale_ref[0]) + shift_ref[0]
    z = _dot(h.astype(BF16), w_ref[...])
    qkv_ref[:, 0:NA_W] = (z[:, _C_QA:_C_KA] * (NA_HEAD_DIM ** -0.5)).astype(BF16)
    qkv_ref[:, NA_W:3 * NA_W] = z[:, _C_KA:_C_U].astype(BF16)
    u_ref[...] = z[:, _C_U:_C_GATE]
    gg_ref[...] = jax.nn.gelu(z[:, _C_GATE:_C_CQ]).astype(BF16)

    cos = cos_ref[...]
    sin = sin_ref[...]
    cos4 = jnp.concatenate([cos] * MLA_HEADS, axis=-1)
    sin4 = jnp.concatenate([sin] * MLA_HEADS, axis=-1)
    nw = MLA_HEADS * HEAD_PAD

    nq = _rms(z[:, _C_CQ:_C_CKV], gq_ref[...])
    q2 = _dot(nq.astype(BF16), wq_ref[...])
    mq_ref[...] = ((q2[:, :nw] * cos4 + q2[:, nw:] * sin4) * mla_scale).astype(BF16)

    nkv = _rms(z[:, _C_CKV:_C_KR], gkv_ref[...])
    kv2 = _dot(nkv.astype(BF16), wkv_ref[...])
    k_rope = z[:, _C_KR:_C_KRP] * cos + z[:, _C_KRP:_IN_COLS] * sin
    mk_ref[...] = (kv2[:, :nw] + jnp.concatenate([k_rope] * MLA_HEADS, axis=-1)).astype(BF16)
    mv_ref[...] = (kv2[:, nw:] + ones_ref[...]).astype(BF16)


def _proj_in(x, shift, scale, g, w_big, gq, wq2, gkv, wkv2, cos_t, sin_t, ones_row, *, tiles_per_sample, n_lat_tiles,
             n_batch):
    t, d = x.shape
    nt = t // TM
    nw = MLA_HEADS * HEAD_PAD

    def seg(i):
        return jnp.where(i % tiles_per_sample >= n_lat_tiles, n_batch, i // tiles_per_sample)

    row = lambda n: pl.BlockSpec((TM, n), lambda i: (i, 0))
    full = lambda a: pl.BlockSpec(a.shape, lambda i: (0,) * a.ndim)
    modspec = pl.BlockSpec((1, 1, d), lambda i: (seg(i), 0, 0))
    tabspec = pl.BlockSpec((TM, HEAD_PAD), lambda i: (i % tiles_per_sample, 0))
    mla_scale = (QK_NOPE + QK_ROPE) ** -0.5
    return pl.pallas_call(
        functools.partial(_proj_in_body, mla_scale=mla_scale),
        grid=(nt,),
        in_specs=[row(d), modspec, modspec, full(g), full(w_big), full(gq), full(wq2), full(gkv), full(wkv2),
                  tabspec, tabspec, full(ones_row)],
        out_specs=[row(3 * NA_W), row(LRU_W), row(LRU_W), row(nw), row(nw), row(nw)],
        out_shape=[jax.ShapeDtypeStruct((t, 3 * NA_W), BF16),
                   jax.ShapeDtypeStruct((t, LRU_W), F32),
                   jax.ShapeDtypeStruct((t, LRU_W), BF16),
                   jax.ShapeDtypeStruct((t, nw), BF16),
                   jax.ShapeDtypeStruct((t, nw), BF16),
                   jax.ShapeDtypeStruct((t, nw), BF16)],
        compiler_params=_params(("parallel",)),
        name="proj_in",
    )(x, shift, scale, g, w_big, gq, wq2, gkv, wkv2, cos_t, sin_t, ones_row)


def _head_block_mask():
    r = lax.broadcasted_iota(jnp.int32, (NA_W, NA_W), 0) // NA_HEAD_DIM
    c = lax.broadcasted_iota(jnp.int32, (NA_W, NA_W), 1) // NA_HEAD_DIM
    return r == c


def _na_attend(q, parts, mask):
    qbig = jnp.where(mask, jnp.concatenate([q] * NA_HEADS, axis=0), jnp.zeros((), q.dtype))
    scores = []
    for k, _, bias in parts:
        s = _dot_nt(qbig, k)
        if bias is not None:
            s = s + bias
        scores.append(s)
    m = functools.reduce(jnp.maximum, [jnp.max(s, axis=-1, keepdims=True) for s in scores])
    ps = [jnp.exp(s - m) for s in scores]
    l = functools.reduce(jnp.add, [jnp.sum(p, axis=-1, keepdims=True) for p in ps])
    o = functools.reduce(jnp.add, [_dot(p.astype(BF16), v) for p, (_, v, _) in zip(ps, parts)])
    o = jnp.where(mask, o / l, 0.0)
    out = o[0:NA_HEAD_DIM]
    for h in range(1, NA_HEADS):
        out = out + o[h * NA_HEAD_DIM:(h + 1) * NA_HEAD_DIM]
    return out


def _na_body(q_ref, k_ref, v_ref, bias_ref, o_ref, *, rows, rows_per_step, n_lat_steps, seq, ctx):
    i = pl.program_id(1)
    mask = _head_block_mask()
    kc = k_ref[0, seq:seq + ctx, :]
    vc = v_ref[0, seq:seq + ctx, :]

    @pl.when(i < n_lat_steps)
    def _():
        for j in range(rows_per_step):
            r = i * rows_per_step + j
            rs = jnp.clip(r - WIN_ROWS // 2, 0, rows - WIN_ROWS)
            off = rs - r + (WIN_ROWS - 1)
            start = pl.multiple_of(rs * GRID_W, GRID_W)
            kw = k_ref[0, pl.ds(start, WIN_ROWS * GRID_W), :]
            vw = v_ref[0, pl.ds(start, WIN_ROWS * GRID_W), :]
            q = q_ref[0, j * GRID_W:(j + 1) * GRID_W, :]
            out = _na_attend(q, [(kw, vw, bias_ref[off]), (kc, vc, None)], mask)
            o_ref[0, j * GRID_W:(j + 1) * GRID_W, :] = out.astype(o_ref.dtype)

    @pl.when(i >= n_lat_steps)
    def _():
        for j in range(rows_per_step):
            q = q_ref[0, j * GRID_W:(j + 1) * GRID_W, :]
            out = _na_attend(q, [(kc, vc, None)], mask)
            o_ref[0, j * GRID_W:(j + 1) * GRID_W, :] = out.astype(o_ref.dtype)


def _na_attention(qkv, bias, *, seq, ctx):
    b, lt, _ = qkv.shape
    rows = seq // GRID_W
    rows_per_step = TM // GRID_W
    n_steps = lt // TM
    n_lat_steps = seq // TM
    body = functools.partial(_na_body, rows=rows, rows_per_step=rows_per_step, n_lat_steps=n_lat_steps,
                             seq=seq, ctx=ctx)
    return pl.pallas_call(
        body,
        grid=(b, n_steps),
        in_specs=[pl.BlockSpec((1, TM, NA_W), lambda bi, i: (bi, i, 0)),
                  pl.BlockSpec((1, lt, NA_W), lambda bi, i: (bi, 0, 1)),
                  pl.BlockSpec((1, lt, NA_W), lambda bi, i: (bi, 0, 2)),
                  pl.BlockSpec(bias.shape, lambda bi, i: (0, 0, 0))],
        out_specs=pl.BlockSpec((1, TM, NA_W), lambda bi, i: (bi, i, 0)),
        out_shape=jax.ShapeDtypeStruct((b, lt, NA_W), BF16),
        compiler_params=_params(("parallel", "arbitrary")),
        name="na_attention",
    )(qkv, qkv, qkv, bias)


def _na_bias_tables(rpb):
    q = np.arange(GRID_W)
    cs = np.clip(q - WIN_COLS // 2, 0, GRID_W - WIN_COLS)
    c = np.arange(GRID_W)
    inside = (c[None, :] >= cs[:, None]) & (c[None, :] < cs[:, None] + WIN_COLS)
    dc = c[None, :] - q[:, None] + (WIN_COLS - 1)
    place = ((np.arange(2 * WIN_COLS - 1)[:, None, None] == dc[None]) & inside[None]).astype(np.float32)
    win = jnp.stack([rpb[:, o:o + WIN_ROWS, :] for o in range(WIN_ROWS)]).astype(F32)
    vals = jnp.einsum('ohik,kqc->ohqic', win, jnp.asarray(place), precision=lax.Precision.HIGHEST)
    vals = jnp.where(inside[None, None, :, None, :], vals, NEG_BIAS)
    return vals.reshape(WIN_ROWS, NA_HEADS * GRID_W, WIN_ROWS * GRID_W)


def _softplus(v):
    return jnp.maximum(v, 0.0) + jnp.log1p(jnp.exp(-jnp.abs(v)))


def _lru_body(up_ref, uc_ref, un_ref, cw_ref, cb_ref, wa_ref, ba_ref, wi_ref, bi_ref, lam_ref,
              o_ref, a_buf, x_buf, st_ref, *, n_lat, chunk, n_batch):
    d = pl.program_id(0)
    i = pl.program_id(1)
    c = _lru_chunk(d, i, n_lat)
    prev_ok = jnp.logical_and(c >= 1, c < n_lat)
    next_ok = c <= n_lat - 2

    row = lax.broadcasted_iota(jnp.int32, (chunk, 1), 0)
    cw = cw_ref[...]
    log_decay = _softplus(-lam_ref[0])
    for b in range(n_batch):
        ub = uc_ref[b]
        p_row = jnp.where(prev_ok, up_ref[b, 7:8, :], 0.0)
        n0 = jnp.where(next_ok, un_ref[b, 0:1, :], 0.0)
        n1 = jnp.where(next_ok, un_ref[b, 1:2, :], 0.0)
        um1 = jnp.where(row == 0, p_row, pltpu.roll(ub, 1, 0))
        up1 = jnp.where(row == chunk - 1, n0, pltpu.roll(ub, chunk - 1, 0))
        up2 = jnp.where(row == chunk - 1, n1, jnp.where(row == chunk - 2, n0, pltpu.roll(ub, chunk - 2, 0)))
        cv = um1 * cw[0:1] + cb_ref[...]
        cv = cv + ub * cw[1:2]
        cv = cv + up1 * cw[2:3]
        cv = cv + up2 * cw[3:4]
        cv16 = cv.astype(BF16)
        r = _sigmoid(_dot(cv16, wa_ref[0]) + ba_ref[0])
        gi = _sigmoid(_dot(cv16, wi_ref[0]) + bi_ref[0])
        log_a = (-LRU_C * r) * log_decay
        a = jnp.exp(log_a)
        a_buf[b] = a
        x_buf[b] = jnp.sqrt(1.0 - a * a) * (gi * cv)

    @pl.when(i == 0)
    def _():
        st_ref[...] = jnp.zeros_like(st_ref)

    def step(t, hs):
        tt = jnp.where(d == 0, t, chunk - 1 - t)
        new = []
        for b in range(n_batch):
            h = a_buf[b, pl.ds(tt, 1), :] * hs[b] + x_buf[b, pl.ds(tt, 1), :]
            o_ref[0, b, pl.ds(tt, 1), :] = h
            new.append(h)
        return tuple(new)

    hs = lax.fori_loop(0, chunk, step, tuple(st_ref[b, 0:1, :] for b in range(n_batch)), unroll=8)
    for b in range(n_batch):
        st_ref[b, 0:1, :] = hs[b]


def _lru_chunk(d, i, n_lat):
    return jnp.where(i == 0, n_lat, jnp.where(d == 0, i - 1, n_lat - i))


def _lru_scan(u, conv_w, conv_b, w_a, b_a, w_i, b_i, lam, *, seq, ctx):
    b, lt, w = u.shape
    chunk = ctx
    n_lat = seq // chunk
    n_chunks = lt // chunk
    sub = chunk // 8
    n_sub = lt // 8

    cur = lambda d, i: (0, _lru_chunk(d, i, n_lat), 0)
    prev = lambda d, i: (0, jnp.maximum(_lru_chunk(d, i, n_lat) * sub - 1, 0), 0)
    nxt = lambda d, i: (0, jnp.minimum((_lru_chunk(d, i, n_lat) + 1) * sub, n_sub - 1), 0)
    per_dir = lambda a: pl.BlockSpec((1,) + a.shape[1:], lambda d, i: (d,) + (0,) * (a.ndim - 1))
    full = lambda a: pl.BlockSpec(a.shape, lambda d, i: (0,) * a.ndim)
    body = functools.partial(_lru_body, n_lat=n_lat, chunk=chunk, n_batch=b)
    return pl.pallas_call(
        body,
        grid=(2, n_chunks),
        in_specs=[pl.BlockSpec((b, 8, w), prev), pl.BlockSpec((b, chunk, w), cur), pl.BlockSpec((b, 8, w), nxt),
                  full(conv_w), full(conv_b), per_dir(w_a), per_dir(b_a), per_dir(w_i), per_dir(b_i), per_dir(lam)],
        out_specs=pl.BlockSpec((1, b, chunk, w), lambda d, i: (d, 0, _lru_chunk(d, i, n_lat), 0)),
        out_shape=jax.ShapeDtypeStruct((2, b, lt, w), F32),
        scratch_shapes=[pltpu.VMEM((b, chunk, w), F32), pltpu.VMEM((b, chunk, w), F32), pltpu.VMEM((b, 8, w), F32)],
        compiler_params=_params(("arbitrary", "arbitrary")),
        name="lru_scan",
    )(u, u, u, conv_w, conv_b, w_a, b_a, w_i, b_i, lam)


def _mla_body(q_ref, k_ref, v_ref, o_ref, *, tk, nk):
    tq = q_ref.shape[1]
    outs = []
    for h in range(MLA_HEADS):
        cols = slice(h * HEAD_PAD, (h + 1) * HEAD_PAD)
        q = q_ref[0, :, cols]

        def chunk(ci, carry, cols=cols, q=q):
            m, acc = carry
            start = pl.multiple_of(ci * tk, tk)
            k = k_ref[0, pl.ds(start, tk), cols]
            v = v_ref[0, pl.ds(start, tk), cols]
            s = _dot_nt(q, k)
            m_new = jnp.maximum(m, jnp.max(s, axis=-1, keepdims=True))
            p = jnp.exp(s - m_new)
            acc = jnp.exp(m - m_new) * acc + _dot(p.astype(BF16), v)
            return m_new, acc

        init = (jnp.full((tq, 1), NEG_BIAS, F32), jnp.zeros((tq, HEAD_PAD), F32))
        _, acc = lax.fori_loop(0, nk, chunk, init)
        outs.append(acc[:, :V_HEAD] / acc[:, V_HEAD:V_HEAD + 1])
    o_ref[0] = jnp.concatenate(outs, axis=-1).astype(o_ref.dtype)


def _mla_attention(mq, mk, mv, *, tq, q_block0, n_q, k_block, k_block0, tk):
    b, lt, nw = mq.shape
    body = functools.partial(_mla_body, tk=tk, nk=k_block // tk)
    return pl.pallas_call(
        body,
        grid=(b, n_q),
        in_specs=[pl.BlockSpec((1, tq, nw), lambda bi, i: (bi, q_block0 + i, 0)),
                  pl.BlockSpec((1, k_block, nw), lambda bi, i: (bi, k_block0, 0)),
                  pl.BlockSpec((1, k_block, nw), lambda bi, i: (bi, k_block0, 0))],
        out_specs=pl.BlockSpec((1, tq, MLA_W), lambda bi, i: (bi, i, 0)),
        out_shape=jax.ShapeDtypeStruct((b, n_q * tq, MLA_W), BF16),
        compiler_params=_params(("parallel", "arbitrary")),
        name="mla_attention",
    )(mq, mk, mv)


def _route(logits, rb):
    aff = _sigmoid(logits)
    sel = aff + rb
    sc = [sel[:, j:j + 1] for j in range(N_EXPERTS)]
    ac = [aff[:, j:j + 1] for j in range(N_EXPERTS)]
    gscore = []
    for g in range(N_GROUPS):
        a, b, c, d = sc[4 * g:4 * g + 4]
        hi1, lo1 = jnp.maximum(a, b), jnp.minimum(a, b)
        hi2, lo2 = jnp.maximum(c, d), jnp.minimum(c, d)
        gscore.append(jnp.maximum(hi1, hi2) + jnp.maximum(jnp.minimum(hi1, hi2), jnp.maximum(lo1, lo2)))
    best = jnp.zeros_like(gscore[0], dtype=jnp.int32)
    bval = gscore[0]
    for g in range(1, N_GROUPS):
        upd = gscore[g] > bval
        best = jnp.where(upd, g, best)
        bval = jnp.where(upd, gscore[g], bval)

    def pick(cols, j):
        out = cols[j]
        for g in range(1, N_GROUPS):
            out = jnp.where(best == g, cols[4 * g + j], out)
        return out

    v = [pick(sc, j) for j in range(EXPERTS_PER_GROUP)]
    av = [pick(ac, j) for j in range(EXPERTS_PER_GROUP)]
    chosen = []
    for i in range(EXPERTS_PER_GROUP):
        rank = jnp.zeros_like(best)
        for j in range(EXPERTS_PER_GROUP):
            if j == i:
                continue
            beats = (v[j] > v[i]) if j > i else (v[j] >= v[i])
            rank = rank + beats.astype(jnp.int32)
        chosen.append(rank < 2)
    wsum = functools.reduce(jnp.add, [jnp.where(ch, a, 0.0) for ch, a in zip(chosen, av)])
    wn = [jnp.where(ch, a / wsum, 0.0) for ch, a in zip(chosen, av)]
    lane = lax.broadcasted_iota(jnp.int32, logits.shape, 1)
    comb = jnp.zeros(logits.shape, F32)
    for e in range(N_EXPERTS):
        w_e = jnp.where(best == e // EXPERTS_PER_GROUP, wn[e % EXPERTS_PER_GROUP], 0.0)
        comb = jnp.where(lane == e, w_e, comb)
    return comb


def _proj_out_body(ya_ref, hf_ref, hr_ref, gg_ref, yc_ref, x_ref, wa_ref, wb_ref, wc_ref, g1_ref, g2_ref,
                   gate_ref, shift_ref, scale_ref, rw_ref, rb_ref, xo_ref, h2_ref, comb_ref):
    yb = (hf_ref[0, 0] + hr_ref[0, 0]) * gg_ref[...].astype(F32)
    y = _dot(ya_ref[...], wa_ref[...]) + _dot(yb.astype(BF16), wb_ref[...]) + _dot(yc_ref[...], wc_ref[...])
    x = x_ref[...] + gate_ref[0] * _rms(y, g1_ref[...])
    xo_ref[...] = x
    h2 = (_rms(x, g2_ref[...]) * (1.0 + scale_ref[0]) + shift_ref[0]).astype(BF16)
    h2_ref[...] = h2
    comb_ref[...] = _route(_dot(h2, rw_ref[...]), rb_ref[...])


def _proj_out(ya, hscan, gg, yc, x, w_a, w_b, w_c, g1, g2, gate, shift, scale, rw, rb, *, tiles_per_sample,
              n_lat_tiles, n_batch):
    t, d = x.shape
    nt = t // TM

    def seg(i):
        return jnp.where(i % tiles_per_sample >= n_lat_tiles, n_batch, i // tiles_per_sample)

    row = lambda n: pl.BlockSpec((TM, n), lambda i: (i, 0))
    full = lambda a: pl.BlockSpec(a.shape, lambda i: (0,) * a.ndim)
    modspec = pl.BlockSpec((1, 1, d), lambda i: (seg(i), 0, 0))
    scan_spec = lambda dd: pl.BlockSpec((1, 1, TM, LRU_W),
                                        lambda i: (dd, i // tiles_per_sample, i % tiles_per_sample, 0))
    return pl.pallas_call(
        _proj_out_body,
        grid=(nt,),
        in_specs=[row(NA_W), scan_spec(0), scan_spec(1), row(LRU_W), row(MLA_W), row(d),
                  full(w_a), full(w_b), full(w_c), full(g1), full(g2), modspec, modspec, modspec,
                  full(rw), full(rb)],
        out_specs=[row(d), row(d), row(LANES)],
        out_shape=[jax.ShapeDtypeStruct((t, d), F32),
                   jax.ShapeDtypeStruct((t, d), BF16),
                   jax.ShapeDtypeStruct((t, LANES), F32)],
        compiler_params=_params(("parallel",)),
        name="proj_out",
    )(ya, hscan, hscan, gg, yc, x, w_a, w_b, w_c, g1, g2, gate, shift, scale, rw, rb)


def _moe_dense_body(h_ref, comb_ref, wg_ref, wu_ref, wd_ref, o_ref, acc_ref):
    e = pl.program_id(1)

    @pl.when(e == 0)
    def _():
        acc_ref[...] = jnp.zeros_like(acc_ref)

    h = h_ref[...]
    a = _dot(h, wg_ref[0])
    he = (a * _sigmoid(a)) * _dot(h, wu_ref[0])
    y = _dot(he.astype(BF16), wd_ref[0])
    lane = lax.broadcasted_iota(jnp.int32, comb_ref.shape, 1)
    cw = jnp.sum(jnp.where(lane == e, comb_ref[...], 0.0), axis=-1, keepdims=True)
    acc_ref[...] += cw * y

    @pl.when(e == pl.num_programs(1) - 1)
    def _():
        o_ref[...] = acc_ref[...]


def _moe_dense(h2, comb, w_gate, w_up, w_down, *, tm):
    t, d = h2.shape
    n_e, _, de = w_gate.shape
    return pl.pallas_call(
        _moe_dense_body,
        grid=(t // tm, n_e),
        in_specs=[pl.BlockSpec((tm, d), lambda i, e: (i, 0)),
                  pl.BlockSpec((tm, LANES), lambda i, e: (i, 0)),
                  pl.BlockSpec((1, d, de), lambda i, e: (e, 0, 0)),
                  pl.BlockSpec((1, d, de), lambda i, e: (e, 0, 0)),
                  pl.BlockSpec((1, de, d), lambda i, e: (e, 0, 0))],
        out_specs=pl.BlockSpec((tm, d), lambda i, e: (i, 0)),
        out_shape=jax.ShapeDtypeStruct((t, d), F32),
        scratch_shapes=[pltpu.VMEM((tm, d), F32)],
        compiler_params=_params(("parallel", "arbitrary")),
        name="moe_dense",
    )(h2, comb, w_gate, w_up, w_down)


def _residual_body(x_ref, f_ref, gate_ref, g_ref, o_ref):
    o_ref[...] = x_ref[...] + gate_ref[0] * _rms(f_ref[...], g_ref[...])


def _residual(x, f, gate, g, *, tiles_per_sample, n_lat_tiles, n_batch):
    t, d = x.shape

    def seg(i):
        return jnp.where(i % tiles_per_sample >= n_lat_tiles, n_batch, i // tiles_per_sample)

    row = pl.BlockSpec((TM, d), lambda i: (i, 0))
    return pl.pallas_call(
        _residual_body,
        grid=(t // TM,),
        in_specs=[row, row, pl.BlockSpec((1, 1, d), lambda i: (seg(i), 0, 0)), pl.BlockSpec(g.shape, lambda i: (0, 0))],
        out_specs=row,
        out_shape=jax.ShapeDtypeStruct((t, d), F32),
        compiler_params=_params(("parallel",)),
        name="moe_residual",
    )(x, f, gate, g)


def _rope_tables(seq, ctx):
    t = np.arange(seq)
    axis_dim = QK_ROPE // 2
    inv = jnp.asarray(ROPE_BASE, F32) ** (-jnp.arange(0, axis_dim, 2, dtype=F32) / axis_dim)
    ang_r = jnp.asarray(t // GRID_W, F32)[:, None] * inv
    ang_c = jnp.asarray(t % GRID_W, F32)[:, None] * inv
    cos = jnp.concatenate([jnp.cos(ang_r), jnp.cos(ang_r), jnp.cos(ang_c), jnp.cos(ang_c)], axis=-1)
    sin = jnp.concatenate([-jnp.sin(ang_r), jnp.sin(ang_r), -jnp.sin(ang_c), jnp.sin(ang_c)], axis=-1)
    pad = HEAD_PAD - QK_NOPE - QK_ROPE
    cos = jnp.concatenate([jnp.ones((seq, QK_NOPE), F32), cos, jnp.ones((seq, pad), F32)], axis=-1)
    sin = jnp.concatenate([jnp.zeros((seq, QK_NOPE), F32), sin, jnp.zeros((seq, pad), F32)], axis=-1)
    cos = jnp.concatenate([cos, jnp.ones((ctx, HEAD_PAD), F32)], axis=0)
    sin = jnp.concatenate([sin, jnp.zeros((ctx, HEAD_PAD), F32)], axis=0)
    return cos, sin


_ROPE_SWAP = np.concatenate([np.arange(8, 16), np.arange(0, 8), np.arange(24, 32), np.arange(16, 24)])


def _layout_w_in(w_in):
    d = w_in.shape[0]
    z = lambda n: jnp.zeros((d, n), w_in.dtype)
    kr = w_in[:, _C_KR:_C_KR + QK_ROPE]
    pad = HEAD_PAD - QK_NOPE - QK_ROPE
    return jnp.concatenate([w_in[:, :_C_KR], z(QK_NOPE), kr, z(pad), z(QK_NOPE), kr[:, _ROPE_SWAP], z(pad)],
                           axis=1).astype(BF16)


def _layout_w_q(w_q):
    r = w_q.shape[0]
    wh = w_q.reshape(r, MLA_HEADS, QK_NOPE + QK_ROPE)
    pad = HEAD_PAD - QK_NOPE - QK_ROPE
    full = jnp.concatenate([wh, jnp.zeros((r, MLA_HEADS, pad), w_q.dtype)], axis=-1)
    swap = jnp.concatenate([jnp.zeros((r, MLA_HEADS, QK_NOPE), w_q.dtype), wh[:, :, QK_NOPE + _ROPE_SWAP],
                            jnp.zeros((r, MLA_HEADS, pad), w_q.dtype)], axis=-1)
    return jnp.concatenate([full.reshape(r, -1), swap.reshape(r, -1)], axis=1).astype(BF16)


def _layout_w_kv(w_kv):
    r = w_kv.shape[0]
    wh = w_kv.reshape(r, MLA_HEADS, QK_NOPE + V_HEAD)
    zk = jnp.zeros((r, MLA_HEADS, HEAD_PAD - QK_NOPE), w_kv.dtype)
    zv = jnp.zeros((r, MLA_HEADS, HEAD_PAD - V_HEAD), w_kv.dtype)
    k = jnp.concatenate([wh[:, :, :QK_NOPE], zk], axis=-1).reshape(r, -1)
    v = jnp.concatenate([wh[:, :, QK_NOPE:], zv], axis=-1).reshape(r, -1)
    return jnp.concatenate([k, v], axis=1).astype(BF16)


def _block_diag(w):
    nd, nb, k, _ = w.shape
    eye = jnp.eye(nb, dtype=w.dtype)
    return (w[:, :, :, None, :] * eye[None, :, None, :, None]).reshape(nd, nb * k, nb * k).astype(BF16)


def _pick_chunk(n, candidates):
    for c in candidates:
        if n % c == 0:
            return c
    raise ValueError(f"no chunk size for {n}")


def kernel(x, c, ctx, c_ctx, w_ada, b_ada, g_norm, w_in, w_out, na_rpb, conv_w, conv_b, lru_w_a, lru_b_a, lru_w_i,
           lru_b_i, lru_lam, mla_g_q, mla_w_q, mla_g_kv, mla_w_kv, router_w, router_b, exp_w_gate, exp_w_up,
           exp_w_down):
    b, seq, d = x.shape
    n_ctx = ctx.shape[1]
    depth = w_ada.shape[0]
    lt = seq + n_ctx
    assert d == D_MODEL and n_ctx == TM and seq % TM == 0 and seq // GRID_W >= WIN_ROWS and b + 1 <= 8
    tiles_per_sample = lt // TM
    n_lat_tiles = seq // TM
    seg_kw = dict(tiles_per_sample=tiles_per_sample, n_lat_tiles=n_lat_tiles, n_batch=b)

    xa = jnp.concatenate([x, ctx], axis=1).reshape(b * lt, d)

    c_rows = jnp.concatenate([c, c_ctx[None, :], jnp.zeros((8 - b - 1, d), F32)], axis=0)
    mod = _ada_tables(c_rows, w_ada, b_ada).reshape(depth, 8, N_MOD, 1, d)

    cos_t, sin_t = _rope_tables(seq, n_ctx)
    ones_row = jnp.zeros((MLA_HEADS, HEAD_PAD), F32).at[:, V_HEAD].set(1.0).reshape(1, -1)
    rw = jnp.concatenate([router_w, jnp.zeros((d, LANES - N_EXPERTS), F32)], axis=1).astype(BF16)
    rb = jnp.concatenate([router_b, jnp.zeros((LANES - N_EXPERTS,), F32)])[None, :]
    tk = _pick_chunk(lt, (768, 512, 256))
    tq = _pick_chunk(seq, (512, 256))
    moe_tm = _pick_chunk(b * lt, (768, 512, 256))

    for l in range(depth):
        last = l == depth - 1
        m = lambda k: mod[l, :b + 1, k]
        g = g_norm[l]
        qkv, u, gg, mq, mk, mv = _proj_in(
            xa, m(0), m(1), g[0:1], _layout_w_in(w_in[l]), mla_g_q[l][None, :], _layout_w_q(mla_w_q[l]),
            mla_g_kv[l][None, :], _layout_w_kv(mla_w_kv[l]), cos_t, sin_t, ones_row, **seg_kw)

        ya = _na_attention(qkv.reshape(b, lt, -1), _na_bias_tables(na_rpb[l]), seq=seq, ctx=n_ctx)
        hscan = _lru_scan(u.reshape(b, lt, -1), conv_w[l], conv_b[l][None, :], _block_diag(lru_w_a[l]),
                          lru_b_a[l][:, None, :], _block_diag(lru_w_i[l]), lru_b_i[l][:, None, :],
                          lru_lam[l][:, None, :], seq=seq, ctx=n_ctx)
        mq3, mk3, mv3 = (a.reshape(b, lt, -1) for a in (mq, mk, mv))
        yc = _mla_attention(mq3, mk3, mv3, tq=tq, q_block0=0, n_q=seq // tq, k_block=lt, k_block0=0, tk=tk)
        if last:
            yc = jnp.concatenate([yc, jnp.zeros((b, n_ctx, MLA_W), BF16)], axis=1)
        else:
            yc_ctx = _mla_attention(mq3, mk3, mv3, tq=n_ctx, q_block0=seq // n_ctx, n_q=1, k_block=n_ctx,
                                    k_block0=seq // n_ctx, tk=n_ctx)
            yc = jnp.concatenate([yc, yc_ctx], axis=1)

        wo = w_out[l].astype(BF16)
        xa, h2, comb = _proj_out(
            ya.reshape(b * lt, -1), hscan, gg, yc.reshape(b * lt, -1), xa,
            wo[:NA_W], wo[NA_W:NA_W + LRU_W], wo[NA_W + LRU_W:], g[1:2], g[2:3], m(2), m(3), m(4), rw, rb, **seg_kw)

        f = _moe_dense(h2, comb, exp_w_gate[l].astype(BF16), exp_w_up[l].astype(BF16), exp_w_down[l].astype(BF16),
                       tm=moe_tm)
        xa = _residual(xa, f, m(5), g[3:4], **seg_kw)

    return xa.reshape(b, lt, d)[:, :seq]
```

```python
import functools

import numpy as np
import jax
import jax.numpy as jnp
from jax import lax
from jax.experimental import pallas as pl
from jax.experimental.pallas import tpu as pltpu

F32 = jnp.float32
BF16 = jnp.bfloat16

D_MODEL = 1024
GRID_W = 64
NA_HEADS = 4
NA_HEAD_DIM = 64
NA_W = NA_HEADS * NA_HEAD_DIM
WIN_ROWS = 8
WIN_COLS = 16
LRU_W = 512
LRU_BLOCKS = 8
CONV_W = 4
LRU_C = 8.0
MLA_HEADS = 4
Q_LORA = 256
KV_LORA = 128
QK_NOPE = 64
QK_ROPE = 32
V_HEAD = 64
MLA_W = MLA_HEADS * V_HEAD
ROPE_BASE = 10000.0
N_EXPERTS = 16
N_GROUPS = 4
EXPERTS_PER_GROUP = 4
D_EXPERT = 512
RMS_EPS = 1e-6
N_MOD = 6

_PAIRS = ((0, 1), (0, 2), (0, 3), (1, 2), (1, 3), (2, 3))
N_PAIRS = len(_PAIRS)
N_CLASSES = N_GROUPS * N_PAIRS

LANES = 128
SUBLANES = 8
TM = 256
HEAD_PAD = 128
NEG_BIAS = -1e30
VMEM_LIMIT = 56 * 1024 * 1024

_C_QA, _C_KA, _C_VA = 0, 256, 512
_C_U = 768
_C_GATE = 1280
_C_CQ = 1792
_C_CKV = 2048
_C_KR = 2176
_C_KRP = 2304
_IN_COLS = 2432


def _params(sem, vmem=VMEM_LIMIT):
    return pltpu.CompilerParams(dimension_semantics=sem, vmem_limit_bytes=vmem)


def _sigmoid(v):
    return 1.0 / (1.0 + jnp.exp(-v))


def _rms(v, g):
    return v * lax.rsqrt(jnp.mean(v * v, axis=-1, keepdims=True) + RMS_EPS) * g


def _dot(a, b):
    return jnp.dot(a, b, preferred_element_type=F32)


def _dot_nt(a, b):
    return lax.dot_general(a, b, (((1,), (1,)), ((), ())), preferred_element_type=F32)


def _ada_body(c_ref, w_ref, b_ref, o_ref):
    c = c_ref[...]
    sc = c * _sigmoid(c)
    o_ref[0] = _dot(sc.astype(BF16), w_ref[0].astype(BF16)) + b_ref[0]


def _ada_tables(c_rows, w_ada, b_ada):
    depth, d, n = w_ada.shape
    tn = 1536
    return pl.pallas_call(
        _ada_body,
        grid=(depth, n // tn),
        in_specs=[pl.BlockSpec((8, d), lambda l, j: (0, 0)),
                  pl.BlockSpec((1, d, tn), lambda l, j: (l, 0, j)),
                  pl.BlockSpec((1, 1, tn), lambda l, j: (l, 0, j))],
        out_specs=pl.BlockSpec((1, 8, tn), lambda l, j: (l, 0, j)),
        out_shape=jax.ShapeDtypeStruct((depth, 8, n), F32),
        compiler_params=_params(("parallel", "parallel")),
        name="ada_tables",
    )(c_rows, w_ada, b_ada.reshape(depth, 1, n))


def _proj_in_body(x_ref, shift_ref, scale_ref, g_ref, w_ref, gq_ref, wq_ref, gkv_ref, wkv_ref,
                  cos_ref, sin_ref, ones_ref,
                  qkv_ref, u_ref, gg_ref, mq_ref, mk_ref, mv_ref, *, mla_scale):
    x = x_ref[...]
    h = _rms(x, g_ref[...]) * (1.0 + scale_ref[0]) + shift_ref[0]
    z = _dot(h.astype(BF16), w_ref[...])
    qkv_ref[:, 0:NA_W] = (z[:, _C_QA:_C_KA] * (NA_HEAD_DIM ** -0.5)).astype(BF16)
    qkv_ref[:, NA_W:3 * NA_W] = z[:, _C_KA:_C_U].astype(BF16)
    u_ref[...] = z[:, _C_U:_C_GATE]
    gg_ref[...] = jax.nn.gelu(z[:, _C_GATE:_C_CQ]).astype(BF16)

    cos = cos_ref[...]
    sin = sin_ref[...]
    cos4 = jnp.concatenate([cos] * MLA_HEADS, axis=-1)
    sin4 = jnp.concatenate([sin] * MLA_HEADS, axis=-1)
    nw = MLA_HEADS * HEAD_PAD

    nq = _rms(z[:, _C_CQ:_C_CKV], gq_ref[...])
    q2 = _dot(nq.astype(BF16), wq_ref[...])
    mq_ref[...] = ((q2[:, :nw] * cos4 + q2[:, nw:] * sin4) * mla_scale).astype(BF16)

    nkv = _rms(z[:, _C_CKV:_C_KR], gkv_ref[...])
    kv2 = _dot(nkv.astype(BF16), wkv_ref[...])
    k_rope = z[:, _C_KR:_C_KRP] * cos + z[:, _C_KRP:_IN_COLS] * sin
    mk_ref[...] = (kv2[:, :nw] + jnp.concatenate([k_rope] * MLA_HEADS, axis=-1)).astype(BF16)
    mv_ref[...] = (kv2[:, nw:] + ones_ref[...]).astype(BF16)


def _proj_in(x, shift, scale, g, w_big, gq, wq2, gkv, wkv2, cos_t, sin_t, ones_row, *, tiles_per_sample, n_lat_tiles,
             n_batch):
    t, d = x.shape
    nt = t // TM
    nw = MLA_HEADS * HEAD_PAD

    def seg(i):
        return jnp.where(i % tiles_per_sample >= n_lat_tiles, n_batch, i // tiles_per_sample)

    row = lambda n: pl.BlockSpec((TM, n), lambda i: (i, 0))
    full = lambda a: pl.BlockSpec(a.shape, lambda i: (0,) * a.ndim)
    modspec = pl.BlockSpec((1, 1, d), lambda i: (seg(i), 0, 0))
    tabspec = pl.BlockSpec((TM, HEAD_PAD), lambda i: (i % tiles_per_sample, 0))
    mla_scale = (QK_NOPE + QK_ROPE) ** -0.5
    return pl.pallas_call(
        functools.partial(_proj_in_body, mla_scale=mla_scale),
        grid=(nt,),
        in_specs=[row(d), modspec, modspec, full(g), full(w_big), full(gq), full(wq2), full(gkv), full(wkv2),
                  tabspec, tabspec, full(ones_row)],
        out_specs=[row(3 * NA_W), row(LRU_W), row(LRU_W), row(nw), row(nw), row(nw)],
        out_shape=[jax.ShapeDtypeStruct((t, 3 * NA_W), BF16),
                   jax.ShapeDtypeStruct((t, LRU_W), F32),
                   jax.ShapeDtypeStruct((t, LRU_W), BF16),
                   jax.ShapeDtypeStruct((t, nw), BF16),
                   jax.ShapeDtypeStruct((t, nw), BF16),
                   jax.ShapeDtypeStruct((t, nw), BF16)],
        compiler_params=_params(("parallel",)),
        name="proj_in",
    )(x, shift, scale, g, w_big, gq, wq2, gkv, wkv2, cos_t, sin_t, ones_row)


def _head_block_mask():
    r = lax.broadcasted_iota(jnp.int32, (NA_W, NA_W), 0) // NA_HEAD_DIM
    c = lax.broadcasted_iota(jnp.int32, (NA_W, NA_W), 1) // NA_HEAD_DIM
    return r == c


def _na_attend(q, parts, mask):
    qbig = jnp.where(mask, jnp.concatenate([q] * NA_HEADS, axis=0), jnp.zeros((), q.dtype))
    scores = []
    for k, _, bias in parts:
        s = _dot_nt(qbig, k)
        if bias is not None:
            s = s + bias
        scores.append(s)
    m = functools.reduce(jnp.maximum, [jnp.max(s, axis=-1, keepdims=True) for s in scores])
    ps = [jnp.exp(s - m) for s in scores]
    l = functools.reduce(jnp.add, [jnp.sum(p, axis=-1, keepdims=True) for p in ps])
    o = functools.reduce(jnp.add, [_dot(p.astype(BF16), v) for p, (_, v, _) in zip(ps, parts)])
    o = jnp.where(mask, o / l, 0.0)
    out = o[0:NA_HEAD_DIM]
    for h in range(1, NA_HEADS):
        out = out + o[h * NA_HEAD_DIM:(h + 1) * NA_HEAD_DIM]
    return out


def _na_body(q_ref, k_ref, v_ref, bias_ref, o_ref, *, rows, rows_per_step, n_lat_steps, seq, ctx):
    i = pl.program_id(1)
    mask = _head_block_mask()
    kc = k_ref[0, seq:seq + ctx, :]
    vc = v_ref[0, seq:seq + ctx, :]

    @pl.when(i < n_lat_steps)
    def _():
        for j in range(rows_per_step):
            r = i * rows_per_step + j
            rs = jnp.clip(r - WIN_ROWS // 2, 0, rows - WIN_ROWS)
            off = rs - r + (WIN_ROWS - 1)
            start = pl.multiple_of(rs * GRID_W, GRID_W)
            kw = k_ref[0, pl.ds(start, WIN_ROWS * GRID_W), :]
            vw = v_ref[0, pl.ds(start, WIN_ROWS * GRID_W), :]
            q = q_ref[0, j * GRID_W:(j + 1) * GRID_W, :]
            out = _na_attend(q, [(kw, vw, bias_ref[off]), (kc, vc, None)], mask)
            o_ref[0, j * GRID_W:(j + 1) * GRID_W, :] = out.astype(o_ref.dtype)

    @pl.when(i >= n_lat_steps)
    def _():
        for j in range(rows_per_step):
            q = q_ref[0, j * GRID_W:(j + 1) * GRID_W, :]
            out = _na_attend(q, [(kc, vc, None)], mask)
            o_ref[0, j * GRID_W:(j + 1) * GRID_W, :] = out.astype(o_ref.dtype)


def _na_attention(qkv, bias, *, seq, ctx):
    b, lt, _ = qkv.shape
    rows = seq // GRID_W
    rows_per_step = TM // GRID_W
    n_steps = lt // TM
    n_lat_steps = seq // TM
    body = functools.partial(_na_body, rows=rows, rows_per_step=rows_per_step, n_lat_steps=n_lat_steps,
                             seq=seq, ctx=ctx)
    return pl.pallas_call(
        body,
        grid=(b, n_steps),
        in_specs=[pl.BlockSpec((1, TM, NA_W), lambda bi, i: (bi, i, 0)),
                  pl.BlockSpec((1, lt, NA_W), lambda bi, i: (bi, 0, 1)),
                  pl.BlockSpec((1, lt, NA_W), lambda bi, i: (bi, 0, 2)),
                  pl.BlockSpec(bias.shape, lambda bi, i: (0, 0, 0))],
        out_specs=pl.BlockSpec((1, TM, NA_W), lambda bi, i: (bi, i, 0)),
        out_shape=jax.ShapeDtypeStruct((b, lt, NA_W), BF16),
        compiler_params=_params(("parallel", "arbitrary")),
        name="na_attention",
    )(qkv, qkv, qkv, bias)


def _na_bias_tables(rpb):
    q = np.arange(GRID_W)
    cs = np.clip(q - WIN_COLS // 2, 0, GRID_W - WIN_COLS)
    c = np.arange(GRID_W)
    inside = (c[None, :] >= cs[:, None]) & (c[None, :] < cs[:, None] + WIN_COLS)
    dc = c[None, :] - q[:, None] + (WIN_COLS - 1)
    place = ((np.arange(2 * WIN_COLS - 1)[:, None, None] == dc[None]) & inside[None]).astype(np.float32)
    win = jnp.stack([rpb[:, o:o + WIN_ROWS, :] for o in range(WIN_ROWS)]).astype(F32)
    vals = jnp.einsum('ohik,kqc->ohqic', win, jnp.asarray(place), precision=lax.Precision.HIGHEST)
    vals = jnp.where(inside[None, None, :, None, :], vals, NEG_BIAS)
    return vals.reshape(WIN_ROWS, NA_HEADS * GRID_W, WIN_ROWS * GRID_W)


def _softplus(v):
    return jnp.maximum(v, 0.0) + jnp.log1p(jnp.exp(-jnp.abs(v)))


def _lru_body(up_ref, uc_ref, un_ref, cw_ref, cb_ref, wa_ref, ba_ref, wi_ref, bi_ref, lam_ref,
              o_ref, a_buf, x_buf, st_ref, *, n_lat, chunk, n_batch):
    d = pl.program_id(0)
    i = pl.program_id(1)
    c = _lru_chunk(d, i, n_lat)
    prev_ok = jnp.logical_and(c >= 1, c < n_lat)
    next_ok = c <= n_lat - 2

    row = lax.broadcasted_iota(jnp.int32, (chunk, 1), 0)
    cw = cw_ref[...]
    log_decay = _softplus(-lam_ref[0])
    for b in range(n_batch):
        ub = uc_ref[b]
        p_row = jnp.where(prev_ok, up_ref[b, 7:8, :], 0.0)
        n0 = jnp.where(next_ok, un_ref[b, 0:1, :], 0.0)
        n1 = jnp.where(next_ok, un_ref[b, 1:2, :], 0.0)
        um1 = jnp.where(row == 0, p_row, pltpu.roll(ub, 1, 0))
        up1 = jnp.where(row == chunk - 1, n0, pltpu.roll(ub, chunk - 1, 0))
        up2 = jnp.where(row == chunk - 1, n1, jnp.where(row == chunk - 2, n0, pltpu.roll(ub, chunk - 2, 0)))
        cv = um1 * cw[0:1] + cb_ref[...]
        cv = cv + ub * cw[1:2]
        cv = cv + up1 * cw[2:3]
        cv = cv + up2 * cw[3:4]
        cv16 = cv.astype(BF16)
        r = _sigmoid(_dot(cv16, wa_ref[0]) + ba_ref[0])
        gi = _sigmoid(_dot(cv16, wi_ref[0]) + bi_ref[0])
        log_a = (-LRU_C * r) * log_decay
        a = jnp.exp(log_a)
        a_buf[b] = a
        x_buf[b] = jnp.sqrt(1.0 - a * a) * (gi * cv)

    @pl.when(i == 0)
    def _():
        st_ref[...] = jnp.zeros_like(st_ref)

    def step(t, hs):
        tt = jnp.where(d == 0, t, chunk - 1 - t)
        new = []
        for b in range(n_batch):
            h = a_buf[b, pl.ds(tt, 1), :] * hs[b] + x_buf[b, pl.ds(tt, 1), :]
            o_ref[0, b, pl.ds(tt, 1), :] = h
            new.append(h)
        return tuple(new)

    hs = lax.fori_loop(0, chunk, step, tuple(st_ref[b, 0:1, :] for b in range(n_batch)), unroll=8)
    for b in range(n_batch):
        st_ref[b, 0:1, :] = hs[b]


def _lru_chunk(d, i, n_lat):
    return jnp.where(i == 0, n_lat, jnp.where(d == 0, i - 1, n_lat - i))


def _lru_scan(u, conv_w, conv_b, w_a, b_a, w_i, b_i, lam, *, seq, ctx):
    b, lt, w = u.shape
    chunk = ctx
    n_lat = seq // chunk
    n_chunks = lt // chunk
    sub = chunk // 8
    n_sub = lt // 8

    cur = lambda d, i: (0, _lru_chunk(d, i, n_lat), 0)
    prev = lambda d, i: (0, jnp.maximum(_lru_chunk(d, i, n_lat) * sub - 1, 0), 0)
    nxt = lambda d, i: (0, jnp.minimum((_lru_chunk(d, i, n_lat) + 1) * sub, n_sub - 1), 0)
    per_dir = lambda a: pl.BlockSpec((1,) + a.shape[1:], lambda d, i: (d,) + (0,) * (a.ndim - 1))
    full = lambda a: pl.BlockSpec(a.shape, lambda d, i: (0,) * a.ndim)
    body = functools.partial(_lru_body, n_lat=n_lat, chunk=chunk, n_batch=b)
    return pl.pallas_call(
        body,
        grid=(2, n_chunks),
        in_specs=[pl.BlockSpec((b, 8, w), prev), pl.BlockSpec((b, chunk, w), cur), pl.BlockSpec((b, 8, w), nxt),
                  full(conv_w), full(conv_b), per_dir(w_a), per_dir(b_a), per_dir(w_i), per_dir(b_i), per_dir(lam)],
        out_specs=pl.BlockSpec((1, b, chunk, w), lambda d, i: (d, 0, _lru_chunk(d, i, n_lat), 0)),
        out_shape=jax.ShapeDtypeStruct((2, b, lt, w), F32),
        scratch_shapes=[pltpu.VMEM((b, chunk, w), F32), pltpu.VMEM((b, chunk, w), F32), pltpu.VMEM((b, 8, w), F32)],
        compiler_params=_params(("arbitrary", "arbitrary")),
        name="lru_scan",
    )(u, u, u, conv_w, conv_b, w_a, b_a, w_i, b_i, lam)


def _mla_body(q_ref, k_ref, v_ref, o_ref, m_ref, acc_ref, *, tk, nk):
    m_ref[...] = jnp.full(m_ref.shape, NEG_BIAS, F32)
    acc_ref[...] = jnp.zeros(acc_ref.shape, F32)

    def chunk(ci, carry):
        start = pl.multiple_of(ci * tk, tk)
        for h in range(MLA_HEADS):
            cols = slice(h * HEAD_PAD, (h + 1) * HEAD_PAD)
            s = _dot_nt(q_ref[0, :, cols], k_ref[0, pl.ds(start, tk), cols])
            m_old = m_ref[h]
            m_new = jnp.maximum(m_old, jnp.max(s, axis=-1, keepdims=True))
            p = jnp.exp(s - m_new)
            acc_ref[h] = jnp.exp(m_old - m_new) * acc_ref[h] + _dot(p.astype(BF16), v_ref[0, pl.ds(start, tk), cols])
            m_ref[h] = m_new
        return carry

    lax.fori_loop(0, nk, chunk, 0)
    outs = [acc_ref[h][:, :V_HEAD] / acc_ref[h][:, V_HEAD:V_HEAD + 1] for h in range(MLA_HEADS)]
    o_ref[0] = jnp.concatenate(outs, axis=-1).astype(o_ref.dtype)


def _mla_attention(mq, mk, mv, *, tq, q_block0, n_q, k_block, k_block0, tk):
    b, lt, nw = mq.shape
    body = functools.partial(_mla_body, tk=tk, nk=k_block // tk)
    return pl.pallas_call(
        body,
        grid=(b, n_q),
        in_specs=[pl.BlockSpec((1, tq, nw), lambda bi, i: (bi, q_block0 + i, 0)),
                  pl.BlockSpec((1, k_block, nw), lambda bi, i: (bi, k_block0, 0), pipeline_mode=pl.Buffered(1)),
                  pl.BlockSpec((1, k_block, nw), lambda bi, i: (bi, k_block0, 0), pipeline_mode=pl.Buffered(1))],
        out_specs=pl.BlockSpec((1, tq, MLA_W), lambda bi, i: (bi, i, 0)),
        out_shape=jax.ShapeDtypeStruct((b, n_q * tq, MLA_W), BF16),
        scratch_shapes=[pltpu.VMEM((MLA_HEADS, tq, 1), F32), pltpu.VMEM((MLA_HEADS, tq, HEAD_PAD), F32)],
        compiler_params=_params(("parallel", "arbitrary")),
        name="mla_attention",
    )(mq, mk, mv)


def _route(logits, rb):
    sel = _sigmoid(logits) + rb
    sc = [sel[:, j:j + 1] for j in range(N_EXPERTS)]
    gscore = []
    for g in range(N_GROUPS):
        a, b, c, d = sc[4 * g:4 * g + 4]
        hi1, lo1 = jnp.maximum(a, b), jnp.minimum(a, b)
        hi2, lo2 = jnp.maximum(c, d), jnp.minimum(c, d)
        gscore.append(jnp.maximum(hi1, hi2) + jnp.maximum(jnp.minimum(hi1, hi2), jnp.maximum(lo1, lo2)))
    best = jnp.zeros_like(gscore[0], dtype=jnp.int32)
    bval = gscore[0]
    for g in range(1, N_GROUPS):
        upd = gscore[g] > bval
        best = jnp.where(upd, g, best)
        bval = jnp.where(upd, gscore[g], bval)

    def pick(cols, j):
        out = cols[j]
        for g in range(1, N_GROUPS):
            out = jnp.where(best == g, cols[4 * g + j], out)
        return out

    v = [pick(sc, j) for j in range(EXPERTS_PER_GROUP)]
    code = jnp.zeros_like(best)
    for i in range(EXPERTS_PER_GROUP):
        rank = jnp.zeros_like(best)
        for j in range(EXPERTS_PER_GROUP):
            if j == i:
                continue
            beats = (v[j] > v[i]) if j > i else (v[j] >= v[i])
            rank = rank + beats.astype(jnp.int32)
        code = code + jnp.where(rank < 2, 1 << i, 0)
    pair = jnp.zeros_like(best)
    for p, (lo, hi) in enumerate(_PAIRS):
        pair = jnp.where(code == (1 << lo) + (1 << hi), p, pair)
    return best * N_PAIRS + pair


def _proj_out_body(ya_ref, hf_ref, hr_ref, gg_ref, yc_ref, x_ref, wa_ref, wb_ref, wc_ref, g1_ref, g2_ref,
                   gate_ref, shift_ref, scale_ref, rw_ref, rb_ref, xo_ref, h2_ref, cls_ref):
    yb = (hf_ref[0, 0] + hr_ref[0, 0]) * gg_ref[...].astype(F32)
    y = _dot(ya_ref[...], wa_ref[...]) + _dot(yb.astype(BF16), wb_ref[...]) + _dot(yc_ref[...], wc_ref[...])
    x = x_ref[...] + gate_ref[0] * _rms(y, g1_ref[...])
    xo_ref[...] = x
    h2 = (_rms(x, g2_ref[...]) * (1.0 + scale_ref[0]) + shift_ref[0]).astype(BF16)
    _store_token_tiles(h2_ref, h2.astype(F32))
    cls_ref[...] = jnp.broadcast_to(_route(_dot(h2, rw_ref[...]), rb_ref[...]), cls_ref.shape)


def _proj_out(ya, hscan, gg, yc, x, w_a, w_b, w_c, g1, g2, gate, shift, scale, rw, rb, *, tiles_per_sample,
              n_lat_tiles, n_batch):
    t, d = x.shape
    nt = t // TM

    def seg(i):
        return jnp.where(i % tiles_per_sample >= n_lat_tiles, n_batch, i // tiles_per_sample)

    row = lambda n: pl.BlockSpec((TM, n), lambda i: (i, 0))
    full = lambda a: pl.BlockSpec(a.shape, lambda i: (0,) * a.ndim)
    modspec = pl.BlockSpec((1, 1, d), lambda i: (seg(i), 0, 0))
    scan_spec = lambda dd: pl.BlockSpec((1, 1, TM, LRU_W),
                                        lambda i: (dd, i // tiles_per_sample, i % tiles_per_sample, 0))
    return pl.pallas_call(
        _proj_out_body,
        grid=(nt,),
        in_specs=[row(NA_W), scan_spec(0), scan_spec(1), row(LRU_W), row(MLA_W), row(d),
                  full(w_a), full(w_b), full(w_c), full(g1), full(g2), modspec, modspec, modspec,
                  full(rw), full(rb)],
        out_specs=[row(d), pl.BlockSpec((TM * SUBLANES, LANES), lambda i: (i, 0)), row(LANES)],
        out_shape=[jax.ShapeDtypeStruct((t, d), F32),
                   jax.ShapeDtypeStruct((t * SUBLANES, LANES), F32),
                   jax.ShapeDtypeStruct((t, LANES), jnp.int32)],
        compiler_params=_params(("parallel",)),
        name="proj_out",
    )(ya, hscan, hscan, gg, yc, x, w_a, w_b, w_c, g1, g2, gate, shift, scale, rw, rb)


def _store_token_tiles(ref, v):
    n = v.shape[0]
    for s in range(SUBLANES):
        ref[pl.ds(s, n, stride=SUBLANES), :] = v[:, s * LANES:(s + 1) * LANES]


def _load_token_tiles(ref, n):
    return jnp.concatenate([ref[pl.ds(s, n, stride=SUBLANES), :] for s in range(SUBLANES)], axis=1)


def _gather_token_tiles(idx_ref, base, src_ref, buf_ref, n):
    def copy(r, carry):
        j = idx_ref[base + r]
        buf_ref[pl.ds(pl.multiple_of(r * SUBLANES, SUBLANES), SUBLANES), :] = (
            src_ref[0, pl.ds(pl.multiple_of(j * SUBLANES, SUBLANES), SUBLANES), :])
        return carry

    lax.fori_loop(0, n, copy, 0, unroll=8)


def _dispatch_body(src_ref, h_ref, o_ref, buf_ref, *, tokens_per_sample):
    base = pl.program_id(0) * tokens_per_sample + pl.program_id(1) * TM
    _gather_token_tiles(src_ref, base, h_ref, buf_ref, TM)
    o_ref[...] = _load_token_tiles(buf_ref, TM).astype(o_ref.dtype)


def _dispatch(src, h2t, *, n_batch, tokens_per_sample):
    n_tiles = tokens_per_sample // TM
    return pl.pallas_call(
        functools.partial(_dispatch_body, tokens_per_sample=tokens_per_sample),
        grid_spec=pltpu.PrefetchScalarGridSpec(
            num_scalar_prefetch=1,
            grid=(n_batch, n_tiles),
            in_specs=[pl.BlockSpec((1, tokens_per_sample * SUBLANES, LANES), lambda b, i, src: (b, 0, 0),
                                   pipeline_mode=pl.Buffered(1))],
            out_specs=pl.BlockSpec((TM, D_MODEL), lambda b, i, src: (b * n_tiles + i, 0)),
            scratch_shapes=[pltpu.VMEM((TM * SUBLANES, LANES), F32)]),
        out_shape=jax.ShapeDtypeStruct((n_batch * tokens_per_sample, D_MODEL), BF16),
        compiler_params=_params(("arbitrary", "arbitrary")),
        name="moe_dispatch",
    )(src, h2t.reshape(n_batch, tokens_per_sample * SUBLANES, LANES))


def _moe_body(tile_ref, elo_ref, ehi_ref, lo_ref, hi_ref, valid_ref, first_ref,
              x_ref, rw_ref, wg0_ref, wu0_ref, wd0_ref, wg1_ref, wu1_ref, wd1_ref, o_ref):
    k = pl.program_id(0)

    @pl.when(valid_ref[k] == 1)
    def _():
        x = x_ref[...]
        aff = _sigmoid(_dot(x, rw_ref[...]))
        lane = lax.broadcasted_iota(jnp.int32, aff.shape, 1)
        a_lo = jnp.sum(jnp.where(lane == elo_ref[k], aff, 0.0), axis=-1, keepdims=True)
        a_hi = jnp.sum(jnp.where(lane == ehi_ref[k], aff, 0.0), axis=-1, keepdims=True)
        den = a_lo + a_hi

        def expert(wg, wu, wd):
            a = _dot(x, wg[0])
            he = (a * _sigmoid(a)) * _dot(x, wu[0])
            return _dot(he.astype(BF16), wd[0])

        y = (a_lo / den) * expert(wg0_ref, wu0_ref, wd0_ref) + (a_hi / den) * expert(wg1_ref, wu1_ref, wd1_ref)
        row = lax.broadcasted_iota(jnp.int32, (TM, 1), 0)
        y = jnp.where(jnp.logical_and(row >= lo_ref[k], row < hi_ref[k]), y, 0.0)

        @pl.when(first_ref[k] == 1)
        def _():
            _store_token_tiles(o_ref, y)

        @pl.when(first_ref[k] == 0)
        def _():
            _store_token_tiles(o_ref, _load_token_tiles(o_ref, TM) + y)


def _moe_routed(items, xs, rw, w_gate, w_up, w_down):
    t, d = xs.shape
    _, _, de = w_gate.shape
    n_items = items[0].shape[0]
    xspec = pl.BlockSpec((TM, d), lambda k, tile, *_: (tile[k], 0))
    w_in_spec = lambda which: pl.BlockSpec((1, d, de), lambda k, tile, elo, ehi, *_: ((elo, ehi)[which][k], 0, 0))
    w_out_spec = lambda which: pl.BlockSpec((1, de, d), lambda k, tile, elo, ehi, *_: ((elo, ehi)[which][k], 0, 0))
    return pl.pallas_call(
        _moe_body,
        grid_spec=pltpu.PrefetchScalarGridSpec(
            num_scalar_prefetch=len(items),
            grid=(n_items,),
            in_specs=[xspec, pl.BlockSpec(rw.shape, lambda k, *_: (0, 0)),
                      w_in_spec(0), w_in_spec(0), w_out_spec(0), w_in_spec(1), w_in_spec(1), w_out_spec(1)],
            out_specs=pl.BlockSpec((TM * SUBLANES, LANES), lambda k, tile, *_: (tile[k], 0))),
        out_shape=jax.ShapeDtypeStruct((t * SUBLANES, LANES), F32),
        compiler_params=_params(("arbitrary",)),
        name="moe_routed",
    )(*items, xs, rw, w_gate, w_up, w_down, w_gate, w_up, w_down)


def _combine_body(pos_ref, y_ref, x_ref, gate_ref, g_ref, o_ref, buf_ref, *, tokens_per_sample):
    base = pl.program_id(0) * tokens_per_sample + pl.program_id(1) * TM
    _gather_token_tiles(pos_ref, base, y_ref, buf_ref, TM)
    o_ref[...] = x_ref[...] + gate_ref[0] * _rms(_load_token_tiles(buf_ref, TM), g_ref[...])


def _combine(pos, ys, x, gate, g, *, n_batch, tokens_per_sample, n_lat_tiles):
    n_tiles = tokens_per_sample // TM
    d = x.shape[1]
    row = pl.BlockSpec((TM, d), lambda b, i, pos: (b * n_tiles + i, 0))
    return pl.pallas_call(
        functools.partial(_combine_body, tokens_per_sample=tokens_per_sample),
        grid_spec=pltpu.PrefetchScalarGridSpec(
            num_scalar_prefetch=1,
            grid=(n_batch, n_tiles),
            in_specs=[pl.BlockSpec((1, tokens_per_sample * SUBLANES, LANES), lambda b, i, pos: (b, 0, 0),
                                   pipeline_mode=pl.Buffered(1)),
                      row,
                      pl.BlockSpec((1, 1, d), lambda b, i, pos: (jnp.where(i >= n_lat_tiles, n_batch, b), 0, 0)),
                      pl.BlockSpec(g.shape, lambda b, i, pos: (0, 0))],
            out_specs=row,
            scratch_shapes=[pltpu.VMEM((TM * SUBLANES, LANES), F32)]),
        out_shape=jax.ShapeDtypeStruct(x.shape, F32),
        compiler_params=_params(("arbitrary", "arbitrary")),
        name="moe_combine",
    )(pos, ys.reshape(n_batch, tokens_per_sample * SUBLANES, LANES), x, gate, g)


def _routing_plan(cls, *, n_tiles):
    b, l = cls.shape
    n_items = n_tiles + N_CLASSES - 1
    onehot = (cls[..., None] == jnp.arange(N_CLASSES, dtype=jnp.int32)).astype(jnp.int32)
    csum = jnp.cumsum(onehot, axis=1)
    counts = csum[:, -1]
    starts = jnp.cumsum(counts, axis=1) - counts
    pos = jnp.sum(onehot * (starts[:, None, :] + csum - 1), axis=-1)
    bi = jnp.arange(b, dtype=jnp.int32)[:, None]
    src = jnp.zeros((b, l), jnp.int32).at[bi, pos].set(jnp.broadcast_to(jnp.arange(l, dtype=jnp.int32), (b, l)))

    first_tile = starts // TM
    last_tile = (starts + counts - 1) // TM
    m = jnp.where(counts > 0, last_tile - first_tile + 1, 0)
    o_end = jnp.cumsum(m, axis=1)
    o_start = o_end - m
    total = o_end[:, -1:]
    k = jnp.arange(n_items, dtype=jnp.int32)[None, :]
    kk = jnp.minimum(k, total - 1)
    c_k = jnp.sum((kk[:, :, None] >= o_end[:, None, :]).astype(jnp.int32), axis=-1)
    take = lambda a: jnp.take_along_axis(a, c_k, axis=1)
    tile_k = take(first_tile) + kk - take(o_start)
    valid = (k < total).astype(jnp.int32)
    lo = jnp.maximum(take(starts), tile_k * TM) - tile_k * TM
    hi = jnp.minimum(take(starts + counts), (tile_k + 1) * TM) - tile_k * TM
    prev_tile = jnp.concatenate([jnp.full((b, 1), -1, jnp.int32), tile_k[:, :-1]], axis=1)
    first = (tile_k != prev_tile).astype(jnp.int32)
    pair_lo = jnp.asarray([p[0] for p in _PAIRS], jnp.int32)
    pair_hi = jnp.asarray([p[1] for p in _PAIRS], jnp.int32)
    e_lo = (c_k // N_PAIRS) * EXPERTS_PER_GROUP + pair_lo[c_k % N_PAIRS]
    e_hi = (c_k // N_PAIRS) * EXPERTS_PER_GROUP + pair_hi[c_k % N_PAIRS]
    tile_g = tile_k + bi * n_tiles
    items = tuple(a.reshape(-1).astype(jnp.int32) for a in (tile_g, e_lo, e_hi, lo, hi, valid, first))
    return pos.reshape(-1).astype(jnp.int32), src.reshape(-1), items


def _rope_tables(seq, ctx):
    t = np.arange(seq)
    axis_dim = QK_ROPE // 2
    inv = jnp.asarray(ROPE_BASE, F32) ** (-jnp.arange(0, axis_dim, 2, dtype=F32) / axis_dim)
    ang_r = jnp.asarray(t // GRID_W, F32)[:, None] * inv
    ang_c = jnp.asarray(t % GRID_W, F32)[:, None] * inv
    cos = jnp.concatenate([jnp.cos(ang_r), jnp.cos(ang_r), jnp.cos(ang_c), jnp.cos(ang_c)], axis=-1)
    sin = jnp.concatenate([-jnp.sin(ang_r), jnp.sin(ang_r), -jnp.sin(ang_c), jnp.sin(ang_c)], axis=-1)
    pad = HEAD_PAD - QK_NOPE - QK_ROPE
    cos = jnp.concatenate([jnp.ones((seq, QK_NOPE), F32), cos, jnp.ones((seq, pad), F32)], axis=-1)
    sin = jnp.concatenate([jnp.zeros((seq, QK_NOPE), F32), sin, jnp.zeros((seq, pad), F32)], axis=-1)
    cos = jnp.concatenate([cos, jnp.ones((ctx, HEAD_PAD), F32)], axis=0)
    sin = jnp.concatenate([sin, jnp.zeros((ctx, HEAD_PAD), F32)], axis=0)
    return cos, sin


_ROPE_SWAP = np.concatenate([np.arange(8, 16), np.arange(0, 8), np.arange(24, 32), np.arange(16, 24)])


def _layout_w_in(w_in):
    d = w_in.shape[0]
    z = lambda n: jnp.zeros((d, n), w_in.dtype)
    kr = w_in[:, _C_KR:_C_KR + QK_ROPE]
    pad = HEAD_PAD - QK_NOPE - QK_ROPE
    return jnp.concatenate([w_in[:, :_C_KR], z(QK_NOPE), kr, z(pad), z(QK_NOPE), kr[:, _ROPE_SWAP], z(pad)],
                           axis=1).astype(BF16)


def _layout_w_q(w_q):
    r = w_q.shape[0]
    wh = w_q.reshape(r, MLA_HEADS, QK_NOPE + QK_ROPE)
    pad = HEAD_PAD - QK_NOPE - QK_ROPE
    full = jnp.concatenate([wh, jnp.zeros((r, MLA_HEADS, pad), w_q.dtype)], axis=-1)
    swap = jnp.concatenate([jnp.zeros((r, MLA_HEADS, QK_NOPE), w_q.dtype), wh[:, :, QK_NOPE + _ROPE_SWAP],
                            jnp.zeros((r, MLA_HEADS, pad), w_q.dtype)], axis=-1)
    return jnp.concatenate([full.reshape(r, -1), swap.reshape(r, -1)], axis=1).astype(BF16)


def _layout_w_kv(w_kv):
    r = w_kv.shape[0]
    wh = w_kv.reshape(r, MLA_HEADS, QK_NOPE + V_HEAD)
    zk = jnp.zeros((r, MLA_HEADS, HEAD_PAD - QK_NOPE), w_kv.dtype)
    zv = jnp.zeros((r, MLA_HEADS, HEAD_PAD - V_HEAD), w_kv.dtype)
    k = jnp.concatenate([wh[:, :, :QK_NOPE], zk], axis=-1).reshape(r, -1)
    v = jnp.concatenate([wh[:, :, QK_NOPE:], zv], axis=-1).reshape(r, -1)
    return jnp.concatenate([k, v], axis=1).astype(BF16)


def _block_diag(w):
    nd, nb, k, _ = w.shape
    eye = jnp.eye(nb, dtype=w.dtype)
    return (w[:, :, :, None, :] * eye[None, :, None, :, None]).reshape(nd, nb * k, nb * k).astype(BF16)


def _pick_chunk(n, candidates):
    for c in candidates:
        if n % c == 0:
            return c
    raise ValueError(f"no chunk size for {n}")


def kernel(x, c, ctx, c_ctx, w_ada, b_ada, g_norm, w_in, w_out, na_rpb, conv_w, conv_b, lru_w_a, lru_b_a, lru_w_i,
           lru_b_i, lru_lam, mla_g_q, mla_w_q, mla_g_kv, mla_w_kv, router_w, router_b, exp_w_gate, exp_w_up,
           exp_w_down):
    b, seq, d = x.shape
    n_ctx = ctx.shape[1]
    depth = w_ada.shape[0]
    lt = seq + n_ctx
    assert d == D_MODEL == SUBLANES * LANES and n_ctx == TM and seq % TM == 0 and seq // GRID_W >= WIN_ROWS
    assert b + 1 <= 8
    tiles_per_sample = lt // TM
    n_lat_tiles = seq // TM
    seg_kw = dict(tiles_per_sample=tiles_per_sample, n_lat_tiles=n_lat_tiles, n_batch=b)

    xa = jnp.concatenate([x, ctx], axis=1).reshape(b * lt, d)

    c_rows = jnp.concatenate([c, c_ctx[None, :], jnp.zeros((8 - b - 1, d), F32)], axis=0)
    mod = _ada_tables(c_rows, w_ada, b_ada).reshape(depth, 8, N_MOD, 1, d)

    cos_t, sin_t = _rope_tables(seq, n_ctx)
    ones_row = jnp.zeros((MLA_HEADS, HEAD_PAD), F32).at[:, V_HEAD].set(1.0).reshape(1, -1)
    rw = jnp.concatenate([router_w, jnp.zeros((d, LANES - N_EXPERTS), F32)], axis=1).astype(BF16)
    rb = jnp.concatenate([router_b, jnp.zeros((LANES - N_EXPERTS,), F32)])[None, :]
    tk = _pick_chunk(lt, (768, 512, 256))
    tq = _pick_chunk(seq, (2048, 1024, 512, 256))

    for l in range(depth):
        last = l == depth - 1
        m = lambda k: mod[l, :b + 1, k]
        g = g_norm[l]
        qkv, u, gg, mq, mk, mv = _proj_in(
            xa, m(0), m(1), g[0:1], _layout_w_in(w_in[l]), mla_g_q[l][None, :], _layout_w_q(mla_w_q[l]),
            mla_g_kv[l][None, :], _layout_w_kv(mla_w_kv[l]), cos_t, sin_t, ones_row, **seg_kw)

        ya = _na_attention(qkv.reshape(b, lt, -1), _na_bias_tables(na_rpb[l]), seq=seq, ctx=n_ctx)
        hscan = _lru_scan(u.reshape(b, lt, -1), conv_w[l], conv_b[l][None, :], _block_diag(lru_w_a[l]),
                          lru_b_a[l][:, None, :], _block_diag(lru_w_i[l]), lru_b_i[l][:, None, :],
                          lru_lam[l][:, None, :], seq=seq, ctx=n_ctx)
        mq3, mk3, mv3 = (a.reshape(b, lt, -1) for a in (mq, mk, mv))
        yc = _mla_attention(mq3, mk3, mv3, tq=tq, q_block0=0, n_q=seq // tq, k_block=lt, k_block0=0, tk=tk)
        if last:
            yc = jnp.concatenate([yc, jnp.zeros((b, n_ctx, MLA_W), BF16)], axis=1)
        else:
            yc_ctx = _mla_attention(mq3, mk3, mv3, tq=n_ctx, q_block0=seq // n_ctx, n_q=1, k_block=n_ctx,
                                    k_block0=seq // n_ctx, tk=n_ctx)
            yc = jnp.concatenate([yc, yc_ctx], axis=1)

        wo = w_out[l].astype(BF16)
        xa, h2t, cls = _proj_out(
            ya.reshape(b * lt, -1), hscan, gg, yc.reshape(b * lt, -1), xa,
            wo[:NA_W], wo[NA_W:NA_W + LRU_W], wo[NA_W + LRU_W:], g[1:2], g[2:3], m(2), m(3), m(4), rw, rb, **seg_kw)

        pos, src, items = _routing_plan(cls[:, 0].reshape(b, lt), n_tiles=tiles_per_sample)
        xs = _dispatch(src, h2t, n_batch=b, tokens_per_sample=lt)
        ys = _moe_routed(items, xs, rw, exp_w_gate[l].astype(BF16), exp_w_up[l].astype(BF16),
                         exp_w_down[l].astype(BF16))
        xa = _combine(pos, ys, xa, m(5), g[3:4], n_batch=b, tokens_per_sample=lt, n_lat_tiles=n_lat_tiles)

    return xa.reshape(b, lt, d)[:, :seq]
```

```python
import functools

import numpy as np
import jax
import jax.numpy as jnp
from jax import lax
from jax.experimental import pallas as pl
from jax.experimental.pallas import tpu as pltpu

F32 = jnp.float32
BF16 = jnp.bfloat16

D_MODEL = 1024
GRID_W = 64
NA_HEADS = 4
NA_HEAD_DIM = 64
NA_W = NA_HEADS * NA_HEAD_DIM
WIN_ROWS = 8
WIN_COLS = 16
LRU_W = 512
LRU_BLOCKS = 8
CONV_W = 4
LRU_C = 8.0
MLA_HEADS = 4
Q_LORA = 256
KV_LORA = 128
QK_NOPE = 64
QK_ROPE = 32
V_HEAD = 64
MLA_W = MLA_HEADS * V_HEAD
ROPE_BASE = 10000.0
N_EXPERTS = 16
N_GROUPS = 4
EXPERTS_PER_GROUP = 4
D_EXPERT = 512
RMS_EPS = 1e-6
N_MOD = 6

_PAIRS = ((0, 1), (0, 2), (0, 3), (1, 2), (1, 3), (2, 3))
N_PAIRS = len(_PAIRS)
N_CLASSES = N_GROUPS * N_PAIRS

LANES = 128
SUBLANES = 8
TM = 256
HEAD_PAD = 128
NEG_BIAS = -1e30
VMEM_LIMIT = 56 * 1024 * 1024

_C_QA, _C_KA, _C_VA = 0, 256, 512
_C_U = 768
_C_GATE = 1280
_C_CQ = 1792
_C_CKV = 2048
_C_KR = 2176
_C_KRP = 2304
_IN_COLS = 2432


def _params(sem, vmem=VMEM_LIMIT):
    return pltpu.CompilerParams(dimension_semantics=sem, vmem_limit_bytes=vmem)


def _sigmoid(v):
    return 1.0 / (1.0 + jnp.exp(-v))


def _rms(v, g):
    return v * lax.rsqrt(jnp.mean(v * v, axis=-1, keepdims=True) + RMS_EPS) * g


def _dot(a, b):
    return jnp.dot(a, b, preferred_element_type=F32)


def _dot_nt(a, b):
    return lax.dot_general(a, b, (((1,), (1,)), ((), ())), preferred_element_type=F32)


def _ada_body(c_ref, w_ref, b_ref, o_ref):
    c = c_ref[...]
    sc = c * _sigmoid(c)
    o_ref[0] = _dot(sc.astype(BF16), w_ref[0].astype(BF16)) + b_ref[0]


def _ada_tables(c_rows, w_ada, b_ada):
    depth, d, n = w_ada.shape
    tn = 1536
    return pl.pallas_call(
        _ada_body,
        grid=(depth, n // tn),
        in_specs=[pl.BlockSpec((8, d), lambda l, j: (0, 0)),
                  pl.BlockSpec((1, d, tn), lambda l, j: (l, 0, j)),
                  pl.BlockSpec((1, 1, tn), lambda l, j: (l, 0, j))],
        out_specs=pl.BlockSpec((1, 8, tn), lambda l, j: (l, 0, j)),
        out_shape=jax.ShapeDtypeStruct((depth, 8, n), F32),
        compiler_params=_params(("parallel", "parallel")),
        name="ada_tables",
    )(c_rows, w_ada, b_ada.reshape(depth, 1, n))


def _proj_in_body(x_ref, shift_ref, scale_ref, g_ref, w_ref, gq_ref, wq_ref, gkv_ref, wkv_ref,
                  cos_ref, sin_ref, ones_ref,
                  qkv_ref, u_ref, gg_ref, mq_ref, mk_ref, mv_ref, *, mla_scale):
    x = x_ref[...]
    h = _rms(x, g_ref[...]) * (1.0 + scale_ref[0]) + shift_ref[0]
    z = _dot(h.astype(BF16), w_ref[...])
    qkv_ref[:, 0:NA_W] = (z[:, _C_QA:_C_KA] * (NA_HEAD_DIM ** -0.5)).astype(BF16)
    qkv_ref[:, NA_W:3 * NA_W] = z[:, _C_KA:_C_U].astype(BF16)
    u_ref[...] = z[:, _C_U:_C_GATE]
    gg_ref[...] = jax.nn.gelu(z[:, _C_GATE:_C_CQ]).astype(BF16)

    cos = cos_ref[...]
    sin = sin_ref[...]
    cos4 = jnp.concatenate([cos] * MLA_HEADS, axis=-1)
    sin4 = jnp.concatenate([sin] * MLA_HEADS, axis=-1)
    nw = MLA_HEADS * HEAD_PAD

    nq = _rms(z[:, _C_CQ:_C_CKV], gq_ref[...])
    q2 = _dot(nq.astype(BF16), wq_ref[...])
    mq_ref[...] = ((q2[:, :nw] * cos4 + q2[:, nw:] * sin4) * mla_scale).astype(BF16)

    nkv = _rms(z[:, _C_CKV:_C_KR], gkv_ref[...])
    kv2 = _dot(nkv.astype(BF16), wkv_ref[...])
    k_rope = z[:, _C_KR:_C_KRP] * cos + z[:, _C_KRP:_IN_COLS] * sin
    mk_ref[...] = (kv2[:, :nw] + jnp.concatenate([k_rope] * MLA_HEADS, axis=-1)).astype(BF16)
    mv_ref[...] = (kv2[:, nw:] + ones_ref[...]).astype(BF16)


def _proj_in(x, shift, scale, g, w_big, gq, wq2, gkv, wkv2, cos_t, sin_t, ones_row, *, tiles_per_sample, n_lat_tiles,
             n_batch):
    t, d = x.shape
    nt = t // TM
    nw = MLA_HEADS * HEAD_PAD

    def seg(i):
        return jnp.where(i % tiles_per_sample >= n_lat_tiles, n_batch, i // tiles_per_sample)

    row = lambda n: pl.BlockSpec((TM, n), lambda i: (i, 0))
    full = lambda a: pl.BlockSpec(a.shape, lambda i: (0,) * a.ndim)
    modspec = pl.BlockSpec((1, 1, d), lambda i: (seg(i), 0, 0))
    tabspec = pl.BlockSpec((TM, HEAD_PAD), lambda i: (i % tiles_per_sample, 0))
    mla_scale = (QK_NOPE + QK_ROPE) ** -0.5
    return pl.pallas_call(
        functools.partial(_proj_in_body, mla_scale=mla_scale),
        grid=(nt,),
        in_specs=[row(d), modspec, modspec, full(g), full(w_big), full(gq), full(wq2), full(gkv), full(wkv2),
                  tabspec, tabspec, full(ones_row)],
        out_specs=[row(3 * NA_W), row(LRU_W), row(LRU_W), row(nw), row(nw), row(nw)],
        out_shape=[jax.ShapeDtypeStruct((t, 3 * NA_W), BF16),
                   jax.ShapeDtypeStruct((t, LRU_W), F32),
                   jax.ShapeDtypeStruct((t, LRU_W), BF16),
                   jax.ShapeDtypeStruct((t, nw), BF16),
                   jax.ShapeDtypeStruct((t, nw), BF16),
                   jax.ShapeDtypeStruct((t, nw), BF16)],
        compiler_params=_params(("parallel",)),
        name="proj_in",
    )(x, shift, scale, g, w_big, gq, wq2, gkv, wkv2, cos_t, sin_t, ones_row)


def _head_block_mask():
    r = lax.broadcasted_iota(jnp.int32, (NA_W, NA_W), 0) // NA_HEAD_DIM
    c = lax.broadcasted_iota(jnp.int32, (NA_W, NA_W), 1) // NA_HEAD_DIM
    return r == c


def _na_attend(q, parts, mask):
    qbig = jnp.where(mask, jnp.concatenate([q] * NA_HEADS, axis=0), jnp.zeros((), q.dtype))
    scores = []
    for k, _, bias in parts:
        s = _dot_nt(qbig, k)
        if bias is not None:
            s = s + bias
        scores.append(s)
    m = functools.reduce(jnp.maximum, [jnp.max(s, axis=-1, keepdims=True) for s in scores])
    ps = [jnp.exp(s - m) for s in scores]
    l = functools.reduce(jnp.add, [jnp.sum(p, axis=-1, keepdims=True) for p in ps])
    o = functools.reduce(jnp.add, [_dot(p.astype(BF16), v) for p, (_, v, _) in zip(ps, parts)])
    o = jnp.where(mask, o / l, 0.0)
    out = o[0:NA_HEAD_DIM]
    for h in range(1, NA_HEADS):
        out = out + o[h * NA_HEAD_DIM:(h + 1) * NA_HEAD_DIM]
    return out


def _na_body(q_ref, k_ref, v_ref, bias_ref, o_ref, *, rows, rows_per_step, n_lat_steps, seq, ctx):
    i = pl.program_id(1)
    mask = _head_block_mask()
    kc = k_ref[0, seq:seq + ctx, :]
    vc = v_ref[0, seq:seq + ctx, :]

    @pl.when(i < n_lat_steps)
    def _():
        for j in range(rows_per_step):
            r = i * rows_per_step + j
            rs = jnp.clip(r - WIN_ROWS // 2, 0, rows - WIN_ROWS)
            off = rs - r + (WIN_ROWS - 1)
            start = pl.multiple_of(rs * GRID_W, GRID_W)
            kw = k_ref[0, pl.ds(start, WIN_ROWS * GRID_W), :]
            vw = v_ref[0, pl.ds(start, WIN_ROWS * GRID_W), :]
            q = q_ref[0, j * GRID_W:(j + 1) * GRID_W, :]
            out = _na_attend(q, [(kw, vw, bias_ref[off]), (kc, vc, None)], mask)
            o_ref[0, j * GRID_W:(j + 1) * GRID_W, :] = out.astype(o_ref.dtype)

    @pl.when(i >= n_lat_steps)
    def _():
        for j in range(rows_per_step):
            q = q_ref[0, j * GRID_W:(j + 1) * GRID_W, :]
            out = _na_attend(q, [(kc, vc, None)], mask)
            o_ref[0, j * GRID_W:(j + 1) * GRID_W, :] = out.astype(o_ref.dtype)


def _na_attention(qkv, bias, *, seq, ctx):
    b, lt, _ = qkv.shape
    rows = seq // GRID_W
    rows_per_step = TM // GRID_W
    n_steps = lt // TM
    n_lat_steps = seq // TM
    body = functools.partial(_na_body, rows=rows, rows_per_step=rows_per_step, n_lat_steps=n_lat_steps,
                             seq=seq, ctx=ctx)
    return pl.pallas_call(
        body,
        grid=(b, n_steps),
        in_specs=[pl.BlockSpec((1, TM, NA_W), lambda bi, i: (bi, i, 0)),
                  pl.BlockSpec((1, lt, NA_W), lambda bi, i: (bi, 0, 1)),
                  pl.BlockSpec((1, lt, NA_W), lambda bi, i: (bi, 0, 2)),
                  pl.BlockSpec(bias.shape, lambda bi, i: (0, 0, 0))],
        out_specs=pl.BlockSpec((1, TM, NA_W), lambda bi, i: (bi, i, 0)),
        out_shape=jax.ShapeDtypeStruct((b, lt, NA_W), BF16),
        compiler_params=_params(("parallel", "arbitrary")),
        name="na_attention",
    )(qkv, qkv, qkv, bias)


def _na_bias_tables(rpb):
    q = np.arange(GRID_W)
    cs = np.clip(q - WIN_COLS // 2, 0, GRID_W - WIN_COLS)
    c = np.arange(GRID_W)
    inside = (c[None, :] >= cs[:, None]) & (c[None, :] < cs[:, None] + WIN_COLS)
    dc = c[None, :] - q[:, None] + (WIN_COLS - 1)
    place = ((np.arange(2 * WIN_COLS - 1)[:, None, None] == dc[None]) & inside[None]).astype(np.float32)
    win = jnp.stack([rpb[:, o:o + WIN_ROWS, :] for o in range(WIN_ROWS)]).astype(F32)
    vals = jnp.einsum('ohik,kqc->ohqic', win, jnp.asarray(place), precision=lax.Precision.HIGHEST)
    vals = jnp.where(inside[None, None, :, None, :], vals, NEG_BIAS)
    return vals.reshape(WIN_ROWS, NA_HEADS * GRID_W, WIN_ROWS * GRID_W)


def _softplus(v):
    return jnp.maximum(v, 0.0) + jnp.log1p(jnp.exp(-jnp.abs(v)))


def _lru_body(up_ref, uc_ref, un_ref, cw_ref, cb_ref, wa_ref, ba_ref, wi_ref, bi_ref, lam_ref,
              o_ref, a_buf, x_buf, st_ref, *, n_lat, chunk, n_batch):
    d = pl.program_id(0)
    i = pl.program_id(1)
    c = _lru_chunk(d, i, n_lat)
    prev_ok = jnp.logical_and(c >= 1, c < n_lat)
    next_ok = c <= n_lat - 2

    row = lax.broadcasted_iota(jnp.int32, (chunk, 1), 0)
    cw = cw_ref[...]
    log_decay = _softplus(-lam_ref[0])
    for b in range(n_batch):
        ub = uc_ref[b]
        p_row = jnp.where(prev_ok, up_ref[b, 7:8, :], 0.0)
        n0 = jnp.where(next_ok, un_ref[b, 0:1, :], 0.0)
        n1 = jnp.where(next_ok, un_ref[b, 1:2, :], 0.0)
        um1 = jnp.where(row == 0, p_row, pltpu.roll(ub, 1, 0))
        up1 = jnp.where(row == chunk - 1, n0, pltpu.roll(ub, chunk - 1, 0))
        up2 = jnp.where(row == chunk - 1, n1, jnp.where(row == chunk - 2, n0, pltpu.roll(ub, chunk - 2, 0)))
        cv = um1 * cw[0:1] + cb_ref[...]
        cv = cv + ub * cw[1:2]
        cv = cv + up1 * cw[2:3]
        cv = cv + up2 * cw[3:4]
        cv16 = cv.astype(BF16)
        r = _sigmoid(_dot(cv16, wa_ref[0]) + ba_ref[0])
        gi = _sigmoid(_dot(cv16, wi_ref[0]) + bi_ref[0])
        log_a = (-LRU_C * r) * log_decay
        a = jnp.exp(log_a)
        a_buf[b] = a
        x_buf[b] = jnp.sqrt(1.0 - a * a) * (gi * cv)

    @pl.when(i == 0)
    def _():
        st_ref[...] = jnp.zeros_like(st_ref)

    def step(t, hs):
        tt = jnp.where(d == 0, t, chunk - 1 - t)
        new = []
        for b in range(n_batch):
            h = a_buf[b, pl.ds(tt, 1), :] * hs[b] + x_buf[b, pl.ds(tt, 1), :]
            o_ref[0, b, pl.ds(tt, 1), :] = h
            new.append(h)
        return tuple(new)

    hs = lax.fori_loop(0, chunk, step, tuple(st_ref[b, 0:1, :] for b in range(n_batch)), unroll=8)
    for b in range(n_batch):
        st_ref[b, 0:1, :] = hs[b]


def _lru_chunk(d, i, n_lat):
    return jnp.where(i == 0, n_lat, jnp.where(d == 0, i - 1, n_lat - i))


def _lru_scan(u, conv_w, conv_b, w_a, b_a, w_i, b_i, lam, *, seq, ctx):
    b, lt, w = u.shape
    chunk = ctx
    n_lat = seq // chunk
    n_chunks = lt // chunk
    sub = chunk // 8
    n_sub = lt // 8

    cur = lambda d, i: (0, _lru_chunk(d, i, n_lat), 0)
    prev = lambda d, i: (0, jnp.maximum(_lru_chunk(d, i, n_lat) * sub - 1, 0), 0)
    nxt = lambda d, i: (0, jnp.minimum((_lru_chunk(d, i, n_lat) + 1) * sub, n_sub - 1), 0)
    per_dir = lambda a: pl.BlockSpec((1,) + a.shape[1:], lambda d, i: (d,) + (0,) * (a.ndim - 1))
    full = lambda a: pl.BlockSpec(a.shape, lambda d, i: (0,) * a.ndim)
    body = functools.partial(_lru_body, n_lat=n_lat, chunk=chunk, n_batch=b)
    return pl.pallas_call(
        body,
        grid=(2, n_chunks),
        in_specs=[pl.BlockSpec((b, 8, w), prev), pl.BlockSpec((b, chunk, w), cur), pl.BlockSpec((b, 8, w), nxt),
                  full(conv_w), full(conv_b), per_dir(w_a), per_dir(b_a), per_dir(w_i), per_dir(b_i), per_dir(lam)],
        out_specs=pl.BlockSpec((1, b, chunk, w), lambda d, i: (d, 0, _lru_chunk(d, i, n_lat), 0)),
        out_shape=jax.ShapeDtypeStruct((2, b, lt, w), F32),
        scratch_shapes=[pltpu.VMEM((b, chunk, w), F32), pltpu.VMEM((b, chunk, w), F32), pltpu.VMEM((b, 8, w), F32)],
        compiler_params=_params(("arbitrary", "arbitrary")),
        name="lru_scan",
    )(u, u, u, conv_w, conv_b, w_a, b_a, w_i, b_i, lam)


def _mla_body(q_ref, k_ref, v_ref, o_ref, m_ref, acc_ref, *, tk, nk):
    m_ref[...] = jnp.full(m_ref.shape, NEG_BIAS, F32)
    acc_ref[...] = jnp.zeros(acc_ref.shape, F32)

    def chunk(ci, carry):
        start = pl.multiple_of(ci * tk, tk)
        for h in range(MLA_HEADS):
            cols = slice(h * HEAD_PAD, (h + 1) * HEAD_PAD)
            s = _dot_nt(q_ref[0, :, cols], k_ref[0, pl.ds(start, tk), cols])
            blocks = [s[:, j * LANES:(j + 1) * LANES] for j in range(tk // LANES)]
            m_old = m_ref[h]
            m_new = jnp.maximum(m_old, jnp.max(functools.reduce(jnp.maximum, blocks), axis=-1, keepdims=True))
            p = jnp.concatenate([jnp.exp((blk - m_new).astype(BF16)) for blk in blocks], axis=-1)
            acc_ref[h] = jnp.exp(m_old - m_new) * acc_ref[h] + _dot(p, v_ref[0, pl.ds(start, tk), cols])
            m_ref[h] = m_new
        return carry

    lax.fori_loop(0, nk, chunk, 0)
    outs = [acc_ref[h][:, :V_HEAD] / acc_ref[h][:, V_HEAD:V_HEAD + 1] for h in range(MLA_HEADS)]
    o_ref[0] = jnp.concatenate(outs, axis=-1).astype(o_ref.dtype)


def _mla_attention(mq, mk, mv, *, tq, q_block0, n_q, k_block, k_block0, tk):
    b, lt, nw = mq.shape
    body = functools.partial(_mla_body, tk=tk, nk=k_block // tk)
    return pl.pallas_call(
        body,
        grid=(b, n_q),
        in_specs=[pl.BlockSpec((1, tq, nw), lambda bi, i: (bi, q_block0 + i, 0)),
                  pl.BlockSpec((1, k_block, nw), lambda bi, i: (bi, k_block0, 0), pipeline_mode=pl.Buffered(1)),
                  pl.BlockSpec((1, k_block, nw), lambda bi, i: (bi, k_block0, 0), pipeline_mode=pl.Buffered(1))],
        out_specs=pl.BlockSpec((1, tq, MLA_W), lambda bi, i: (bi, i, 0)),
        out_shape=jax.ShapeDtypeStruct((b, n_q * tq, MLA_W), BF16),
        scratch_shapes=[pltpu.VMEM((MLA_HEADS, tq, LANES), F32), pltpu.VMEM((MLA_HEADS, tq, HEAD_PAD), F32)],
        compiler_params=_params(("parallel", "arbitrary")),
        name="mla_attention",
    )(mq, mk, mv)


def _route(logits, rb):
    sel = _sigmoid(logits) + rb
    sc = [sel[j:j + 1, :] for j in range(N_EXPERTS)]
    gscore = []
    for g in range(N_GROUPS):
        a, b, c, d = sc[4 * g:4 * g + 4]
        hi1, lo1 = jnp.maximum(a, b), jnp.minimum(a, b)
        hi2, lo2 = jnp.maximum(c, d), jnp.minimum(c, d)
        gscore.append(jnp.maximum(hi1, hi2) + jnp.maximum(jnp.minimum(hi1, hi2), jnp.maximum(lo1, lo2)))
    best = jnp.zeros_like(gscore[0], dtype=jnp.int32)
    bval = gscore[0]
    for g in range(1, N_GROUPS):
        upd = gscore[g] > bval
        best = jnp.where(upd, g, best)
        bval = jnp.where(upd, gscore[g], bval)

    def pick(cols, j):
        out = cols[j]
        for g in range(1, N_GROUPS):
            out = jnp.where(best == g, cols[4 * g + j], out)
        return out

    v = [pick(sc, j) for j in range(EXPERTS_PER_GROUP)]
    code = jnp.zeros_like(best)
    for i in range(EXPERTS_PER_GROUP):
        rank = jnp.zeros_like(best)
        for j in range(EXPERTS_PER_GROUP):
            if j == i:
                continue
            beats = (v[j] > v[i]) if j > i else (v[j] >= v[i])
            rank = rank + beats.astype(jnp.int32)
        code = code + jnp.where(rank < 2, 1 << i, 0)
    pair = jnp.zeros_like(best)
    for p, (lo, hi) in enumerate(_PAIRS):
        pair = jnp.where(code == (1 << lo) + (1 << hi), p, pair)
    return best * N_PAIRS + pair


def _proj_out_body(ya_ref, hf_ref, hr_ref, gg_ref, yc_ref, x_ref, wa_ref, wb_ref, wc_ref, g1_ref, g2_ref,
                   gate_ref, shift_ref, scale_ref, rwt_ref, rb_ref, xo_ref, h2_ref, cls_ref):
    yb = (hf_ref[0, 0] + hr_ref[0, 0]) * gg_ref[...].astype(F32)
    y = _dot(ya_ref[...], wa_ref[...]) + _dot(yb.astype(BF16), wb_ref[...]) + _dot(yc_ref[...], wc_ref[...])
    x = x_ref[...] + gate_ref[0] * _rms(y, g1_ref[...])
    xo_ref[...] = x
    h2 = (_rms(x, g2_ref[...]) * (1.0 + scale_ref[0]) + shift_ref[0]).astype(BF16)
    _store_token_tiles(h2_ref, h2.astype(F32))
    cls_ref[0] = _route(_dot_nt(rwt_ref[...], h2), rb_ref[...])


def _proj_out(ya, hscan, gg, yc, x, w_a, w_b, w_c, g1, g2, gate, shift, scale, rw, rb, *, tiles_per_sample,
              n_lat_tiles, n_batch):
    t, d = x.shape
    nt = t // TM

    def seg(i):
        return jnp.where(i % tiles_per_sample >= n_lat_tiles, n_batch, i // tiles_per_sample)

    row = lambda n: pl.BlockSpec((TM, n), lambda i: (i, 0))
    full = lambda a: pl.BlockSpec(a.shape, lambda i: (0,) * a.ndim)
    modspec = pl.BlockSpec((1, 1, d), lambda i: (seg(i), 0, 0))
    scan_spec = lambda dd: pl.BlockSpec((1, 1, TM, LRU_W),
                                        lambda i: (dd, i // tiles_per_sample, i % tiles_per_sample, 0))
    return pl.pallas_call(
        _proj_out_body,
        grid=(nt,),
        in_specs=[row(NA_W), scan_spec(0), scan_spec(1), row(LRU_W), row(MLA_W), row(d),
                  full(w_a), full(w_b), full(w_c), full(g1), full(g2), modspec, modspec, modspec,
                  full(rw), full(rb)],
        out_specs=[row(d), pl.BlockSpec((TM * SUBLANES, LANES), lambda i: (i, 0)),
                   pl.BlockSpec((1, 1, TM), lambda i: (i, 0, 0))],
        out_shape=[jax.ShapeDtypeStruct((t, d), F32),
                   jax.ShapeDtypeStruct((t * SUBLANES, LANES), F32),
                   jax.ShapeDtypeStruct((nt, 1, TM), jnp.int32)],
        compiler_params=_params(("parallel",)),
        name="proj_out",
    )(ya, hscan, hscan, gg, yc, x, w_a, w_b, w_c, g1, g2, gate, shift, scale, rw, rb)


def _store_token_tiles(ref, v):
    n = v.shape[0]
    for s in range(SUBLANES):
        ref[pl.ds(s, n, stride=SUBLANES), :] = v[:, s * LANES:(s + 1) * LANES]


def _load_token_tiles(ref, n):
    return jnp.concatenate([ref[pl.ds(s, n, stride=SUBLANES), :] for s in range(SUBLANES)], axis=1)


def _gather_token_tiles(idx_ref, base, src_ref, buf_ref, n):
    def copy(r, carry):
        j = idx_ref[base + r]
        buf_ref[pl.ds(pl.multiple_of(r * SUBLANES, SUBLANES), SUBLANES), :] = (
            src_ref[0, pl.ds(pl.multiple_of(j * SUBLANES, SUBLANES), SUBLANES), :])
        return carry

    lax.fori_loop(0, n, copy, 0, unroll=8)


def _dispatch_body(src_ref, h_ref, o_ref, buf_ref, *, tokens_per_sample):
    base = pl.program_id(0) * tokens_per_sample + pl.program_id(1) * TM
    _gather_token_tiles(src_ref, base, h_ref, buf_ref, TM)
    o_ref[...] = _load_token_tiles(buf_ref, TM).astype(o_ref.dtype)


def _dispatch(src, h2t, *, n_batch, tokens_per_sample):
    n_tiles = tokens_per_sample // TM
    return pl.pallas_call(
        functools.partial(_dispatch_body, tokens_per_sample=tokens_per_sample),
        grid_spec=pltpu.PrefetchScalarGridSpec(
            num_scalar_prefetch=1,
            grid=(n_batch, n_tiles),
            in_specs=[pl.BlockSpec((1, tokens_per_sample * SUBLANES, LANES), lambda b, i, src: (b, 0, 0),
                                   pipeline_mode=pl.Buffered(1))],
            out_specs=pl.BlockSpec((TM, D_MODEL), lambda b, i, src: (b * n_tiles + i, 0)),
            scratch_shapes=[pltpu.VMEM((TM * SUBLANES, LANES), F32)]),
        out_shape=jax.ShapeDtypeStruct((n_batch * tokens_per_sample, D_MODEL), BF16),
        compiler_params=_params(("arbitrary", "arbitrary")),
        name="moe_dispatch",
    )(src, h2t.reshape(n_batch, tokens_per_sample * SUBLANES, LANES))


def _moe_body(tile_ref, elo_ref, ehi_ref, lo_ref, hi_ref, valid_ref, first_ref,
              x_ref, rw_ref, wg0_ref, wu0_ref, wd0_ref, wg1_ref, wu1_ref, wd1_ref, o_ref):
    k = pl.program_id(0)

    @pl.when(valid_ref[k] == 1)
    def _():
        x = x_ref[...]
        aff = _sigmoid(_dot(x, rw_ref[...]))
        lane = lax.broadcasted_iota(jnp.int32, aff.shape, 1)
        a_lo = jnp.sum(jnp.where(lane == elo_ref[k], aff, 0.0), axis=-1, keepdims=True)
        a_hi = jnp.sum(jnp.where(lane == ehi_ref[k], aff, 0.0), axis=-1, keepdims=True)
        den = a_lo + a_hi

        def expert(wg, wu, wd):
            a = _dot(x, wg[0])
            he = (a * _sigmoid(a)) * _dot(x, wu[0])
            return _dot(he.astype(BF16), wd[0])

        y = (a_lo / den) * expert(wg0_ref, wu0_ref, wd0_ref) + (a_hi / den) * expert(wg1_ref, wu1_ref, wd1_ref)
        row = lax.broadcasted_iota(jnp.int32, (TM, 1), 0)
        y = jnp.where(jnp.logical_and(row >= lo_ref[k], row < hi_ref[k]), y, 0.0)

        @pl.when(first_ref[k] == 1)
        def _():
            _store_token_tiles(o_ref, y)

        @pl.when(first_ref[k] == 0)
        def _():
            _store_token_tiles(o_ref, _load_token_tiles(o_ref, TM) + y)


def _moe_routed(items, xs, rw, w_gate, w_up, w_down):
    t, d = xs.shape
    _, _, de = w_gate.shape
    n_items = items[0].shape[0]
    xspec = pl.BlockSpec((TM, d), lambda k, tile, *_: (tile[k], 0))
    w_in_spec = lambda which: pl.BlockSpec((1, d, de), lambda k, tile, elo, ehi, *_: ((elo, ehi)[which][k], 0, 0))
    w_out_spec = lambda which: pl.BlockSpec((1, de, d), lambda k, tile, elo, ehi, *_: ((elo, ehi)[which][k], 0, 0))
    return pl.pallas_call(
        _moe_body,
        grid_spec=pltpu.PrefetchScalarGridSpec(
            num_scalar_prefetch=len(items),
            grid=(n_items,),
            in_specs=[xspec, pl.BlockSpec(rw.shape, lambda k, *_: (0, 0)),
                      w_in_spec(0), w_in_spec(0), w_out_spec(0), w_in_spec(1), w_in_spec(1), w_out_spec(1)],
            out_specs=pl.BlockSpec((TM * SUBLANES, LANES), lambda k, tile, *_: (tile[k], 0))),
        out_shape=jax.ShapeDtypeStruct((t * SUBLANES, LANES), F32),
        compiler_params=_params(("arbitrary",)),
        name="moe_routed",
    )(*items, xs, rw, w_gate, w_up, w_down, w_gate, w_up, w_down)


def _combine_body(pos_ref, y_ref, x_ref, gate_ref, g_ref, o_ref, buf_ref, *, tokens_per_sample):
    base = pl.program_id(0) * tokens_per_sample + pl.program_id(1) * TM
    _gather_token_tiles(pos_ref, base, y_ref, buf_ref, TM)
    o_ref[...] = x_ref[...] + gate_ref[0] * _rms(_load_token_tiles(buf_ref, TM), g_ref[...])


def _combine(pos, ys, x, gate, g, *, n_batch, tokens_per_sample, n_lat_tiles, n_out_tiles):
    n_tiles = tokens_per_sample // TM
    d = x.shape[1]
    return pl.pallas_call(
        functools.partial(_combine_body, tokens_per_sample=tokens_per_sample),
        grid_spec=pltpu.PrefetchScalarGridSpec(
            num_scalar_prefetch=1,
            grid=(n_batch, n_out_tiles),
            in_specs=[pl.BlockSpec((1, tokens_per_sample * SUBLANES, LANES), lambda b, i, pos: (b, 0, 0),
                                   pipeline_mode=pl.Buffered(1)),
                      pl.BlockSpec((TM, d), lambda b, i, pos: (b * n_tiles + i, 0)),
                      pl.BlockSpec((1, 1, d), lambda b, i, pos: (jnp.where(i >= n_lat_tiles, n_batch, b), 0, 0)),
                      pl.BlockSpec(g.shape, lambda b, i, pos: (0, 0))],
            out_specs=pl.BlockSpec((TM, d), lambda b, i, pos: (b * n_out_tiles + i, 0)),
            scratch_shapes=[pltpu.VMEM((TM * SUBLANES, LANES), F32)]),
        out_shape=jax.ShapeDtypeStruct((n_batch * n_out_tiles * TM, d), F32),
        compiler_params=_params(("arbitrary", "arbitrary")),
        name="moe_combine",
    )(pos, ys.reshape(n_batch, tokens_per_sample * SUBLANES, LANES), x, gate, g)


def _routing_plan(cls, *, n_tiles):
    b, l = cls.shape
    n_items = n_tiles + N_CLASSES - 1
    onehot = (cls[..., None] == jnp.arange(N_CLASSES, dtype=jnp.int32)).astype(jnp.int32)
    csum = jnp.cumsum(onehot, axis=1)
    counts = csum[:, -1]
    starts = jnp.cumsum(counts, axis=1) - counts
    pos = jnp.sum(onehot * (starts[:, None, :] + csum - 1), axis=-1)
    bi = jnp.arange(b, dtype=jnp.int32)[:, None]
    src = jnp.zeros((b, l), jnp.int32).at[bi, pos].set(jnp.broadcast_to(jnp.arange(l, dtype=jnp.int32), (b, l)))

    first_tile = starts // TM
    last_tile = (starts + counts - 1) // TM
    m = jnp.where(counts > 0, last_tile - first_tile + 1, 0)
    o_end = jnp.cumsum(m, axis=1)
    o_start = o_end - m
    total = o_end[:, -1:]
    k = jnp.arange(n_items, dtype=jnp.int32)[None, :]
    kk = jnp.minimum(k, total - 1)
    c_k = jnp.sum((kk[:, :, None] >= o_end[:, None, :]).astype(jnp.int32), axis=-1)
    take = lambda a: jnp.take_along_axis(a, c_k, axis=1)
    tile_k = take(first_tile) + kk - take(o_start)
    valid = (k < total).astype(jnp.int32)
    lo = jnp.maximum(take(starts), tile_k * TM) - tile_k * TM
    hi = jnp.minimum(take(starts + counts), (tile_k + 1) * TM) - tile_k * TM
    prev_tile = jnp.concatenate([jnp.full((b, 1), -1, jnp.int32), tile_k[:, :-1]], axis=1)
    first = (tile_k != prev_tile).astype(jnp.int32)
    pair_lo = jnp.asarray([p[0] for p in _PAIRS], jnp.int32)
    pair_hi = jnp.asarray([p[1] for p in _PAIRS], jnp.int32)
    e_lo = (c_k // N_PAIRS) * EXPERTS_PER_GROUP + pair_lo[c_k % N_PAIRS]
    e_hi = (c_k // N_PAIRS) * EXPERTS_PER_GROUP + pair_hi[c_k % N_PAIRS]
    tile_g = tile_k + bi * n_tiles
    items = tuple(a.reshape(-1).astype(jnp.int32) for a in (tile_g, e_lo, e_hi, lo, hi, valid, first))
    return pos.reshape(-1).astype(jnp.int32), src.reshape(-1), items


def _rope_tables(seq, ctx):
    t = np.arange(seq)
    axis_dim = QK_ROPE // 2
    inv = np.float32(ROPE_BASE) ** (-np.arange(0, axis_dim, 2, dtype=np.float32) / np.float32(axis_dim))
    ang_r = (t // GRID_W).astype(np.float32)[:, None] * inv
    ang_c = (t % GRID_W).astype(np.float32)[:, None] * inv
    cos = jnp.asarray(np.concatenate([np.cos(ang_r), np.cos(ang_r), np.cos(ang_c), np.cos(ang_c)], axis=-1))
    sin = jnp.asarray(np.concatenate([-np.sin(ang_r), np.sin(ang_r), -np.sin(ang_c), np.sin(ang_c)], axis=-1))
    pad = HEAD_PAD - QK_NOPE - QK_ROPE
    cos = jnp.concatenate([jnp.ones((seq, QK_NOPE), F32), cos, jnp.ones((seq, pad), F32)], axis=-1)
    sin = jnp.concatenate([jnp.zeros((seq, QK_NOPE), F32), sin, jnp.zeros((seq, pad), F32)], axis=-1)
    cos = jnp.concatenate([cos, jnp.ones((ctx, HEAD_PAD), F32)], axis=0)
    sin = jnp.concatenate([sin, jnp.zeros((ctx, HEAD_PAD), F32)], axis=0)
    return cos, sin


_ROPE_SWAP = np.concatenate([np.arange(8, 16), np.arange(0, 8), np.arange(24, 32), np.arange(16, 24)])


def _layout_w_in(w_in):
    d = w_in.shape[0]
    z = lambda n: jnp.zeros((d, n), w_in.dtype)
    kr = w_in[:, _C_KR:_C_KR + QK_ROPE]
    pad = HEAD_PAD - QK_NOPE - QK_ROPE
    return jnp.concatenate([w_in[:, :_C_KR], z(QK_NOPE), kr, z(pad), z(QK_NOPE), kr[:, _ROPE_SWAP], z(pad)],
                           axis=1).astype(BF16)


def _layout_w_q(w_q):
    r = w_q.shape[0]
    wh = w_q.reshape(r, MLA_HEADS, QK_NOPE + QK_ROPE)
    pad = HEAD_PAD - QK_NOPE - QK_ROPE
    full = jnp.concatenate([wh, jnp.zeros((r, MLA_HEADS, pad), w_q.dtype)], axis=-1)
    swap = jnp.concatenate([jnp.zeros((r, MLA_HEADS, QK_NOPE), w_q.dtype), wh[:, :, QK_NOPE + _ROPE_SWAP],
                            jnp.zeros((r, MLA_HEADS, pad), w_q.dtype)], axis=-1)
    return jnp.concatenate([full.reshape(r, -1), swap.reshape(r, -1)], axis=1).astype(BF16)


def _layout_w_kv(w_kv):
    r = w_kv.shape[0]
    wh = w_kv.reshape(r, MLA_HEADS, QK_NOPE + V_HEAD)
    zk = jnp.zeros((r, MLA_HEADS, HEAD_PAD - QK_NOPE), w_kv.dtype)
    zv = jnp.zeros((r, MLA_HEADS, HEAD_PAD - V_HEAD), w_kv.dtype)
    k = jnp.concatenate([wh[:, :, :QK_NOPE], zk], axis=-1).reshape(r, -1)
    v = jnp.concatenate([wh[:, :, QK_NOPE:], zv], axis=-1).reshape(r, -1)
    return jnp.concatenate([k, v], axis=1).astype(BF16)


def _block_diag(w):
    nd, nb, k, _ = w.shape
    eye = jnp.eye(nb, dtype=w.dtype)
    return (w[:, :, :, None, :] * eye[None, :, None, :, None]).reshape(nd, nb * k, nb * k).astype(BF16)


def _pick_chunk(n, candidates):
    for c in candidates:
        if n % c == 0:
            return c
    raise ValueError(f"no chunk size for {n}")


def kernel(x, c, ctx, c_ctx, w_ada, b_ada, g_norm, w_in, w_out, na_rpb, conv_w, conv_b, lru_w_a, lru_b_a, lru_w_i,
           lru_b_i, lru_lam, mla_g_q, mla_w_q, mla_g_kv, mla_w_kv, router_w, router_b, exp_w_gate, exp_w_up,
           exp_w_down):
    b, seq, d = x.shape
    n_ctx = ctx.shape[1]
    depth = w_ada.shape[0]
    lt = seq + n_ctx
    assert d == D_MODEL == SUBLANES * LANES and n_ctx == TM and seq % TM == 0 and seq // GRID_W >= WIN_ROWS
    assert b + 1 <= 8
    tiles_per_sample = lt // TM
    n_lat_tiles = seq // TM
    seg_kw = dict(tiles_per_sample=tiles_per_sample, n_lat_tiles=n_lat_tiles, n_batch=b)

    xa = jnp.concatenate([x, ctx], axis=1).reshape(b * lt, d)

    c_rows = jnp.concatenate([c, c_ctx[None, :], jnp.zeros((8 - b - 1, d), F32)], axis=0)
    mod = _ada_tables(c_rows, w_ada, b_ada).reshape(depth, 8, N_MOD, 1, d)

    cos_t, sin_t = _rope_tables(seq, n_ctx)
    ones_row = jnp.zeros((MLA_HEADS, HEAD_PAD), F32).at[:, V_HEAD].set(1.0).reshape(1, -1)
    rw = jnp.concatenate([router_w, jnp.zeros((d, LANES - N_EXPERTS), F32)], axis=1).astype(BF16)
    rwt = router_w.T.astype(BF16)
    rb = router_b[:, None]
    tk = _pick_chunk(lt, (768, 512, 256))
    tq = _pick_chunk(seq, (2048, 1024, 512, 256))

    for l in range(depth):
        last = l == depth - 1
        m = lambda k: mod[l, :b + 1, k]
        g = g_norm[l]
        qkv, u, gg, mq, mk, mv = _proj_in(
            xa, m(0), m(1), g[0:1], _layout_w_in(w_in[l]), mla_g_q[l][None, :], _layout_w_q(mla_w_q[l]),
            mla_g_kv[l][None, :], _layout_w_kv(mla_w_kv[l]), cos_t, sin_t, ones_row, **seg_kw)

        ya = _na_attention(qkv.reshape(b, lt, -1), _na_bias_tables(na_rpb[l]), seq=seq, ctx=n_ctx)
        hscan = _lru_scan(u.reshape(b, lt, -1), conv_w[l], conv_b[l][None, :], _block_diag(lru_w_a[l]),
                          lru_b_a[l][:, None, :], _block_diag(lru_w_i[l]), lru_b_i[l][:, None, :],
                          lru_lam[l][:, None, :], seq=seq, ctx=n_ctx)
        mq3, mk3, mv3 = (a.reshape(b, lt, -1) for a in (mq, mk, mv))
        yc = _mla_attention(mq3, mk3, mv3, tq=tq, q_block0=0, n_q=seq // tq, k_block=lt, k_block0=0, tk=tk)
        if last:
            yc = jnp.concatenate([yc, jnp.zeros((b, n_ctx, MLA_W), BF16)], axis=1)
        else:
            yc_ctx = _mla_attention(mq3, mk3, mv3, tq=n_ctx, q_block0=seq // n_ctx, n_q=1, k_block=n_ctx,
                                    k_block0=seq // n_ctx, tk=n_ctx)
            yc = jnp.concatenate([yc, yc_ctx], axis=1)

        wo = w_out[l].astype(BF16)
        xa, h2t, cls = _proj_out(
            ya.reshape(b * lt, -1), hscan, gg, yc.reshape(b * lt, -1), xa,
            wo[:NA_W], wo[NA_W:NA_W + LRU_W], wo[NA_W + LRU_W:], g[1:2], g[2:3], m(2), m(3), m(4), rwt, rb, **seg_kw)

        pos, src, items = _routing_plan(cls.reshape(b, lt), n_tiles=tiles_per_sample)
        xs = _dispatch(src, h2t, n_batch=b, tokens_per_sample=lt)
        ys = _moe_routed(items, xs, rw, exp_w_gate[l].astype(BF16), exp_w_up[l].astype(BF16),
                         exp_w_down[l].astype(BF16))
        xa = _combine(pos, ys, xa, m(5), g[3:4], n_batch=b, tokens_per_sample=lt, n_lat_tiles=n_lat_tiles,
                      n_out_tiles=n_lat_tiles if last else tiles_per_sample)

    return xa.reshape(b, seq, d)
```

```python
import functools

import numpy as np
import jax
import jax.numpy as jnp
from jax import lax
from jax.experimental import pallas as pl
from jax.experimental.pallas import tpu as pltpu

F32 = jnp.float32
BF16 = jnp.bfloat16

D_MODEL = 1024
GRID_W = 64
NA_HEADS = 4
NA_HEAD_DIM = 64
NA_W = NA_HEADS * NA_HEAD_DIM
WIN_ROWS = 8
WIN_COLS = 16
LRU_W = 512
LRU_BLOCKS = 8
CONV_W = 4
LRU_C = 8.0
MLA_HEADS = 4
Q_LORA = 256
KV_LORA = 128
QK_NOPE = 64
QK_ROPE = 32
V_HEAD = 64
MLA_W = MLA_HEADS * V_HEAD
ROPE_BASE = 10000.0
N_EXPERTS = 16
N_GROUPS = 4
EXPERTS_PER_GROUP = 4
D_EXPERT = 512
RMS_EPS = 1e-6
N_MOD = 6

_PAIRS = ((0, 1), (0, 2), (0, 3), (1, 2), (1, 3), (2, 3))
N_PAIRS = len(_PAIRS)
N_CLASSES = N_GROUPS * N_PAIRS

LANES = 128
SUBLANES = 8
TM = 256
HEAD_PAD = 128
NEG_BIAS = -1e30
VMEM_LIMIT = 56 * 1024 * 1024

_C_QA, _C_KA, _C_VA = 0, 256, 512
_C_U = 768
_C_GATE = 1280
_C_CQ = 1792
_C_CKV = 2048
_C_KR = 2176
_C_KRP = 2304
_IN_COLS = 2432


def _params(sem, vmem=VMEM_LIMIT):
    return pltpu.CompilerParams(dimension_semantics=sem, vmem_limit_bytes=vmem)


def _sigmoid(v):
    return 1.0 / (1.0 + jnp.exp(-v))


def _rms(v, g):
    return v * lax.rsqrt(jnp.mean(v * v, axis=-1, keepdims=True) + RMS_EPS) * g


def _dot(a, b):
    return jnp.dot(a, b, preferred_element_type=F32)


def _dot_nt(a, b):
    return lax.dot_general(a, b, (((1,), (1,)), ((), ())), preferred_element_type=F32)


def _ada_body(c_ref, w_ref, b_ref, o_ref):
    c = c_ref[...]
    sc = c * _sigmoid(c)
    o_ref[0] = _dot(sc.astype(BF16), w_ref[0].astype(BF16)) + b_ref[0]


def _ada_tables(c_rows, w_ada, b_ada):
    depth, d, n = w_ada.shape
    tn = 1536
    return pl.pallas_call(
        _ada_body,
        grid=(depth, n // tn),
        in_specs=[pl.BlockSpec((8, d), lambda l, j: (0, 0)),
                  pl.BlockSpec((1, d, tn), lambda l, j: (l, 0, j)),
                  pl.BlockSpec((1, 1, tn), lambda l, j: (l, 0, j))],
        out_specs=pl.BlockSpec((1, 8, tn), lambda l, j: (l, 0, j)),
        out_shape=jax.ShapeDtypeStruct((depth, 8, n), F32),
        compiler_params=_params(("parallel", "parallel")),
        name="ada_tables",
    )(c_rows, w_ada, b_ada.reshape(depth, 1, n))


def _proj_in_body(x_ref, shift_ref, scale_ref, g_ref, w_ref, gq_ref, wq_ref, gkv_ref, wkv_ref,
                  cos_ref, sin_ref, ones_ref,
                  qkv_ref, u_ref, gg_ref, mq_ref, mk_ref, mv_ref, *, mla_scale):
    x = x_ref[...]
    h = _rms(x, g_ref[...]) * (1.0 + scale_ref[0]) + shift_ref[0]
    z = _dot(h.astype(BF16), w_ref[...])
    qkv_ref[:, 0:NA_W] = (z[:, _C_QA:_C_KA] * (NA_HEAD_DIM ** -0.5)).astype(BF16)
    qkv_ref[:, NA_W:3 * NA_W] = z[:, _C_KA:_C_U].astype(BF16)
    u_ref[...] = z[:, _C_U:_C_GATE]
    gg_ref[...] = jax.nn.gelu(z[:, _C_GATE:_C_CQ]).astype(BF16)

    cos = cos_ref[...]
    sin = sin_ref[...]
    cos4 = jnp.concatenate([cos] * MLA_HEADS, axis=-1)
    sin4 = jnp.concatenate([sin] * MLA_HEADS, axis=-1)
    nw = MLA_HEADS * HEAD_PAD

    nq = _rms(z[:, _C_CQ:_C_CKV], gq_ref[...])
    q2 = _dot(nq.astype(BF16), wq_ref[...])
    mq_ref[...] = ((q2[:, :nw] * cos4 + q2[:, nw:] * sin4) * mla_scale).astype(BF16)

    nkv = _rms(z[:, _C_CKV:_C_KR], gkv_ref[...])
    kv2 = _dot(nkv.astype(BF16), wkv_ref[...])
    k_rope = z[:, _C_KR:_C_KRP] * cos + z[:, _C_KRP:_IN_COLS] * sin
    mk_ref[...] = (kv2[:, :nw] + jnp.concatenate([k_rope] * MLA_HEADS, axis=-1)).astype(BF16)
    mv_ref[...] = (kv2[:, nw:] + ones_ref[...]).astype(BF16)


def _proj_in(x, shift, scale, g, w_big, gq, wq2, gkv, wkv2, cos_t, sin_t, ones_row, *, tiles_per_sample, n_lat_tiles,
             n_batch):
    t, d = x.shape
    nt = t // TM
    nw = MLA_HEADS * HEAD_PAD

    def seg(i):
        return jnp.where(i % tiles_per_sample >= n_lat_tiles, n_batch, i // tiles_per_sample)

    row = lambda n: pl.BlockSpec((TM, n), lambda i: (i, 0))
    full = lambda a: pl.BlockSpec(a.shape, lambda i: (0,) * a.ndim)
    modspec = pl.BlockSpec((1, 1, d), lambda i: (seg(i), 0, 0))
    tabspec = pl.BlockSpec((TM, HEAD_PAD), lambda i: (i % tiles_per_sample, 0))
    mla_scale = (QK_NOPE + QK_ROPE) ** -0.5
    return pl.pallas_call(
        functools.partial(_proj_in_body, mla_scale=mla_scale),
        grid=(nt,),
        in_specs=[row(d), modspec, modspec, full(g), full(w_big), full(gq), full(wq2), full(gkv), full(wkv2),
                  tabspec, tabspec, full(ones_row)],
        out_specs=[row(3 * NA_W), row(LRU_W), row(LRU_W), row(nw), row(nw), row(nw)],
        out_shape=[jax.ShapeDtypeStruct((t, 3 * NA_W), BF16),
                   jax.ShapeDtypeStruct((t, LRU_W), F32),
                   jax.ShapeDtypeStruct((t, LRU_W), BF16),
                   jax.ShapeDtypeStruct((t, nw), BF16),
                   jax.ShapeDtypeStruct((t, nw), BF16),
                   jax.ShapeDtypeStruct((t, nw), BF16)],
        compiler_params=_params(("parallel",)),
        name="proj_in",
    )(x, shift, scale, g, w_big, gq, wq2, gkv, wkv2, cos_t, sin_t, ones_row)


def _head_block_mask():
    r = lax.broadcasted_iota(jnp.int32, (NA_W, NA_W), 0) // NA_HEAD_DIM
    c = lax.broadcasted_iota(jnp.int32, (NA_W, NA_W), 1) // NA_HEAD_DIM
    return r == c


def _na_attend(q, parts, mask):
    qbig = jnp.where(mask, jnp.concatenate([q] * NA_HEADS, axis=0), jnp.zeros((), q.dtype))
    scores = []
    for k, _, bias in parts:
        s = _dot_nt(qbig, k)
        if bias is not None:
            s = s + bias
        scores.append(s)
    m = functools.reduce(jnp.maximum, [jnp.max(s, axis=-1, keepdims=True) for s in scores])
    ps = [jnp.exp(s - m) for s in scores]
    l = functools.reduce(jnp.add, [jnp.sum(p, axis=-1, keepdims=True) for p in ps])
    o = functools.reduce(jnp.add, [_dot(p.astype(BF16), v) for p, (_, v, _) in zip(ps, parts)])
    o = jnp.where(mask, o / l, 0.0)
    out = o[0:NA_HEAD_DIM]
    for h in range(1, NA_HEADS):
        out = out + o[h * NA_HEAD_DIM:(h + 1) * NA_HEAD_DIM]
    return out


def _na_body(q_ref, k_ref, v_ref, bias_ref, o_ref, *, rows, rows_per_step, n_lat_steps, seq, ctx):
    i = pl.program_id(1)
    mask = _head_block_mask()
    kc = k_ref[0, seq:seq + ctx, :]
    vc = v_ref[0, seq:seq + ctx, :]

    @pl.when(i < n_lat_steps)
    def _():
        for j in range(rows_per_step):
            r = i * rows_per_step + j
            rs = jnp.clip(r - WIN_ROWS // 2, 0, rows - WIN_ROWS)
            off = rs - r + (WIN_ROWS - 1)
            start = pl.multiple_of(rs * GRID_W, GRID_W)
            kw = k_ref[0, pl.ds(start, WIN_ROWS * GRID_W), :]
            vw = v_ref[0, pl.ds(start, WIN_ROWS * GRID_W), :]
            q = q_ref[0, j * GRID_W:(j + 1) * GRID_W, :]
            out = _na_attend(q, [(kw, vw, bias_ref[off]), (kc, vc, None)], mask)
            o_ref[0, j * GRID_W:(j + 1) * GRID_W, :] = out.astype(o_ref.dtype)

    @pl.when(i >= n_lat_steps)
    def _():
        for j in range(rows_per_step):
            q = q_ref[0, j * GRID_W:(j + 1) * GRID_W, :]
            out = _na_attend(q, [(kc, vc, None)], mask)
            o_ref[0, j * GRID_W:(j + 1) * GRID_W, :] = out.astype(o_ref.dtype)


def _na_attention(qkv, bias, *, seq, ctx):
    b, lt, _ = qkv.shape
    rows = seq // GRID_W
    rows_per_step = TM // GRID_W
    n_steps = lt // TM
    n_lat_steps = seq // TM
    body = functools.partial(_na_body, rows=rows, rows_per_step=rows_per_step, n_lat_steps=n_lat_steps,
                             seq=seq, ctx=ctx)
    return pl.pallas_call(
        body,
        grid=(b, n_steps),
        in_specs=[pl.BlockSpec((1, TM, NA_W), lambda bi, i: (bi, i, 0)),
                  pl.BlockSpec((1, lt, NA_W), lambda bi, i: (bi, 0, 1)),
                  pl.BlockSpec((1, lt, NA_W), lambda bi, i: (bi, 0, 2)),
                  pl.BlockSpec(bias.shape, lambda bi, i: (0, 0, 0))],
        out_specs=pl.BlockSpec((1, TM, NA_W), lambda bi, i: (bi, i, 0)),
        out_shape=jax.ShapeDtypeStruct((b, lt, NA_W), BF16),
        compiler_params=_params(("parallel", "arbitrary")),
        name="na_attention",
    )(qkv, qkv, qkv, bias)


def _na_bias_tables(rpb):
    q = np.arange(GRID_W)
    cs = np.clip(q - WIN_COLS // 2, 0, GRID_W - WIN_COLS)
    c = np.arange(GRID_W)
    inside = (c[None, :] >= cs[:, None]) & (c[None, :] < cs[:, None] + WIN_COLS)
    dc = c[None, :] - q[:, None] + (WIN_COLS - 1)
    place = ((np.arange(2 * WIN_COLS - 1)[:, None, None] == dc[None]) & inside[None]).astype(np.float32)
    win = jnp.stack([rpb[:, o:o + WIN_ROWS, :] for o in range(WIN_ROWS)]).astype(F32)
    vals = jnp.einsum('ohik,kqc->ohqic', win, jnp.asarray(place), precision=lax.Precision.HIGHEST)
    vals = jnp.where(inside[None, None, :, None, :], vals, NEG_BIAS)
    return vals.reshape(WIN_ROWS, NA_HEADS * GRID_W, WIN_ROWS * GRID_W)


def _softplus(v):
    return jnp.maximum(v, 0.0) + jnp.log1p(jnp.exp(-jnp.abs(v)))


def _lru_body(up_ref, uc_ref, un_ref, cw_ref, cb_ref, wa_ref, ba_ref, wi_ref, bi_ref, lam_ref,
              o_ref, a_buf, x_buf, st_ref, *, n_lat, chunk, n_batch):
    d = pl.program_id(0)
    i = pl.program_id(1)
    c = _lru_chunk(d, i, n_lat)
    prev_ok = jnp.logical_and(c >= 1, c < n_lat)
    next_ok = c <= n_lat - 2

    row = lax.broadcasted_iota(jnp.int32, (chunk, 1), 0)
    cw = cw_ref[...]
    log_decay = -LRU_C * _softplus(-lam_ref[0])
    for b in range(n_batch):
        ub = uc_ref[b]
        p_row = jnp.where(prev_ok, up_ref[b, 7:8, :], 0.0)
        n0 = jnp.where(next_ok, un_ref[b, 0:1, :], 0.0)
        n1 = jnp.where(next_ok, un_ref[b, 1:2, :], 0.0)
        um1 = jnp.where(row == 0, p_row, pltpu.roll(ub, 1, 0))
        up1 = jnp.where(row == chunk - 1, n0, pltpu.roll(ub, chunk - 1, 0))
        up2 = jnp.where(row == chunk - 1, n1, jnp.where(row == chunk - 2, n0, pltpu.roll(ub, chunk - 2, 0)))
        cv = um1 * cw[0:1] + cb_ref[...]
        cv = cv + ub * cw[1:2]
        cv = cv + up1 * cw[2:3]
        cv = cv + up2 * cw[3:4]
        cv16 = cv.astype(BF16)
        r = _sigmoid(_dot(cv16, wa_ref[0]) + ba_ref[0])
        gi = _sigmoid(_dot(cv16, wi_ref[0]) + bi_ref[0])
        log_a = r * log_decay
        a = jnp.exp(log_a)
        a_buf[b] = a
        x_buf[b] = jnp.sqrt(1.0 - a * a) * (gi * cv)

    @pl.when(i == 0)
    def _():
        st_ref[...] = jnp.zeros_like(st_ref)

    def step(t, hs):
        tt = jnp.where(d == 0, t, chunk - 1 - t)
        new = []
        for b in range(n_batch):
            h = a_buf[b, pl.ds(tt, 1), :] * hs[b] + x_buf[b, pl.ds(tt, 1), :]
            o_ref[0, b, pl.ds(tt, 1), :] = h
            new.append(h)
        return tuple(new)

    hs = lax.fori_loop(0, chunk, step, tuple(st_ref[b, 0:1, :] for b in range(n_batch)), unroll=8)
    for b in range(n_batch):
        st_ref[b, 0:1, :] = hs[b]


def _lru_chunk(d, i, n_lat):
    return jnp.where(i == 0, n_lat, jnp.where(d == 0, i - 1, n_lat - i))


def _lru_scan(u, conv_w, conv_b, w_a, b_a, w_i, b_i, lam, *, seq, ctx):
    b, lt, w = u.shape
    chunk = ctx
    n_lat = seq // chunk
    n_chunks = lt // chunk
    sub = chunk // 8
    n_sub = lt // 8

    cur = lambda d, i: (0, _lru_chunk(d, i, n_lat), 0)
    prev = lambda d, i: (0, jnp.maximum(_lru_chunk(d, i, n_lat) * sub - 1, 0), 0)
    nxt = lambda d, i: (0, jnp.minimum((_lru_chunk(d, i, n_lat) + 1) * sub, n_sub - 1), 0)
    per_dir = lambda a: pl.BlockSpec((1,) + a.shape[1:], lambda d, i: (d,) + (0,) * (a.ndim - 1))
    full = lambda a: pl.BlockSpec(a.shape, lambda d, i: (0,) * a.ndim)
    body = functools.partial(_lru_body, n_lat=n_lat, chunk=chunk, n_batch=b)
    return pl.pallas_call(
        body,
        grid=(2, n_chunks),
        in_specs=[pl.BlockSpec((b, 8, w), prev), pl.BlockSpec((b, chunk, w), cur), pl.BlockSpec((b, 8, w), nxt),
                  full(conv_w), full(conv_b), per_dir(w_a), per_dir(b_a), per_dir(w_i), per_dir(b_i), per_dir(lam)],
        out_specs=pl.BlockSpec((1, b, chunk, w), lambda d, i: (d, 0, _lru_chunk(d, i, n_lat), 0)),
        out_shape=jax.ShapeDtypeStruct((2, b, lt, w), F32),
        scratch_shapes=[pltpu.VMEM((b, chunk, w), F32), pltpu.VMEM((b, chunk, w), F32), pltpu.VMEM((b, 8, w), F32)],
        compiler_params=_params(("arbitrary", "arbitrary")),
        name="lru_scan",
    )(u, u, u, conv_w, conv_b, w_a, b_a, w_i, b_i, lam)


def _mla_body(q_ref, k_ref, v_ref, o_ref, m_ref, acc_ref, *, tk, nk):
    m_ref[...] = jnp.full(m_ref.shape, NEG_BIAS, F32)
    acc_ref[...] = jnp.zeros(acc_ref.shape, F32)

    def chunk(ci, carry):
        start = pl.multiple_of(ci * tk, tk)
        for h in range(MLA_HEADS):
            cols = slice(h * HEAD_PAD, (h + 1) * HEAD_PAD)
            s = _dot_nt(q_ref[0, :, cols], k_ref[0, pl.ds(start, tk), cols])
            blocks = [s[:, j * LANES:(j + 1) * LANES] for j in range(tk // LANES)]
            m_old = m_ref[h]
            m_new = jnp.maximum(m_old, jnp.max(functools.reduce(jnp.maximum, blocks), axis=-1, keepdims=True))
            p = jnp.concatenate([jnp.exp((blk - m_new).astype(BF16)) for blk in blocks], axis=-1)
            acc_ref[h] = jnp.exp(m_old - m_new) * acc_ref[h] + _dot(p, v_ref[0, pl.ds(start, tk), cols])
            m_ref[h] = m_new
        return carry

    lax.fori_loop(0, nk, chunk, 0)
    outs = [acc_ref[h][:, :V_HEAD] / acc_ref[h][:, V_HEAD:V_HEAD + 1] for h in range(MLA_HEADS)]
    o_ref[0] = jnp.concatenate(outs, axis=-1).astype(o_ref.dtype)


def _mla_attention(mq, mk, mv, *, tq, q_block0, n_q, k_block, k_block0, tk):
    b, lt, nw = mq.shape
    body = functools.partial(_mla_body, tk=tk, nk=k_block // tk)
    return pl.pallas_call(
        body,
        grid=(b, n_q),
        in_specs=[pl.BlockSpec((1, tq, nw), lambda bi, i: (bi, q_block0 + i, 0)),
                  pl.BlockSpec((1, k_block, nw), lambda bi, i: (bi, k_block0, 0), pipeline_mode=pl.Buffered(1)),
                  pl.BlockSpec((1, k_block, nw), lambda bi, i: (bi, k_block0, 0), pipeline_mode=pl.Buffered(1))],
        out_specs=pl.BlockSpec((1, tq, MLA_W), lambda bi, i: (bi, i, 0)),
        out_shape=jax.ShapeDtypeStruct((b, n_q * tq, MLA_W), BF16),
        scratch_shapes=[pltpu.VMEM((MLA_HEADS, tq, LANES), F32), pltpu.VMEM((MLA_HEADS, tq, HEAD_PAD), F32)],
        compiler_params=_params(("parallel", "arbitrary")),
        name="mla_attention",
    )(mq, mk, mv)


def _route(logits, rb):
    sel = _sigmoid(logits) + rb
    sc = [sel[j:j + 1, :] for j in range(N_EXPERTS)]
    gscore = []
    for g in range(N_GROUPS):
        a, b, c, d = sc[4 * g:4 * g + 4]
        hi1, lo1 = jnp.maximum(a, b), jnp.minimum(a, b)
        hi2, lo2 = jnp.maximum(c, d), jnp.minimum(c, d)
        gscore.append(jnp.maximum(hi1, hi2) + jnp.maximum(jnp.minimum(hi1, hi2), jnp.maximum(lo1, lo2)))
    best = jnp.zeros_like(gscore[0], dtype=jnp.int32)
    bval = gscore[0]
    for g in range(1, N_GROUPS):
        upd = gscore[g] > bval
        best = jnp.where(upd, g, best)
        bval = jnp.where(upd, gscore[g], bval)

    def pick(cols, j):
        out = cols[j]
        for g in range(1, N_GROUPS):
            out = jnp.where(best == g, cols[4 * g + j], out)
        return out

    v = [pick(sc, j) for j in range(EXPERTS_PER_GROUP)]
    code = jnp.zeros_like(best)
    for i in range(EXPERTS_PER_GROUP):
        rank = jnp.zeros_like(best)
        for j in range(EXPERTS_PER_GROUP):
            if j == i:
                continue
            beats = (v[j] > v[i]) if j > i else (v[j] >= v[i])
            rank = rank + beats.astype(jnp.int32)
        code = code + jnp.where(rank < 2, 1 << i, 0)
    pair = jnp.zeros_like(best)
    for p, (lo, hi) in enumerate(_PAIRS):
        pair = jnp.where(code == (1 << lo) + (1 << hi), p, pair)
    return best * N_PAIRS + pair


def _proj_out_body(ya_ref, hf_ref, hr_ref, gg_ref, yc_ref, x_ref, wa_ref, wb_ref, wc_ref, g1_ref, g2_ref,
                   gate_ref, shift_ref, scale_ref, rwt_ref, rb_ref, xo_ref, h2_ref, cls_ref):
    yb = (hf_ref[0, 0] + hr_ref[0, 0]) * gg_ref[...].astype(F32)
    y = _dot(ya_ref[...], wa_ref[...]) + _dot(yb.astype(BF16), wb_ref[...]) + _dot(yc_ref[...], wc_ref[...])
    x = x_ref[...] + gate_ref[0] * _rms(y, g1_ref[...])
    xo_ref[...] = x
    h2 = (_rms(x, g2_ref[...]) * (1.0 + scale_ref[0]) + shift_ref[0]).astype(BF16)
    _store_token_tiles(h2_ref, h2.astype(F32))
    cls_ref[0] = _route(_dot_nt(rwt_ref[...], h2), rb_ref[...])


def _proj_out(ya, hscan, gg, yc, x, w_a, w_b, w_c, g1, g2, gate, shift, scale, rw, rb, *, tiles_per_sample,
              n_lat_tiles, n_batch):
    t, d = x.shape
    nt = t // TM

    def seg(i):
        return jnp.where(i % tiles_per_sample >= n_lat_tiles, n_batch, i // tiles_per_sample)

    row = lambda n: pl.BlockSpec((TM, n), lambda i: (i, 0))
    full = lambda a: pl.BlockSpec(a.shape, lambda i: (0,) * a.ndim)
    modspec = pl.BlockSpec((1, 1, d), lambda i: (seg(i), 0, 0))
    scan_spec = lambda dd: pl.BlockSpec((1, 1, TM, LRU_W),
                                        lambda i: (dd, i // tiles_per_sample, i % tiles_per_sample, 0))
    return pl.pallas_call(
        _proj_out_body,
        grid=(nt,),
        in_specs=[row(NA_W), scan_spec(0), scan_spec(1), row(LRU_W), row(MLA_W), row(d),
                  full(w_a), full(w_b), full(w_c), full(g1), full(g2), modspec, modspec, modspec,
                  full(rw), full(rb)],
        out_specs=[row(d), pl.BlockSpec((TM * SUBLANES, LANES), lambda i: (i, 0)),
                   pl.BlockSpec((1, 1, TM), lambda i: (i, 0, 0))],
        out_shape=[jax.ShapeDtypeStruct((t, d), F32),
                   jax.ShapeDtypeStruct((t * SUBLANES, LANES), F32),
                   jax.ShapeDtypeStruct((nt, 1, TM), jnp.int32)],
        compiler_params=_params(("parallel",)),
        name="proj_out",
    )(ya, hscan, hscan, gg, yc, x, w_a, w_b, w_c, g1, g2, gate, shift, scale, rw, rb)


def _store_token_tiles(ref, v):
    n = v.shape[0]
    for s in range(SUBLANES):
        ref[pl.ds(s, n, stride=SUBLANES), :] = v[:, s * LANES:(s + 1) * LANES]


def _load_token_tiles(ref, n):
    return jnp.concatenate([ref[pl.ds(s, n, stride=SUBLANES), :] for s in range(SUBLANES)], axis=1)


def _gather_token_tiles(idx_ref, base, src_ref, buf_ref, n):
    def copy(r, carry):
        j = idx_ref[base + r]
        buf_ref[pl.ds(pl.multiple_of(r * SUBLANES, SUBLANES), SUBLANES), :] = (
            src_ref[0, pl.ds(pl.multiple_of(j * SUBLANES, SUBLANES), SUBLANES), :])
        return carry

    lax.fori_loop(0, n, copy, 0, unroll=8)


def _dispatch_body(pos_ref, h_ref, o_ref, buf_ref, src_ref, *, tokens_per_sample):
    @pl.when(pl.program_id(1) == 0)
    def _():
        base = pl.program_id(0) * tokens_per_sample

        def invert(t, carry):
            src_ref[pos_ref[base + t]] = t
            return carry

        lax.fori_loop(0, tokens_per_sample, invert, 0, unroll=8)

    _gather_token_tiles(src_ref, pl.program_id(1) * TM, h_ref, buf_ref, TM)
    o_ref[...] = _load_token_tiles(buf_ref, TM).astype(o_ref.dtype)


def _dispatch(pos, h2t, *, n_batch, tokens_per_sample):
    n_tiles = tokens_per_sample // TM
    return pl.pallas_call(
        functools.partial(_dispatch_body, tokens_per_sample=tokens_per_sample),
        grid_spec=pltpu.PrefetchScalarGridSpec(
            num_scalar_prefetch=1,
            grid=(n_batch, n_tiles),
            in_specs=[pl.BlockSpec((1, tokens_per_sample * SUBLANES, LANES), lambda b, i, pos: (b, 0, 0),
                                   pipeline_mode=pl.Buffered(1))],
            out_specs=pl.BlockSpec((TM, D_MODEL), lambda b, i, pos: (b * n_tiles + i, 0)),
            scratch_shapes=[pltpu.VMEM((TM * SUBLANES, LANES), F32), pltpu.SMEM((tokens_per_sample,), jnp.int32)]),
        out_shape=jax.ShapeDtypeStruct((n_batch * tokens_per_sample, D_MODEL), BF16),
        compiler_params=_params(("arbitrary", "arbitrary")),
        name="moe_dispatch",
    )(pos, h2t.reshape(n_batch, tokens_per_sample * SUBLANES, LANES))


def _moe_body(tile_ref, elo_ref, ehi_ref, lo_ref, hi_ref, valid_ref, first_ref,
              x_ref, rw_ref, wg0_ref, wu0_ref, wd0_ref, wg1_ref, wu1_ref, wd1_ref, o_ref):
    k = pl.program_id(0)

    @pl.when(valid_ref[k] == 1)
    def _():
        x = x_ref[...]
        aff = _sigmoid(_dot(x, rw_ref[...]))
        lane = lax.broadcasted_iota(jnp.int32, aff.shape, 1)
        a_lo = jnp.sum(jnp.where(lane == elo_ref[k], aff, 0.0), axis=-1, keepdims=True)
        a_hi = jnp.sum(jnp.where(lane == ehi_ref[k], aff, 0.0), axis=-1, keepdims=True)
        den = a_lo + a_hi

        def expert(wg, wu, wd):
            a = _dot(x, wg[0])
            he = (a * _sigmoid(a)) * _dot(x, wu[0])
            return _dot(he.astype(BF16), wd[0])

        y = (a_lo / den) * expert(wg0_ref, wu0_ref, wd0_ref) + (a_hi / den) * expert(wg1_ref, wu1_ref, wd1_ref)
        row = lax.broadcasted_iota(jnp.int32, (TM, 1), 0)
        y = jnp.where(jnp.logical_and(row >= lo_ref[k], row < hi_ref[k]), y, 0.0)

        @pl.when(first_ref[k] == 1)
        def _():
            _store_token_tiles(o_ref, y)

        @pl.when(first_ref[k] == 0)
        def _():
            _store_token_tiles(o_ref, _load_token_tiles(o_ref, TM) + y)


def _moe_routed(items, xs, rw, w_gate, w_up, w_down, *, expert_base):
    t, d = xs.shape
    _, _, de = w_gate.shape
    n_items = items[0].shape[0]
    xspec = pl.BlockSpec((TM, d), lambda k, tile, *_: (tile[k], 0))
    w_in_spec = lambda which: pl.BlockSpec(
        (1, d, de), lambda k, tile, elo, ehi, *_: (expert_base + (elo, ehi)[which][k], 0, 0))
    w_out_spec = lambda which: pl.BlockSpec(
        (1, de, d), lambda k, tile, elo, ehi, *_: (expert_base + (elo, ehi)[which][k], 0, 0))
    return pl.pallas_call(
        _moe_body,
        grid_spec=pltpu.PrefetchScalarGridSpec(
            num_scalar_prefetch=len(items),
            grid=(n_items,),
            in_specs=[xspec, pl.BlockSpec(rw.shape, lambda k, *_: (0, 0)),
                      w_in_spec(0), w_in_spec(0), w_out_spec(0), w_in_spec(1), w_in_spec(1), w_out_spec(1)],
            out_specs=pl.BlockSpec((TM * SUBLANES, LANES), lambda k, tile, *_: (tile[k], 0))),
        out_shape=jax.ShapeDtypeStruct((t * SUBLANES, LANES), F32),
        compiler_params=_params(("arbitrary",)),
        name="moe_routed",
    )(*items, xs, rw, w_gate, w_up, w_down, w_gate, w_up, w_down)


def _combine_body(pos_ref, y_ref, x_ref, gate_ref, g_ref, o_ref, buf_ref, *, tokens_per_sample):
    base = pl.program_id(0) * tokens_per_sample + pl.program_id(1) * TM
    _gather_token_tiles(pos_ref, base, y_ref, buf_ref, TM)
    o_ref[...] = x_ref[...] + gate_ref[0] * _rms(_load_token_tiles(buf_ref, TM), g_ref[...])


def _combine(pos, ys, x, gate, g, *, n_batch, tokens_per_sample, n_lat_tiles, n_out_tiles):
    n_tiles = tokens_per_sample // TM
    d = x.shape[1]
    return pl.pallas_call(
        functools.partial(_combine_body, tokens_per_sample=tokens_per_sample),
        grid_spec=pltpu.PrefetchScalarGridSpec(
            num_scalar_prefetch=1,
            grid=(n_batch, n_out_tiles),
            in_specs=[pl.BlockSpec((1, tokens_per_sample * SUBLANES, LANES), lambda b, i, pos: (b, 0, 0),
                                   pipeline_mode=pl.Buffered(1)),
                      pl.BlockSpec((TM, d), lambda b, i, pos: (b * n_tiles + i, 0)),
                      pl.BlockSpec((1, 1, d), lambda b, i, pos: (jnp.where(i >= n_lat_tiles, n_batch, b), 0, 0)),
                      pl.BlockSpec(g.shape, lambda b, i, pos: (0, 0))],
            out_specs=pl.BlockSpec((TM, d), lambda b, i, pos: (b * n_out_tiles + i, 0)),
            scratch_shapes=[pltpu.VMEM((TM * SUBLANES, LANES), F32)]),
        out_shape=jax.ShapeDtypeStruct((n_batch * n_out_tiles * TM, d), F32),
        compiler_params=_params(("arbitrary", "arbitrary")),
        name="moe_combine",
    )(pos, ys.reshape(n_batch, tokens_per_sample * SUBLANES, LANES), x, gate, g)


def _routing_plan(cls, *, n_tiles):
    b, l = cls.shape
    n_items = n_tiles + N_CLASSES - 1
    onehot = (cls[..., None] == jnp.arange(N_CLASSES, dtype=jnp.int32)).astype(jnp.int32)
    csum = jnp.cumsum(onehot, axis=1)
    counts = csum[:, -1]
    starts = jnp.cumsum(counts, axis=1) - counts
    pos = jnp.sum(onehot * (starts[:, None, :] + csum - 1), axis=-1)
    bi = jnp.arange(b, dtype=jnp.int32)[:, None]

    first_tile = starts // TM
    last_tile = (starts + counts - 1) // TM
    m = jnp.where(counts > 0, last_tile - first_tile + 1, 0)
    o_end = jnp.cumsum(m, axis=1)
    o_start = o_end - m
    total = o_end[:, -1:]
    k = jnp.arange(n_items, dtype=jnp.int32)[None, :]
    kk = jnp.minimum(k, total - 1)
    c_k = jnp.sum((kk[:, :, None] >= o_end[:, None, :]).astype(jnp.int32), axis=-1)
    take = lambda a: jnp.take_along_axis(a, c_k, axis=1)
    tile_k = take(first_tile) + kk - take(o_start)
    valid = (k < total).astype(jnp.int32)
    lo = jnp.maximum(take(starts), tile_k * TM) - tile_k * TM
    hi = jnp.minimum(take(starts + counts), (tile_k + 1) * TM) - tile_k * TM
    prev_tile = jnp.concatenate([jnp.full((b, 1), -1, jnp.int32), tile_k[:, :-1]], axis=1)
    first = (tile_k != prev_tile).astype(jnp.int32)
    pair_lo = jnp.asarray([p[0] for p in _PAIRS], jnp.int32)
    pair_hi = jnp.asarray([p[1] for p in _PAIRS], jnp.int32)
    e_lo = (c_k // N_PAIRS) * EXPERTS_PER_GROUP + pair_lo[c_k % N_PAIRS]
    e_hi = (c_k // N_PAIRS) * EXPERTS_PER_GROUP + pair_hi[c_k % N_PAIRS]
    tile_g = tile_k + bi * n_tiles
    items = tuple(a.reshape(-1).astype(jnp.int32) for a in (tile_g, e_lo, e_hi, lo, hi, valid, first))
    return pos.reshape(-1).astype(jnp.int32), items


def _rope_tables(seq, ctx):
    t = np.arange(seq)
    axis_dim = QK_ROPE // 2
    inv = np.float32(ROPE_BASE) ** (-np.arange(0, axis_dim, 2, dtype=np.float32) / np.float32(axis_dim))
    ang_r = (t // GRID_W).astype(np.float32)[:, None] * inv
    ang_c = (t % GRID_W).astype(np.float32)[:, None] * inv
    cos = jnp.asarray(np.concatenate([np.cos(ang_r), np.cos(ang_r), np.cos(ang_c), np.cos(ang_c)], axis=-1))
    sin = jnp.asarray(np.concatenate([-np.sin(ang_r), np.sin(ang_r), -np.sin(ang_c), np.sin(ang_c)], axis=-1))
    pad = HEAD_PAD - QK_NOPE - QK_ROPE
    cos = jnp.concatenate([jnp.ones((seq, QK_NOPE), F32), cos, jnp.ones((seq, pad), F32)], axis=-1)
    sin = jnp.concatenate([jnp.zeros((seq, QK_NOPE), F32), sin, jnp.zeros((seq, pad), F32)], axis=-1)
    cos = jnp.concatenate([cos, jnp.ones((ctx, HEAD_PAD), F32)], axis=0)
    sin = jnp.concatenate([sin, jnp.zeros((ctx, HEAD_PAD), F32)], axis=0)
    return cos, sin


_ROPE_SWAP = np.concatenate([np.arange(8, 16), np.arange(0, 8), np.arange(24, 32), np.arange(16, 24)])


def _layout_w_in(w_in):
    d = w_in.shape[0]
    z = lambda n: jnp.zeros((d, n), w_in.dtype)
    kr = w_in[:, _C_KR:_C_KR + QK_ROPE]
    pad = HEAD_PAD - QK_NOPE - QK_ROPE
    return jnp.concatenate([w_in[:, :_C_KR], z(QK_NOPE), kr, z(pad), z(QK_NOPE), kr[:, _ROPE_SWAP], z(pad)],
                           axis=1).astype(BF16)


def _layout_w_q(w_q):
    r = w_q.shape[0]
    wh = w_q.reshape(r, MLA_HEADS, QK_NOPE + QK_ROPE)
    pad = HEAD_PAD - QK_NOPE - QK_ROPE
    full = jnp.concatenate([wh, jnp.zeros((r, MLA_HEADS, pad), w_q.dtype)], axis=-1)
    swap = jnp.concatenate([jnp.zeros((r, MLA_HEADS, QK_NOPE), w_q.dtype), wh[:, :, QK_NOPE + _ROPE_SWAP],
                            jnp.zeros((r, MLA_HEADS, pad), w_q.dtype)], axis=-1)
    return jnp.concatenate([full.reshape(r, -1), swap.reshape(r, -1)], axis=1).astype(BF16)


def _layout_w_kv(w_kv):
    r = w_kv.shape[0]
    wh = w_kv.reshape(r, MLA_HEADS, QK_NOPE + V_HEAD)
    zk = jnp.zeros((r, MLA_HEADS, HEAD_PAD - QK_NOPE), w_kv.dtype)
    zv = jnp.zeros((r, MLA_HEADS, HEAD_PAD - V_HEAD), w_kv.dtype)
    k = jnp.concatenate([wh[:, :, :QK_NOPE], zk], axis=-1).reshape(r, -1)
    v = jnp.concatenate([wh[:, :, QK_NOPE:], zv], axis=-1).reshape(r, -1)
    return jnp.concatenate([k, v], axis=1).astype(BF16)


def _block_diag(w):
    nd, nb, k, _ = w.shape
    eye = jnp.eye(nb, dtype=w.dtype)
    return (w[:, :, :, None, :] * eye[None, :, None, :, None]).reshape(nd, nb * k, nb * k).astype(BF16)


def _pick_chunk(n, candidates):
    for c in candidates:
        if n % c == 0:
            return c
    raise ValueError(f"no chunk size for {n}")


def kernel(x, c, ctx, c_ctx, w_ada, b_ada, g_norm, w_in, w_out, na_rpb, conv_w, conv_b, lru_w_a, lru_b_a, lru_w_i,
           lru_b_i, lru_lam, mla_g_q, mla_w_q, mla_g_kv, mla_w_kv, router_w, router_b, exp_w_gate, exp_w_up,
           exp_w_down):
    b, seq, d = x.shape
    n_ctx = ctx.shape[1]
    depth = w_ada.shape[0]
    lt = seq + n_ctx
    assert d == D_MODEL == SUBLANES * LANES and n_ctx == TM and seq % TM == 0 and seq // GRID_W >= WIN_ROWS
    assert b + 1 <= 8
    tiles_per_sample = lt // TM
    n_lat_tiles = seq // TM
    seg_kw = dict(tiles_per_sample=tiles_per_sample, n_lat_tiles=n_lat_tiles, n_batch=b)

    xa = jnp.concatenate([x, ctx], axis=1).reshape(b * lt, d)

    c_rows = jnp.concatenate([c, c_ctx[None, :], jnp.zeros((8 - b - 1, d), F32)], axis=0)
    mod = _ada_tables(c_rows, w_ada, b_ada).reshape(depth, 8, N_MOD, 1, d)

    cos_t, sin_t = _rope_tables(seq, n_ctx)
    ones_row = jnp.zeros((MLA_HEADS, HEAD_PAD), F32).at[:, V_HEAD].set(1.0).reshape(1, -1)
    rw = jnp.concatenate([router_w, jnp.zeros((d, LANES - N_EXPERTS), F32)], axis=1).astype(BF16)
    wg_all, wu_all, wd_all = (w.astype(BF16).reshape((depth * N_EXPERTS,) + w.shape[2:])
                              for w in (exp_w_gate, exp_w_up, exp_w_down))
    rwt = router_w.T.astype(BF16)
    rb = router_b[:, None]
    tk = _pick_chunk(lt, (768, 512, 256))
    tq = _pick_chunk(seq, (2048, 1024, 512, 256))

    for l in range(depth):
        last = l == depth - 1
        m = lambda k: mod[l, :b + 1, k]
        g = g_norm[l]
        qkv, u, gg, mq, mk, mv = _proj_in(
            xa, m(0), m(1), g[0:1], _layout_w_in(w_in[l]), mla_g_q[l][None, :], _layout_w_q(mla_w_q[l]),
            mla_g_kv[l][None, :], _layout_w_kv(mla_w_kv[l]), cos_t, sin_t, ones_row, **seg_kw)

        ya = _na_attention(qkv.reshape(b, lt, -1), _na_bias_tables(na_rpb[l]), seq=seq, ctx=n_ctx)
        hscan = _lru_scan(u.reshape(b, lt, -1), conv_w[l], conv_b[l][None, :], _block_diag(lru_w_a[l]),
                          lru_b_a[l][:, None, :], _block_diag(lru_w_i[l]), lru_b_i[l][:, None, :],
                          lru_lam[l][:, None, :], seq=seq, ctx=n_ctx)
        mq3, mk3, mv3 = (a.reshape(b, lt, -1) for a in (mq, mk, mv))
        yc = _mla_attention(mq3, mk3, mv3, tq=tq, q_block0=0, n_q=seq // tq, k_block=lt, k_block0=0, tk=tk)
        if last:
            yc = jnp.concatenate([yc, jnp.zeros((b, n_ctx, MLA_W), BF16)], axis=1)
        else:
            yc_ctx = _mla_attention(mq3, mk3, mv3, tq=n_ctx, q_block0=seq // n_ctx, n_q=1, k_block=n_ctx,
                                    k_block0=seq // n_ctx, tk=n_ctx)
            yc = jnp.concatenate([yc, yc_ctx], axis=1)

        wo = w_out[l].astype(BF16)
        xa, h2t, cls = _proj_out(
            ya.reshape(b * lt, -1), hscan, gg, yc.reshape(b * lt, -1), xa,
            wo[:NA_W], wo[NA_W:NA_W + LRU_W], wo[NA_W + LRU_W:], g[1:2], g[2:3], m(2), m(3), m(4), rwt, rb, **seg_kw)

        pos, items = _routing_plan(cls.reshape(b, lt), n_tiles=tiles_per_sample)
        xs = _dispatch(pos, h2t, n_batch=b, tokens_per_sample=lt)
        ys = _moe_routed(items, xs, rw, wg_all, wu_all, wd_all, expert_base=l * N_EXPERTS)
        xa = _combine(pos, ys, xa, m(5), g[3:4], n_batch=b, tokens_per_sample=lt, n_lat_tiles=n_lat_tiles,
                      n_out_tiles=n_lat_tiles if last else tiles_per_sample)

    return xa.reshape(b, seq, d)
```

```python
import functools

import numpy as np
import jax
import jax.numpy as jnp
from jax import lax
from jax.experimental import pallas as pl
from jax.experimental.pallas import tpu as pltpu

F32 = jnp.float32
BF16 = jnp.bfloat16

D_MODEL = 1024
GRID_W = 64
NA_HEADS = 4
NA_HEAD_DIM = 64
NA_W = NA_HEADS * NA_HEAD_DIM
WIN_ROWS = 8
WIN_COLS = 16
LRU_W = 512
LRU_BLOCKS = 8
CONV_W = 4
LRU_C = 8.0
MLA_HEADS = 4
Q_LORA = 256
KV_LORA = 128
QK_NOPE = 64
QK_ROPE = 32
V_HEAD = 64
MLA_W = MLA_HEADS * V_HEAD
ROPE_BASE = 10000.0
N_EXPERTS = 16
N_GROUPS = 4
EXPERTS_PER_GROUP = 4
D_EXPERT = 512
RMS_EPS = 1e-6
N_MOD = 6

_PAIRS = ((0, 1), (0, 2), (0, 3), (1, 2), (1, 3), (2, 3))
N_PAIRS = len(_PAIRS)
N_CLASSES = N_GROUPS * N_PAIRS

LANES = 128
SUBLANES = 8
TM = 256
PROJ_TM = 768
HEAD_PAD = 128
NEG_BIAS = -1e30
VMEM_LIMIT = 56 * 1024 * 1024

_C_QA, _C_KA, _C_VA = 0, 256, 512
_C_U = 768
_C_GATE = 1280
_C_CQ = 1792
_C_CKV = 2048
_C_KR = 2176
_C_KRP = 2304
_IN_COLS = 2432


def _params(sem, vmem=VMEM_LIMIT):
    return pltpu.CompilerParams(dimension_semantics=sem, vmem_limit_bytes=vmem)


def _sigmoid(v):
    return 1.0 / (1.0 + jnp.exp(-v))


def _rms(v, g):
    return v * lax.rsqrt(jnp.mean(v * v, axis=-1, keepdims=True) + RMS_EPS) * g


def _dot(a, b):
    return jnp.dot(a, b, preferred_element_type=F32)


def _dot_nt(a, b):
    return lax.dot_general(a, b, (((1,), (1,)), ((), ())), preferred_element_type=F32)


def _ada_body(c_ref, w_ref, b_ref, o_ref):
    c = c_ref[...]
    sc = c * _sigmoid(c)
    o_ref[0] = _dot(sc.astype(BF16), w_ref[0].astype(BF16)) + b_ref[0]


def _ada_tables(c_rows, w_ada, b_ada):
    depth, d, n = w_ada.shape
    tn = 1536
    return pl.pallas_call(
        _ada_body,
        grid=(depth, n // tn),
        in_specs=[pl.BlockSpec((8, d), lambda l, j: (0, 0)),
                  pl.BlockSpec((1, d, tn), lambda l, j: (l, 0, j)),
                  pl.BlockSpec((1, 1, tn), lambda l, j: (l, 0, j))],
        out_specs=pl.BlockSpec((1, 8, tn), lambda l, j: (l, 0, j)),
        out_shape=jax.ShapeDtypeStruct((depth, 8, n), F32),
        compiler_params=_params(("parallel", "parallel")),
        name="ada_tables",
    )(c_rows, w_ada, b_ada.reshape(depth, 1, n))


class _RowTiling:
    def __init__(self, seq, tokens_per_sample, n_batch, tm):
        assert tokens_per_sample % tm == 0
        self.tm = tm
        self.n_batch = n_batch
        self.tiles_per_sample = tokens_per_sample // tm
        self.ctx_tile = seq // tm
        self.split = seq % tm

    def mod_specs(self, d):
        return [pl.BlockSpec((1, 1, d), lambda i: (i // self.tiles_per_sample, 0, 0)),
                pl.BlockSpec((1, 1, d), lambda i: (self.n_batch, 0, 0))]

    def mod_rows(self, sample_ref, ctx_ref):
        j = pl.program_id(0) % self.tiles_per_sample
        top = jnp.where(j <= self.ctx_tile, sample_ref[0], ctx_ref[0])
        bot = jnp.where(j < self.ctx_tile, sample_ref[0], ctx_ref[0])
        return top, bot

    def by_rows(self, v, f_top, f_bot):
        if self.split == 0:
            return f_bot(v)
        return jnp.concatenate([f_top(v[:self.split]), f_bot(v[self.split:])], axis=0)


def _proj_in_body(x_ref, shift_s_ref, shift_c_ref, scale_s_ref, scale_c_ref, g_ref, w_ref, gq_ref, wq_ref,
                  gkv_ref, wkv_ref, cos_ref, sin_ref, ones_ref,
                  qkv_ref, u_ref, gg_ref, mq_ref, mk_ref, mv_ref, *, mla_scale, rt):
    x = x_ref[...]
    sh_t, sh_b = rt.mod_rows(shift_s_ref, shift_c_ref)
    sc_t, sc_b = rt.mod_rows(scale_s_ref, scale_c_ref)
    h = rt.by_rows(_rms(x, g_ref[...]), lambda v: v * (1.0 + sc_t) + sh_t, lambda v: v * (1.0 + sc_b) + sh_b)
    z = _dot(h.astype(BF16), w_ref[...])
    qkv_ref[:, 0:NA_W] = (z[:, _C_QA:_C_KA] * (NA_HEAD_DIM ** -0.5)).astype(BF16)
    qkv_ref[:, NA_W:3 * NA_W] = z[:, _C_KA:_C_U].astype(BF16)
    u_ref[...] = z[:, _C_U:_C_GATE]
    gg_ref[...] = jax.nn.gelu(z[:, _C_GATE:_C_CQ]).astype(BF16)

    cos = cos_ref[...]
    sin = sin_ref[...]
    cos4 = jnp.concatenate([cos] * MLA_HEADS, axis=-1)
    sin4 = jnp.concatenate([sin] * MLA_HEADS, axis=-1)
    nw = MLA_HEADS * HEAD_PAD

    nq = _rms(z[:, _C_CQ:_C_CKV], gq_ref[...])
    q2 = _dot(nq.astype(BF16), wq_ref[...])
    mq_ref[...] = ((q2[:, :nw] * cos4 + q2[:, nw:] * sin4) * mla_scale).astype(BF16)

    nkv = _rms(z[:, _C_CKV:_C_KR], gkv_ref[...])
    kv2 = _dot(nkv.astype(BF16), wkv_ref[...])
    k_rope = z[:, _C_KR:_C_KRP] * cos + z[:, _C_KRP:_IN_COLS] * sin
    mk_ref[...] = (kv2[:, :nw] + jnp.concatenate([k_rope] * MLA_HEADS, axis=-1)).astype(BF16)
    mv_ref[...] = (kv2[:, nw:] + ones_ref[...]).astype(BF16)


def _proj_in(x, shift, scale, g, w_big, gq, wq2, gkv, wkv2, cos_t, sin_t, ones_row, *, rt):
    t, d = x.shape
    tm = rt.tm
    nt = t // tm
    nw = MLA_HEADS * HEAD_PAD
    row = lambda n: pl.BlockSpec((tm, n), lambda i: (i, 0))
    full = lambda a: pl.BlockSpec(a.shape, lambda i: (0,) * a.ndim)
    tabspec = pl.BlockSpec((tm, HEAD_PAD), lambda i: (i % rt.tiles_per_sample, 0))
    mla_scale = (QK_NOPE + QK_ROPE) ** -0.5
    return pl.pallas_call(
        functools.partial(_proj_in_body, mla_scale=mla_scale, rt=rt),
        grid=(nt,),
        in_specs=[row(d), *rt.mod_specs(d), *rt.mod_specs(d), full(g), full(w_big), full(gq), full(wq2), full(gkv),
                  full(wkv2), tabspec, tabspec, full(ones_row)],
        out_specs=[row(3 * NA_W), row(LRU_W), row(LRU_W), row(nw), row(nw), row(nw)],
        out_shape=[jax.ShapeDtypeStruct((t, 3 * NA_W), BF16),
                   jax.ShapeDtypeStruct((t, LRU_W), F32),
                   jax.ShapeDtypeStruct((t, LRU_W), BF16),
                   jax.ShapeDtypeStruct((t, nw), BF16),
                   jax.ShapeDtypeStruct((t, nw), BF16),
                   jax.ShapeDtypeStruct((t, nw), BF16)],
        compiler_params=_params(("parallel",)),
        name="proj_in",
    )(x, shift, shift, scale, scale, g, w_big, gq, wq2, gkv, wkv2, cos_t, sin_t, ones_row)


def _head_block_mask():
    r = lax.broadcasted_iota(jnp.int32, (NA_W, NA_W), 0) // NA_HEAD_DIM
    c = lax.broadcasted_iota(jnp.int32, (NA_W, NA_W), 1) // NA_HEAD_DIM
    return r == c


def _na_attend(q, parts, mask):
    qbig = jnp.where(mask, jnp.concatenate([q] * NA_HEADS, axis=0), jnp.zeros((), q.dtype))
    scores = []
    for k, _, bias in parts:
        s = _dot_nt(qbig, k)
        if bias is not None:
            s = s + bias
        scores.append(s)
    m = functools.reduce(jnp.maximum, [jnp.max(s, axis=-1, keepdims=True) for s in scores])
    ps = [jnp.exp(s - m) for s in scores]
    l = functools.reduce(jnp.add, [jnp.sum(p, axis=-1, keepdims=True) for p in ps])
    o = functools.reduce(jnp.add, [_dot(p.astype(BF16), v) for p, (_, v, _) in zip(ps, parts)])
    o = jnp.where(mask, o / l, 0.0)
    out = o[0:NA_HEAD_DIM]
    for h in range(1, NA_HEADS):
        out = out + o[h * NA_HEAD_DIM:(h + 1) * NA_HEAD_DIM]
    return out


def _na_body(q_ref, k_ref, v_ref, bias_ref, o_ref, *, rows, rows_per_step, n_lat_steps, seq, ctx):
    i = pl.program_id(1)
    mask = _head_block_mask()
    kc = k_ref[0, seq:seq + ctx, :]
    vc = v_ref[0, seq:seq + ctx, :]

    @pl.when(i < n_lat_steps)
    def _():
        for j in range(rows_per_step):
            r = i * rows_per_step + j
            rs = jnp.clip(r - WIN_ROWS // 2, 0, rows - WIN_ROWS)
            off = rs - r + (WIN_ROWS - 1)
            start = pl.multiple_of(rs * GRID_W, GRID_W)
            kw = k_ref[0, pl.ds(start, WIN_ROWS * GRID_W), :]
            vw = v_ref[0, pl.ds(start, WIN_ROWS * GRID_W), :]
            q = q_ref[0, j * GRID_W:(j + 1) * GRID_W, :]
            out = _na_attend(q, [(kw, vw, bias_ref[off]), (kc, vc, None)], mask)
            o_ref[0, j * GRID_W:(j + 1) * GRID_W, :] = out.astype(o_ref.dtype)

    @pl.when(i >= n_lat_steps)
    def _():
        for j in range(rows_per_step):
            q = q_ref[0, j * GRID_W:(j + 1) * GRID_W, :]
            out = _na_attend(q, [(kc, vc, None)], mask)
            o_ref[0, j * GRID_W:(j + 1) * GRID_W, :] = out.astype(o_ref.dtype)


def _na_attention(qkv, bias, *, seq, ctx):
    b, lt, _ = qkv.shape
    rows = seq // GRID_W
    rows_per_step = TM // GRID_W
    n_steps = lt // TM
    n_lat_steps = seq // TM
    body = functools.partial(_na_body, rows=rows, rows_per_step=rows_per_step, n_lat_steps=n_lat_steps,
                             seq=seq, ctx=ctx)
    return pl.pallas_call(
        body,
        grid=(b, n_steps),
        in_specs=[pl.BlockSpec((1, TM, NA_W), lambda bi, i: (bi, i, 0)),
                  pl.BlockSpec((1, lt, NA_W), lambda bi, i: (bi, 0, 1)),
                  pl.BlockSpec((1, lt, NA_W), lambda bi, i: (bi, 0, 2)),
                  pl.BlockSpec(bias.shape, lambda bi, i: (0, 0, 0))],
        out_specs=pl.BlockSpec((1, TM, NA_W), lambda bi, i: (bi, i, 0)),
        out_shape=jax.ShapeDtypeStruct((b, lt, NA_W), BF16),
        compiler_params=_params(("parallel", "arbitrary")),
        name="na_attention",
    )(qkv, qkv, qkv, bias)


def _na_bias_tables(rpb):
    q = np.arange(GRID_W)
    cs = np.clip(q - WIN_COLS // 2, 0, GRID_W - WIN_COLS)
    c = np.arange(GRID_W)
    inside = (c[None, :] >= cs[:, None]) & (c[None, :] < cs[:, None] + WIN_COLS)
    dc = c[None, :] - q[:, None] + (WIN_COLS - 1)
    place = ((np.arange(2 * WIN_COLS - 1)[:, None, None] == dc[None]) & inside[None]).astype(np.float32)
    win = jnp.stack([rpb[:, o:o + WIN_ROWS, :] for o in range(WIN_ROWS)]).astype(F32)
    vals = jnp.einsum('ohik,kqc->ohqic', win, jnp.asarray(place), precision=lax.Precision.HIGHEST)
    vals = jnp.where(inside[None, None, :, None, :], vals, NEG_BIAS)
    return vals.reshape(WIN_ROWS, NA_HEADS * GRID_W, WIN_ROWS * GRID_W)


def _softplus(v):
    return jnp.maximum(v, 0.0) + jnp.log1p(jnp.exp(-jnp.abs(v)))


def _lru_body(up_ref, uc_ref, un_ref, cw_ref, cb_ref, wa_ref, ba_ref, wi_ref, bi_ref, lam_ref,
              o_ref, a_buf, x_buf, st_ref, *, n_lat, chunk, n_batch):
    d = pl.program_id(0)
    i = pl.program_id(1)
    c = _lru_chunk(d, i, n_lat)
    prev_ok = jnp.logical_and(c >= 1, c < n_lat)
    next_ok = c <= n_lat - 2

    row = lax.broadcasted_iota(jnp.int32, (chunk, 1), 0)
    cw = cw_ref[...]
    log_decay = -LRU_C * _softplus(-lam_ref[0])
    for b in range(n_batch):
        ub = uc_ref[b]
        p_row = jnp.where(prev_ok, up_ref[b, 7:8, :], 0.0)
        n0 = jnp.where(next_ok, un_ref[b, 0:1, :], 0.0)
        n1 = jnp.where(next_ok, un_ref[b, 1:2, :], 0.0)
        um1 = jnp.where(row == 0, p_row, pltpu.roll(ub, 1, 0))
        up1 = jnp.where(row == chunk - 1, n0, pltpu.roll(ub, chunk - 1, 0))
        up2 = jnp.where(row == chunk - 1, n1, jnp.where(row == chunk - 2, n0, pltpu.roll(ub, chunk - 2, 0)))
        cv = um1 * cw[0:1] + cb_ref[...]
        cv = cv + ub * cw[1:2]
        cv = cv + up1 * cw[2:3]
        cv = cv + up2 * cw[3:4]
        cv16 = cv.astype(BF16)
        r = _sigmoid(_dot(cv16, wa_ref[0]) + ba_ref[0])
        gi = _sigmoid(_dot(cv16, wi_ref[0]) + bi_ref[0])
        log_a = r * log_decay
        a = jnp.exp(log_a)
        a_buf[b] = a
        x_buf[b] = jnp.sqrt(1.0 - a * a) * (gi * cv)

    @pl.when(i == 0)
    def _():
        st_ref[...] = jnp.zeros_like(st_ref)

    def scan(reverse):
        n_groups = chunk // SUBLANES

        def group(gi, hs):
            base = pl.multiple_of((n_groups - 1 - gi if reverse else gi) * SUBLANES, SUBLANES)
            hs = list(hs)
            for k in range(SUBLANES):
                r = base + (SUBLANES - 1 - k if reverse else k)
                for b in range(n_batch):
                    hs[b] = a_buf[b, pl.ds(r, 1), :] * hs[b] + x_buf[b, pl.ds(r, 1), :]
                    o_ref[0, b, pl.ds(r, 1), :] = hs[b]
            return tuple(hs)

        hs = lax.fori_loop(0, n_groups, group, tuple(st_ref[b, 0:1, :] for b in range(n_batch)))
        for b in range(n_batch):
            st_ref[b, 0:1, :] = hs[b]

    @pl.when(d == 0)
    def _():
        scan(False)

    @pl.when(d == 1)
    def _():
        scan(True)


def _lru_chunk(d, i, n_lat):
    return jnp.where(i == 0, n_lat, jnp.where(d == 0, i - 1, n_lat - i))


def _lru_scan(u, conv_w, conv_b, w_a, b_a, w_i, b_i, lam, *, seq, ctx):
    b, lt, w = u.shape
    chunk = ctx
    n_lat = seq // chunk
    n_chunks = lt // chunk
    sub = chunk // 8
    n_sub = lt // 8

    cur = lambda d, i: (0, _lru_chunk(d, i, n_lat), 0)
    prev = lambda d, i: (0, jnp.maximum(_lru_chunk(d, i, n_lat) * sub - 1, 0), 0)
    nxt = lambda d, i: (0, jnp.minimum((_lru_chunk(d, i, n_lat) + 1) * sub, n_sub - 1), 0)
    per_dir = lambda a: pl.BlockSpec((1,) + a.shape[1:], lambda d, i: (d,) + (0,) * (a.ndim - 1))
    full = lambda a: pl.BlockSpec(a.shape, lambda d, i: (0,) * a.ndim)
    body = functools.partial(_lru_body, n_lat=n_lat, chunk=chunk, n_batch=b)
    return pl.pallas_call(
        body,
        grid=(2, n_chunks),
        in_specs=[pl.BlockSpec((b, 8, w), prev), pl.BlockSpec((b, chunk, w), cur), pl.BlockSpec((b, 8, w), nxt),
                  full(conv_w), full(conv_b), per_dir(w_a), per_dir(b_a), per_dir(w_i), per_dir(b_i), per_dir(lam)],
        out_specs=pl.BlockSpec((1, b, chunk, w), lambda d, i: (d, 0, _lru_chunk(d, i, n_lat), 0)),
        out_shape=jax.ShapeDtypeStruct((2, b, lt, w), F32),
        scratch_shapes=[pltpu.VMEM((b, chunk, w), F32), pltpu.VMEM((b, chunk, w), F32), pltpu.VMEM((b, 8, w), F32)],
        compiler_params=_params(("arbitrary", "arbitrary")),
        name="lru_scan",
    )(u, u, u, conv_w, conv_b, w_a, b_a, w_i, b_i, lam)


def _mla_body(q_ref, k_ref, v_ref, o_ref, m_ref, acc_ref, *, tk, nk):
    m_ref[...] = jnp.full(m_ref.shape, NEG_BIAS, F32)
    acc_ref[...] = jnp.zeros(acc_ref.shape, F32)

    def chunk(ci, carry):
        start = pl.multiple_of(ci * tk, tk)
        for h in range(MLA_HEADS):
            cols = slice(h * HEAD_PAD, (h + 1) * HEAD_PAD)
            s = _dot_nt(q_ref[0, :, cols], k_ref[0, pl.ds(start, tk), cols])
            blocks = [s[:, j * LANES:(j + 1) * LANES] for j in range(tk // LANES)]
            m_old = m_ref[h]
            m_new = jnp.maximum(m_old, jnp.max(functools.reduce(jnp.maximum, blocks), axis=-1, keepdims=True))
            p = jnp.concatenate([jnp.exp((blk - m_new).astype(BF16)) for blk in blocks], axis=-1)
            acc_ref[h] = jnp.exp(m_old - m_new) * acc_ref[h] + _dot(p, v_ref[0, pl.ds(start, tk), cols])
            m_ref[h] = m_new
        return carry

    lax.fori_loop(0, nk, chunk, 0)
    outs = [acc_ref[h][:, :V_HEAD] / acc_ref[h][:, V_HEAD:V_HEAD + 1] for h in range(MLA_HEADS)]
    o_ref[0] = jnp.concatenate(outs, axis=-1).astype(o_ref.dtype)


def _mla_attention(mq, mk, mv, *, tq, q_block0, n_q, k_block, k_block0, tk):
    b, lt, nw = mq.shape
    body = functools.partial(_mla_body, tk=tk, nk=k_block // tk)
    return pl.pallas_call(
        body,
        grid=(b, n_q),
        in_specs=[pl.BlockSpec((1, tq, nw), lambda bi, i: (bi, q_block0 + i, 0)),
                  pl.BlockSpec((1, k_block, nw), lambda bi, i: (bi, k_block0, 0), pipeline_mode=pl.Buffered(1)),
                  pl.BlockSpec((1, k_block, nw), lambda bi, i: (bi, k_block0, 0), pipeline_mode=pl.Buffered(1))],
        out_specs=pl.BlockSpec((1, tq, MLA_W), lambda bi, i: (bi, i, 0)),
        out_shape=jax.ShapeDtypeStruct((b, n_q * tq, MLA_W), BF16),
        scratch_shapes=[pltpu.VMEM((MLA_HEADS, tq, LANES), F32), pltpu.VMEM((MLA_HEADS, tq, HEAD_PAD), F32)],
        compiler_params=_params(("parallel", "arbitrary")),
        name="mla_attention",
    )(mq, mk, mv)


def _route(logits, rb):
    sel = _sigmoid(logits) + rb
    sc = [sel[j:j + 1, :] for j in range(N_EXPERTS)]
    gscore = []
    for g in range(N_GROUPS):
        a, b, c, d = sc[4 * g:4 * g + 4]
        hi1, lo1 = jnp.maximum(a, b), jnp.minimum(a, b)
        hi2, lo2 = jnp.maximum(c, d), jnp.minimum(c, d)
        gscore.append(jnp.maximum(hi1, hi2) + jnp.maximum(jnp.minimum(hi1, hi2), jnp.maximum(lo1, lo2)))
    best = jnp.zeros_like(gscore[0], dtype=jnp.int32)
    bval = gscore[0]
    for g in range(1, N_GROUPS):
        upd = gscore[g] > bval
        best = jnp.where(upd, g, best)
        bval = jnp.where(upd, gscore[g], bval)

    def pick(cols, j):
        out = cols[j]
        for g in range(1, N_GROUPS):
            out = jnp.where(best == g, cols[4 * g + j], out)
        return out

    v = [pick(sc, j) for j in range(EXPERTS_PER_GROUP)]
    code = jnp.zeros_like(best)
    for i in range(EXPERTS_PER_GROUP):
        rank = jnp.zeros_like(best)
        for j in range(EXPERTS_PER_GROUP):
            if j == i:
                continue
            beats = (v[j] > v[i]) if j > i else (v[j] >= v[i])
            rank = rank + beats.astype(jnp.int32)
        code = code + jnp.where(rank < 2, 1 << i, 0)
    pair = jnp.zeros_like(best)
    for p, (lo, hi) in enumerate(_PAIRS):
        pair = jnp.where(code == (1 << lo) + (1 << hi), p, pair)
    return best * N_PAIRS + pair


def _proj_out_body(ya_ref, hf_ref, hr_ref, gg_ref, yc_ref, x_ref, wa_ref, wb_ref, wc_ref, g1_ref, g2_ref,
                   gate_s_ref, gate_c_ref, shift_s_ref, shift_c_ref, scale_s_ref, scale_c_ref, rwt_ref, rb_ref,
                   xo_ref, h2_ref, cls_ref, *, rt):
    yb = (hf_ref[0, 0] + hr_ref[0, 0]) * gg_ref[...].astype(F32)
    y = _dot(ya_ref[...], wa_ref[...]) + _dot(yb.astype(BF16), wb_ref[...]) + _dot(yc_ref[...], wc_ref[...])
    ga_t, ga_b = rt.mod_rows(gate_s_ref, gate_c_ref)
    sh_t, sh_b = rt.mod_rows(shift_s_ref, shift_c_ref)
    sc_t, sc_b = rt.mod_rows(scale_s_ref, scale_c_ref)
    x = x_ref[...] + rt.by_rows(_rms(y, g1_ref[...]), lambda v: ga_t * v, lambda v: ga_b * v)
    xo_ref[...] = x
    h2 = rt.by_rows(_rms(x, g2_ref[...]), lambda v: v * (1.0 + sc_t) + sh_t,
                    lambda v: v * (1.0 + sc_b) + sh_b).astype(BF16)
    _store_token_tiles(h2_ref, h2.astype(F32))
    cls_ref[0] = _route(_dot_nt(rwt_ref[...], h2), rb_ref[...])


def _proj_out(ya, hscan, gg, yc, x, w_a, w_b, w_c, g1, g2, gate, shift, scale, rw, rb, *, rt):
    t, d = x.shape
    tm = rt.tm
    nt = t // tm
    row = lambda n: pl.BlockSpec((tm, n), lambda i: (i, 0))
    full = lambda a: pl.BlockSpec(a.shape, lambda i: (0,) * a.ndim)
    scan_spec = lambda dd: pl.BlockSpec((1, 1, tm, LRU_W),
                                        lambda i: (dd, i // rt.tiles_per_sample, i % rt.tiles_per_sample, 0))
    return pl.pallas_call(
        functools.partial(_proj_out_body, rt=rt),
        grid=(nt,),
        in_specs=[row(NA_W), scan_spec(0), scan_spec(1), row(LRU_W), row(MLA_W), row(d),
                  full(w_a), full(w_b), full(w_c), full(g1), full(g2),
                  *rt.mod_specs(d), *rt.mod_specs(d), *rt.mod_specs(d), full(rw), full(rb)],
        out_specs=[row(d), pl.BlockSpec((tm * SUBLANES, LANES), lambda i: (i, 0)),
                   pl.BlockSpec((1, 1, tm), lambda i: (i, 0, 0))],
        out_shape=[jax.ShapeDtypeStruct((t, d), F32),
                   jax.ShapeDtypeStruct((t * SUBLANES, LANES), F32),
                   jax.ShapeDtypeStruct((nt, 1, tm), jnp.int32)],
        compiler_params=_params(("parallel",)),
        name="proj_out",
    )(ya, hscan, hscan, gg, yc, x, w_a, w_b, w_c, g1, g2, gate, gate, shift, shift, scale, scale, rw, rb)


def _store_token_tiles(ref, v):
    n = v.shape[0]
    for s in range(SUBLANES):
        ref[pl.ds(s, n, stride=SUBLANES), :] = v[:, s * LANES:(s + 1) * LANES]


def _load_token_tiles(ref, n):
    return jnp.concatenate([ref[pl.ds(s, n, stride=SUBLANES), :] for s in range(SUBLANES)], axis=1)


def _gather_token_tiles(idx_ref, base, src_ref, buf_ref, n):
    def copy(r, carry):
        j = idx_ref[base + r]
        buf_ref[pl.ds(pl.multiple_of(r * SUBLANES, SUBLANES), SUBLANES), :] = (
            src_ref[0, pl.ds(pl.multiple_of(j * SUBLANES, SUBLANES), SUBLANES), :])
        return carry

    lax.fori_loop(0, n, copy, 0, unroll=8)


def _dispatch_body(pos_ref, h_ref, o_ref, buf_ref, src_ref, *, tokens_per_sample):
    @pl.when(pl.program_id(1) == 0)
    def _():
        base = pl.program_id(0) * tokens_per_sample

        def invert(t, carry):
            src_ref[pos_ref[base + t]] = t
            return carry

        lax.fori_loop(0, tokens_per_sample, invert, 0, unroll=8)

    tm = o_ref.shape[0]
    _gather_token_tiles(src_ref, pl.program_id(1) * tm, h_ref, buf_ref, tm)
    o_ref[...] = _load_token_tiles(buf_ref, tm).astype(o_ref.dtype)


def _dispatch(pos, h2t, *, n_batch, tokens_per_sample, tm):
    n_tiles = tokens_per_sample // tm
    return pl.pallas_call(
        functools.partial(_dispatch_body, tokens_per_sample=tokens_per_sample),
        grid_spec=pltpu.PrefetchScalarGridSpec(
            num_scalar_prefetch=1,
            grid=(n_batch, n_tiles),
            in_specs=[pl.BlockSpec((1, tokens_per_sample * SUBLANES, LANES), lambda b, i, pos: (b, 0, 0),
                                   pipeline_mode=pl.Buffered(1))],
            out_specs=pl.BlockSpec((tm, D_MODEL), lambda b, i, pos: (b * n_tiles + i, 0)),
            scratch_shapes=[pltpu.VMEM((tm * SUBLANES, LANES), F32), pltpu.SMEM((tokens_per_sample,), jnp.int32)]),
        out_shape=jax.ShapeDtypeStruct((n_batch * tokens_per_sample, D_MODEL), BF16),
        compiler_params=_params(("arbitrary", "arbitrary")),
        name="moe_dispatch",
    )(pos, h2t.reshape(n_batch, tokens_per_sample * SUBLANES, LANES))


def _moe_body(tile_ref, elo_ref, ehi_ref, lo_ref, hi_ref, valid_ref, first_ref,
              x_ref, rw_ref, wg0_ref, wu0_ref, wd0_ref, wg1_ref, wu1_ref, wd1_ref, o_ref):
    k = pl.program_id(0)

    @pl.when(valid_ref[k] == 1)
    def _():
        x = x_ref[...]
        aff = _sigmoid(_dot(x, rw_ref[...]))
        lane = lax.broadcasted_iota(jnp.int32, aff.shape, 1)
        a_lo = jnp.sum(jnp.where(lane == elo_ref[k], aff, 0.0), axis=-1, keepdims=True)
        a_hi = jnp.sum(jnp.where(lane == ehi_ref[k], aff, 0.0), axis=-1, keepdims=True)
        den = a_lo + a_hi

        def expert(wg, wu, wd):
            a = _dot(x, wg[0])
            he = (a * _sigmoid(a)) * _dot(x, wu[0])
            return _dot(he.astype(BF16), wd[0])

        y = (a_lo / den) * expert(wg0_ref, wu0_ref, wd0_ref) + (a_hi / den) * expert(wg1_ref, wu1_ref, wd1_ref)
        row = lax.broadcasted_iota(jnp.int32, (TM, 1), 0)
        y = jnp.where(jnp.logical_and(row >= lo_ref[k], row < hi_ref[k]), y, 0.0)

        @pl.when(first_ref[k] == 1)
        def _():
            _store_token_tiles(o_ref, y)

        @pl.when(first_ref[k] == 0)
        def _():
            _store_token_tiles(o_ref, _load_token_tiles(o_ref, TM) + y)


def _moe_routed(items, xs, rw, w_gate, w_up, w_down, *, expert_base):
    t, d = xs.shape
    _, _, de = w_gate.shape
    n_items = items[0].shape[0]
    xspec = pl.BlockSpec((TM, d), lambda k, tile, *_: (tile[k], 0))
    w_in_spec = lambda which: pl.BlockSpec(
        (1, d, de), lambda k, tile, elo, ehi, *_: (expert_base + (elo, ehi)[which][k], 0, 0))
    w_out_spec = lambda which: pl.BlockSpec(
        (1, de, d), lambda k, tile, elo, ehi, *_: (expert_base + (elo, ehi)[which][k], 0, 0))
    return pl.pallas_call(
        _moe_body,
        grid_spec=pltpu.PrefetchScalarGridSpec(
            num_scalar_prefetch=len(items),
            grid=(n_items,),
            in_specs=[xspec, pl.BlockSpec(rw.shape, lambda k, *_: (0, 0)),
                      w_in_spec(0), w_in_spec(0), w_out_spec(0), w_in_spec(1), w_in_spec(1), w_out_spec(1)],
            out_specs=pl.BlockSpec((TM * SUBLANES, LANES), lambda k, tile, *_: (tile[k], 0))),
        out_shape=jax.ShapeDtypeStruct((t * SUBLANES, LANES), F32),
        compiler_params=_params(("arbitrary",)),
        name="moe_routed",
    )(*items, xs, rw, w_gate, w_up, w_down, w_gate, w_up, w_down)


def _combine_body(pos_ref, y_ref, x_ref, gate_ref, g_ref, o_ref, buf_ref, *, tokens_per_sample):
    base = pl.program_id(0) * tokens_per_sample + pl.program_id(1) * TM
    _gather_token_tiles(pos_ref, base, y_ref, buf_ref, TM)
    o_ref[...] = x_ref[...] + gate_ref[0] * _rms(_load_token_tiles(buf_ref, TM), g_ref[...])


def _combine(pos, ys, x, gate, g, *, n_batch, tokens_per_sample, n_lat_tiles, n_out_tiles):
    n_tiles = tokens_per_sample // TM
    d = x.shape[1]
    return pl.pallas_call(
        functools.partial(_combine_body, tokens_per_sample=tokens_per_sample),
        grid_spec=pltpu.PrefetchScalarGridSpec(
            num_scalar_prefetch=1,
            grid=(n_batch, n_out_tiles),
            in_specs=[pl.BlockSpec((1, tokens_per_sample * SUBLANES, LANES), lambda b, i, pos: (b, 0, 0),
                                   pipeline_mode=pl.Buffered(1)),
                      pl.BlockSpec((TM, d), lambda b, i, pos: (b * n_tiles + i, 0)),
                      pl.BlockSpec((1, 1, d), lambda b, i, pos: (jnp.where(i >= n_lat_tiles, n_batch, b), 0, 0)),
                      pl.BlockSpec(g.shape, lambda b, i, pos: (0, 0))],
            out_specs=pl.BlockSpec((TM, d), lambda b, i, pos: (b * n_out_tiles + i, 0)),
            scratch_shapes=[pltpu.VMEM((TM * SUBLANES, LANES), F32)]),
        out_shape=jax.ShapeDtypeStruct((n_batch * n_out_tiles * TM, d), F32),
        compiler_params=_params(("arbitrary", "arbitrary")),
        name="moe_combine",
    )(pos, ys.reshape(n_batch, tokens_per_sample * SUBLANES, LANES), x, gate, g)


def _routing_plan(cls, *, n_tiles):
    b, l = cls.shape
    n_items = n_tiles + N_CLASSES - 1
    onehot = (cls[..., None] == jnp.arange(N_CLASSES, dtype=jnp.int32)).astype(jnp.int32)
    csum = jnp.cumsum(onehot, axis=1)
    counts = csum[:, -1]
    starts = jnp.cumsum(counts, axis=1) - counts
    pos = jnp.sum(onehot * (starts[:, None, :] + csum - 1), axis=-1)
    bi = jnp.arange(b, dtype=jnp.int32)[:, None]

    first_tile = starts // TM
    last_tile = (starts + counts - 1) // TM
    m = jnp.where(counts > 0, last_tile - first_tile + 1, 0)
    o_end = jnp.cumsum(m, axis=1)
    o_start = o_end - m
    total = o_end[:, -1:]
    k = jnp.arange(n_items, dtype=jnp.int32)[None, :]
    kk = jnp.minimum(k, total - 1)
    c_k = jnp.sum((kk[:, :, None] >= o_end[:, None, :]).astype(jnp.int32), axis=-1)
    take = lambda a: jnp.take_along_axis(a, c_k, axis=1)
    tile_k = take(first_tile) + kk - take(o_start)
    valid = (k < total).astype(jnp.int32)
    lo = jnp.maximum(take(starts), tile_k * TM) - tile_k * TM
    hi = jnp.minimum(take(starts + counts), (tile_k + 1) * TM) - tile_k * TM
    prev_tile = jnp.concatenate([jnp.full((b, 1), -1, jnp.int32), tile_k[:, :-1]], axis=1)
    first = (tile_k != prev_tile).astype(jnp.int32)
    pair_lo = jnp.asarray([p[0] for p in _PAIRS], jnp.int32)
    pair_hi = jnp.asarray([p[1] for p in _PAIRS], jnp.int32)
    e_lo = (c_k // N_PAIRS) * EXPERTS_PER_GROUP + pair_lo[c_k % N_PAIRS]
    e_hi = (c_k // N_PAIRS) * EXPERTS_PER_GROUP + pair_hi[c_k % N_PAIRS]
    tile_g = tile_k + bi * n_tiles
    items = tuple(a.reshape(-1).astype(jnp.int32) for a in (tile_g, e_lo, e_hi, lo, hi, valid, first))
    return pos.reshape(-1).astype(jnp.int32), items


def _rope_tables(seq, ctx):
    t = np.arange(seq)
    axis_dim = QK_ROPE // 2
    inv = np.float32(ROPE_BASE) ** (-np.arange(0, axis_dim, 2, dtype=np.float32) / np.float32(axis_dim))
    ang_r = (t // GRID_W).astype(np.float32)[:, None] * inv
    ang_c = (t % GRID_W).astype(np.float32)[:, None] * inv
    cos = jnp.asarray(np.concatenate([np.cos(ang_r), np.cos(ang_r), np.cos(ang_c), np.cos(ang_c)], axis=-1))
    sin = jnp.asarray(np.concatenate([-np.sin(ang_r), np.sin(ang_r), -np.sin(ang_c), np.sin(ang_c)], axis=-1))
    pad = HEAD_PAD - QK_NOPE - QK_ROPE
    cos = jnp.concatenate([jnp.ones((seq, QK_NOPE), F32), cos, jnp.ones((seq, pad), F32)], axis=-1)
    sin = jnp.concatenate([jnp.zeros((seq, QK_NOPE), F32), sin, jnp.zeros((seq, pad), F32)], axis=-1)
    cos = jnp.concatenate([cos, jnp.ones((ctx, HEAD_PAD), F32)], axis=0)
    sin = jnp.concatenate([sin, jnp.zeros((ctx, HEAD_PAD), F32)], axis=0)
    return cos, sin


_ROPE_SWAP = np.concatenate([np.arange(8, 16), np.arange(0, 8), np.arange(24, 32), np.arange(16, 24)])


def _layout_w_in(w_in):
    d = w_in.shape[0]
    z = lambda n: jnp.zeros((d, n), w_in.dtype)
    kr = w_in[:, _C_KR:_C_KR + QK_ROPE]
    pad = HEAD_PAD - QK_NOPE - QK_ROPE
    return jnp.concatenate([w_in[:, :_C_KR], z(QK_NOPE), kr, z(pad), z(QK_NOPE), kr[:, _ROPE_SWAP], z(pad)],
                           axis=1).astype(BF16)


def _layout_w_q(w_q):
    r = w_q.shape[0]
    wh = w_q.reshape(r, MLA_HEADS, QK_NOPE + QK_ROPE)
    pad = HEAD_PAD - QK_NOPE - QK_ROPE
    full = jnp.concatenate([wh, jnp.zeros((r, MLA_HEADS, pad), w_q.dtype)], axis=-1)
    swap = jnp.concatenate([jnp.zeros((r, MLA_HEADS, QK_NOPE), w_q.dtype), wh[:, :, QK_NOPE + _ROPE_SWAP],
                            jnp.zeros((r, MLA_HEADS, pad), w_q.dtype)], axis=-1)
    return jnp.concatenate([full.reshape(r, -1), swap.reshape(r, -1)], axis=1).astype(BF16)


def _layout_w_kv(w_kv):
    r = w_kv.shape[0]
    wh = w_kv.reshape(r, MLA_HEADS, QK_NOPE + V_HEAD)
    zk = jnp.zeros((r, MLA_HEADS, HEAD_PAD - QK_NOPE), w_kv.dtype)
    zv = jnp.zeros((r, MLA_HEADS, HEAD_PAD - V_HEAD), w_kv.dtype)
    k = jnp.concatenate([wh[:, :, :QK_NOPE], zk], axis=-1).reshape(r, -1)
    v = jnp.concatenate([wh[:, :, QK_NOPE:], zv], axis=-1).reshape(r, -1)
    return jnp.concatenate([k, v], axis=1).astype(BF16)


def _block_diag(w):
    nd, nb, k, _ = w.shape
    eye = jnp.eye(nb, dtype=w.dtype)
    return (w[:, :, :, None, :] * eye[None, :, None, :, None]).reshape(nd, nb * k, nb * k).astype(BF16)


def _pick_chunk(n, candidates):
    for c in candidates:
        if n % c == 0:
            return c
    raise ValueError(f"no chunk size for {n}")


def kernel(x, c, ctx, c_ctx, w_ada, b_ada, g_norm, w_in, w_out, na_rpb, conv_w, conv_b, lru_w_a, lru_b_a, lru_w_i,
           lru_b_i, lru_lam, mla_g_q, mla_w_q, mla_g_kv, mla_w_kv, router_w, router_b, exp_w_gate, exp_w_up,
           exp_w_down):
    b, seq, d = x.shape
    n_ctx = ctx.shape[1]
    depth = w_ada.shape[0]
    lt = seq + n_ctx
    assert d == D_MODEL == SUBLANES * LANES and n_ctx == TM and seq % TM == 0 and seq // GRID_W >= WIN_ROWS
    assert b + 1 <= 8
    tiles_per_sample = lt // TM
    n_lat_tiles = seq // TM
    rt = _RowTiling(seq, lt, b, _pick_chunk(lt, (PROJ_TM, TM)))

    xa = jnp.concatenate([x, ctx], axis=1).reshape(b * lt, d)

    c_rows = jnp.concatenate([c, c_ctx[None, :], jnp.zeros((8 - b - 1, d), F32)], axis=0)
    mod = _ada_tables(c_rows, w_ada, b_ada).reshape(depth, 8, N_MOD, 1, d)

    cos_t, sin_t = _rope_tables(seq, n_ctx)
    ones_row = jnp.zeros((MLA_HEADS, HEAD_PAD), F32).at[:, V_HEAD].set(1.0).reshape(1, -1)
    rw = jnp.concatenate([router_w, jnp.zeros((d, LANES - N_EXPERTS), F32)], axis=1).astype(BF16)
    wg_all, wu_all, wd_all = (w.astype(BF16).reshape((depth * N_EXPERTS,) + w.shape[2:])
                              for w in (exp_w_gate, exp_w_up, exp_w_down))
    rwt = router_w.T.astype(BF16)
    rb = router_b[:, None]
    tk = _pick_chunk(lt, (768, 512, 256))
    tq = _pick_chunk(seq, (2048, 1024, 512, 256))

    for l in range(depth):
        last = l == depth - 1
        m = lambda k: mod[l, :b + 1, k]
        g = g_norm[l]
        qkv, u, gg, mq, mk, mv = _proj_in(
            xa, m(0), m(1), g[0:1], _layout_w_in(w_in[l]), mla_g_q[l][None, :], _layout_w_q(mla_w_q[l]),
            mla_g_kv[l][None, :], _layout_w_kv(mla_w_kv[l]), cos_t, sin_t, ones_row, rt=rt)

        ya = _na_attention(qkv.reshape(b, lt, -1), _na_bias_tables(na_rpb[l]), seq=seq, ctx=n_ctx)
        hscan = _lru_scan(u.reshape(b, lt, -1), conv_w[l], conv_b[l][None, :], _block_diag(lru_w_a[l]),
                          lru_b_a[l][:, None, :], _block_diag(lru_w_i[l]), lru_b_i[l][:, None, :],
                          lru_lam[l][:, None, :], seq=seq, ctx=n_ctx)
        mq3, mk3, mv3 = (a.reshape(b, lt, -1) for a in (mq, mk, mv))
        yc = _mla_attention(mq3, mk3, mv3, tq=tq, q_block0=0, n_q=seq // tq, k_block=lt, k_block0=0, tk=tk)
        if last:
            yc = jnp.concatenate([yc, jnp.zeros((b, n_ctx, MLA_W), BF16)], axis=1)
        else:
            yc_ctx = _mla_attention(mq3, mk3, mv3, tq=n_ctx, q_block0=seq // n_ctx, n_q=1, k_block=n_ctx,
                                    k_block0=seq // n_ctx, tk=n_ctx)
            yc = jnp.concatenate([yc, yc_ctx], axis=1)

        wo = w_out[l].astype(BF16)
        xa, h2t, cls = _proj_out(
            ya.reshape(b * lt, -1), hscan, gg, yc.reshape(b * lt, -1), xa,
            wo[:NA_W], wo[NA_W:NA_W + LRU_W], wo[NA_W + LRU_W:], g[1:2], g[2:3], m(2), m(3), m(4), rwt, rb, rt=rt)

        pos, items = _routing_plan(cls.reshape(b, lt), n_tiles=tiles_per_sample)
        xs = _dispatch(pos, h2t, n_batch=b, tokens_per_sample=lt, tm=rt.tm)
        ys = _moe_routed(items, xs, rw, wg_all, wu_all, wd_all, expert_base=l * N_EXPERTS)
        xa = _combine(pos, ys, xa, m(5), g[3:4], n_batch=b, tokens_per_sample=lt, n_lat_tiles=n_lat_tiles,
                      n_out_tiles=n_lat_tiles if last else tiles_per_sample)

    return xa.reshape(b, seq, d)
```

```python
import functools

import numpy as np
import jax
import jax.numpy as jnp
from jax import lax
from jax.experimental import pallas as pl
from jax.experimental.pallas import tpu as pltpu

F32 = jnp.float32
BF16 = jnp.bfloat16

D_MODEL = 1024
GRID_W = 64
NA_HEADS = 4
NA_HEAD_DIM = 64
NA_W = NA_HEADS * NA_HEAD_DIM
WIN_ROWS = 8
WIN_COLS = 16
LRU_W = 512
LRU_BLOCKS = 8
CONV_W = 4
LRU_C = 8.0
MLA_HEADS = 4
Q_LORA = 256
KV_LORA = 128
QK_NOPE = 64
QK_ROPE = 32
V_HEAD = 64
MLA_W = MLA_HEADS * V_HEAD
ROPE_BASE = 10000.0
N_EXPERTS = 16
N_GROUPS = 4
EXPERTS_PER_GROUP = 4
D_EXPERT = 512
RMS_EPS = 1e-6
N_MOD = 6

_PAIRS = ((0, 1), (0, 2), (0, 3), (1, 2), (1, 3), (2, 3))
N_PAIRS = len(_PAIRS)
N_CLASSES = N_GROUPS * N_PAIRS

LANES = 128
SUBLANES = 8
TM = 256
PROJ_TM = 768
HEAD_PAD = 128
NEG_BIAS = -1e30
VMEM_LIMIT = 56 * 1024 * 1024

_C_QA, _C_KA, _C_VA = 0, 256, 512
_C_U = 768
_C_GATE = 1280
_C_CQ = 1792
_C_CKV = 2048
_C_KR = 2176
_C_KRP = 2304
_IN_COLS = 2432


def _params(sem, vmem=VMEM_LIMIT):
    return pltpu.CompilerParams(dimension_semantics=sem, vmem_limit_bytes=vmem)


def _sigmoid(v):
    return 1.0 / (1.0 + jnp.exp(-v))


def _rms(v, g):
    return v * lax.rsqrt(jnp.mean(v * v, axis=-1, keepdims=True) + RMS_EPS) * g


def _dot(a, b):
    return jnp.dot(a, b, preferred_element_type=F32)


def _dot_nt(a, b):
    return lax.dot_general(a, b, (((1,), (1,)), ((), ())), preferred_element_type=F32)


def _ada_body(c_ref, w_ref, b_ref, o_ref):
    c = c_ref[...]
    sc = c * _sigmoid(c)
    o_ref[0] = _dot(sc.astype(BF16), w_ref[0].astype(BF16)) + b_ref[0]


def _ada_tables(c_rows, w_ada, b_ada):
    depth, d, n = w_ada.shape
    tn = 1536
    return pl.pallas_call(
        _ada_body,
        grid=(depth, n // tn),
        in_specs=[pl.BlockSpec((8, d), lambda l, j: (0, 0)),
                  pl.BlockSpec((1, d, tn), lambda l, j: (l, 0, j)),
                  pl.BlockSpec((1, 1, tn), lambda l, j: (l, 0, j))],
        out_specs=pl.BlockSpec((1, 8, tn), lambda l, j: (l, 0, j)),
        out_shape=jax.ShapeDtypeStruct((depth, 8, n), F32),
        compiler_params=_params(("parallel", "parallel")),
        name="ada_tables",
    )(c_rows, w_ada, b_ada.reshape(depth, 1, n))


class _RowTiling:
    def __init__(self, seq, tokens_per_sample, n_batch, tm):
        assert tokens_per_sample % tm == 0
        self.tm = tm
        self.n_batch = n_batch
        self.tiles_per_sample = tokens_per_sample // tm
        self.ctx_tile = seq // tm
        self.split = seq % tm

    def mod_specs(self, d):
        return [pl.BlockSpec((1, 1, d), lambda i: (i // self.tiles_per_sample, 0, 0)),
                pl.BlockSpec((1, 1, d), lambda i: (self.n_batch, 0, 0))]

    def mod_rows(self, sample_ref, ctx_ref):
        j = pl.program_id(0) % self.tiles_per_sample
        top = jnp.where(j <= self.ctx_tile, sample_ref[0], ctx_ref[0])
        bot = jnp.where(j < self.ctx_tile, sample_ref[0], ctx_ref[0])
        return top, bot

    def by_rows(self, v, f_top, f_bot):
        if self.split == 0:
            return f_bot(v)
        return jnp.concatenate([f_top(v[:self.split]), f_bot(v[self.split:])], axis=0)


def _proj_in_body(x_ref, shift_s_ref, shift_c_ref, scale_s_ref, scale_c_ref, g_ref, w_ref, gq_ref, wq_ref,
                  gkv_ref, wkv_ref, cos_ref, sin_ref, ones_ref,
                  qkv_ref, u_ref, gg_ref, mq_ref, mk_ref, mv_ref, *, mla_scale, rt):
    x = x_ref[...]
    sh_t, sh_b = rt.mod_rows(shift_s_ref, shift_c_ref)
    sc_t, sc_b = rt.mod_rows(scale_s_ref, scale_c_ref)
    h = rt.by_rows(_rms(x, g_ref[...]), lambda v: v * (1.0 + sc_t) + sh_t, lambda v: v * (1.0 + sc_b) + sh_b)
    z = _dot(h.astype(BF16), w_ref[...])
    qkv_ref[:, 0:NA_W] = (z[:, _C_QA:_C_KA] * (NA_HEAD_DIM ** -0.5)).astype(BF16)
    qkv_ref[:, NA_W:3 * NA_W] = z[:, _C_KA:_C_U].astype(BF16)
    u_ref[...] = z[:, _C_U:_C_GATE]
    gg_ref[...] = jax.nn.gelu(z[:, _C_GATE:_C_CQ]).astype(BF16)

    cos = cos_ref[...]
    sin = sin_ref[...]
    cos4 = jnp.concatenate([cos] * MLA_HEADS, axis=-1)
    sin4 = jnp.concatenate([sin] * MLA_HEADS, axis=-1)
    nw = MLA_HEADS * HEAD_PAD

    nq = _rms(z[:, _C_CQ:_C_CKV], gq_ref[...])
    q2 = _dot(nq.astype(BF16), wq_ref[...])
    mq_ref[...] = ((q2[:, :nw] * cos4 + q2[:, nw:] * sin4) * mla_scale).astype(BF16)

    nkv = _rms(z[:, _C_CKV:_C_KR], gkv_ref[...])
    kv2 = _dot(nkv.astype(BF16), wkv_ref[...])
    k_rope = z[:, _C_KR:_C_KRP] * cos + z[:, _C_KRP:_IN_COLS] * sin
    mk_ref[...] = (kv2[:, :nw] + jnp.concatenate([k_rope] * MLA_HEADS, axis=-1)).astype(BF16)
    mv_ref[...] = (kv2[:, nw:] + ones_ref[...]).astype(BF16)


def _proj_in(x, shift, scale, g, w_big, gq, wq2, gkv, wkv2, cos_t, sin_t, ones_row, *, rt):
    t, d = x.shape
    tm = rt.tm
    nt = t // tm
    nw = MLA_HEADS * HEAD_PAD
    row = lambda n: pl.BlockSpec((tm, n), lambda i: (i, 0))
    full = lambda a: pl.BlockSpec(a.shape, lambda i: (0,) * a.ndim)
    tabspec = pl.BlockSpec((tm, HEAD_PAD), lambda i: (i % rt.tiles_per_sample, 0))
    mla_scale = (QK_NOPE + QK_ROPE) ** -0.5
    return pl.pallas_call(
        functools.partial(_proj_in_body, mla_scale=mla_scale, rt=rt),
        grid=(nt,),
        in_specs=[row(d), *rt.mod_specs(d), *rt.mod_specs(d), full(g), full(w_big), full(gq), full(wq2), full(gkv),
                  full(wkv2), tabspec, tabspec, full(ones_row)],
        out_specs=[row(3 * NA_W), row(LRU_W), row(LRU_W), row(nw), row(nw), row(nw)],
        out_shape=[jax.ShapeDtypeStruct((t, 3 * NA_W), BF16),
                   jax.ShapeDtypeStruct((t, LRU_W), F32),
                   jax.ShapeDtypeStruct((t, LRU_W), BF16),
                   jax.ShapeDtypeStruct((t, nw), BF16),
                   jax.ShapeDtypeStruct((t, nw), BF16),
                   jax.ShapeDtypeStruct((t, nw), BF16)],
        compiler_params=_params(("parallel",)),
        name="proj_in",
    )(x, shift, shift, scale, scale, g, w_big, gq, wq2, gkv, wkv2, cos_t, sin_t, ones_row)


def _head_block_mask():
    r = lax.broadcasted_iota(jnp.int32, (NA_W, NA_W), 0) // NA_HEAD_DIM
    c = lax.broadcasted_iota(jnp.int32, (NA_W, NA_W), 1) // NA_HEAD_DIM
    return r == c


def _na_attend(q, parts, mask):
    qbig = jnp.where(mask, jnp.concatenate([q] * NA_HEADS, axis=0), jnp.zeros((), q.dtype))
    scores = []
    for k, _, bias in parts:
        s = _dot_nt(qbig, k)
        if bias is not None:
            s = s + bias
        scores.append(s)
    m = functools.reduce(jnp.maximum, [jnp.max(s, axis=-1, keepdims=True) for s in scores])
    ps = [jnp.exp(s - m) for s in scores]
    l = functools.reduce(jnp.add, [jnp.sum(p, axis=-1, keepdims=True) for p in ps])
    o = functools.reduce(jnp.add, [_dot(p.astype(BF16), v) for p, (_, v, _) in zip(ps, parts)])
    o = jnp.where(mask, o / l, 0.0)
    out = o[0:NA_HEAD_DIM]
    for h in range(1, NA_HEADS):
        out = out + o[h * NA_HEAD_DIM:(h + 1) * NA_HEAD_DIM]
    return out


def _na_body(q_ref, k_ref, v_ref, bias_ref, o_ref, *, rows, rows_per_step, n_lat_steps, seq, ctx):
    i = pl.program_id(1)
    mask = _head_block_mask()
    kc = k_ref[0, seq:seq + ctx, :]
    vc = v_ref[0, seq:seq + ctx, :]

    @pl.when(i < n_lat_steps)
    def _():
        for j in range(rows_per_step):
            r = i * rows_per_step + j
            rs = jnp.clip(r - WIN_ROWS // 2, 0, rows - WIN_ROWS)
            off = rs - r + (WIN_ROWS - 1)
            start = pl.multiple_of(rs * GRID_W, GRID_W)
            kw = k_ref[0, pl.ds(start, WIN_ROWS * GRID_W), :]
            vw = v_ref[0, pl.ds(start, WIN_ROWS * GRID_W), :]
            q = q_ref[0, j * GRID_W:(j + 1) * GRID_W, :]
            out = _na_attend(q, [(kw, vw, bias_ref[off]), (kc, vc, None)], mask)
            o_ref[0, j * GRID_W:(j + 1) * GRID_W, :] = out.astype(o_ref.dtype)

    @pl.when(i >= n_lat_steps)
    def _():
        for j in range(rows_per_step):
            q = q_ref[0, j * GRID_W:(j + 1) * GRID_W, :]
            out = _na_attend(q, [(kc, vc, None)], mask)
            o_ref[0, j * GRID_W:(j + 1) * GRID_W, :] = out.astype(o_ref.dtype)


def _na_attention(qkv, bias, *, seq, ctx):
    b, lt, _ = qkv.shape
    rows = seq // GRID_W
    rows_per_step = TM // GRID_W
    n_steps = lt // TM
    n_lat_steps = seq // TM
    body = functools.partial(_na_body, rows=rows, rows_per_step=rows_per_step, n_lat_steps=n_lat_steps,
                             seq=seq, ctx=ctx)
    return pl.pallas_call(
        body,
        grid=(b, n_steps),
        in_specs=[pl.BlockSpec((1, TM, NA_W), lambda bi, i: (bi, i, 0)),
                  pl.BlockSpec((1, lt, NA_W), lambda bi, i: (bi, 0, 1)),
                  pl.BlockSpec((1, lt, NA_W), lambda bi, i: (bi, 0, 2)),
                  pl.BlockSpec(bias.shape, lambda bi, i: (0, 0, 0))],
        out_specs=pl.BlockSpec((1, TM, NA_W), lambda bi, i: (bi, i, 0)),
        out_shape=jax.ShapeDtypeStruct((b, lt, NA_W), BF16),
        compiler_params=_params(("parallel", "arbitrary")),
        name="na_attention",
    )(qkv, qkv, qkv, bias)


def _na_bias_tables(rpb):
    q = np.arange(GRID_W)
    cs = np.clip(q - WIN_COLS // 2, 0, GRID_W - WIN_COLS)
    c = np.arange(GRID_W)
    inside = (c[None, :] >= cs[:, None]) & (c[None, :] < cs[:, None] + WIN_COLS)
    dc = c[None, :] - q[:, None] + (WIN_COLS - 1)
    place = ((np.arange(2 * WIN_COLS - 1)[:, None, None] == dc[None]) & inside[None]).astype(np.float32)
    win = jnp.stack([rpb[:, o:o + WIN_ROWS, :] for o in range(WIN_ROWS)]).astype(F32)
    vals = jnp.einsum('ohik,kqc->ohqic', win, jnp.asarray(place), precision=lax.Precision.HIGHEST)
    vals = jnp.where(inside[None, None, :, None, :], vals, NEG_BIAS)
    return vals.reshape(WIN_ROWS, NA_HEADS * GRID_W, WIN_ROWS * GRID_W)


def _softplus(v):
    return jnp.maximum(v, 0.0) + jnp.log1p(jnp.exp(-jnp.abs(v)))


def _lru_body(up_ref, uc_ref, un_ref, cw_ref, cb_ref, wa_ref, ba_ref, wi_ref, bi_ref, lam_ref,
              o_ref, a_buf, x_buf, st_ref, *, n_lat, chunk, n_batch):
    d = pl.program_id(0)
    i = pl.program_id(1)
    c = _lru_chunk(d, i, n_lat)
    prev_ok = jnp.logical_and(c >= 1, c < n_lat)
    next_ok = c <= n_lat - 2

    row = lax.broadcasted_iota(jnp.int32, (chunk, 1), 0)
    cw = cw_ref[...]
    log_decay = -LRU_C * _softplus(-lam_ref[0])
    for b in range(n_batch):
        ub = uc_ref[b]
        p_row = jnp.where(prev_ok, up_ref[b, 7:8, :], 0.0)
        n0 = jnp.where(next_ok, un_ref[b, 0:1, :], 0.0)
        n1 = jnp.where(next_ok, un_ref[b, 1:2, :], 0.0)
        um1 = jnp.where(row == 0, p_row, pltpu.roll(ub, 1, 0))
        up1 = jnp.where(row == chunk - 1, n0, pltpu.roll(ub, chunk - 1, 0))
        up2 = jnp.where(row == chunk - 1, n1, jnp.where(row == chunk - 2, n0, pltpu.roll(ub, chunk - 2, 0)))
        cv = um1 * cw[0:1] + cb_ref[...]
        cv = cv + ub * cw[1:2]
        cv = cv + up1 * cw[2:3]
        cv = cv + up2 * cw[3:4]
        cv16 = cv.astype(BF16)
        r = _sigmoid(_dot(cv16, wa_ref[0]) + ba_ref[0])
        gi = _sigmoid(_dot(cv16, wi_ref[0]) + bi_ref[0])
        log_a = r * log_decay
        a = jnp.exp(log_a)
        a_buf[b] = a
        x_buf[b] = jnp.sqrt(1.0 - a * a) * (gi * cv)

    @pl.when(i == 0)
    def _():
        st_ref[...] = jnp.zeros_like(st_ref)

    def scan(reverse):
        n_groups = chunk // SUBLANES

        def group(gi, hs):
            base = pl.multiple_of((n_groups - 1 - gi if reverse else gi) * SUBLANES, SUBLANES)
            hs = list(hs)
            for k in range(SUBLANES):
                r = base + (SUBLANES - 1 - k if reverse else k)
                for b in range(n_batch):
                    hs[b] = a_buf[b, pl.ds(r, 1), :] * hs[b] + x_buf[b, pl.ds(r, 1), :]
                    o_ref[0, b, pl.ds(r, 1), :] = hs[b]
            return tuple(hs)

        hs = lax.fori_loop(0, n_groups, group, tuple(st_ref[b, 0:1, :] for b in range(n_batch)))
        for b in range(n_batch):
            st_ref[b, 0:1, :] = hs[b]

    @pl.when(d == 0)
    def _():
        scan(False)

    @pl.when(d == 1)
    def _():
        scan(True)


def _lru_chunk(d, i, n_lat):
    return jnp.where(i == 0, n_lat, jnp.where(d == 0, i - 1, n_lat - i))


def _lru_scan(u, conv_w, conv_b, w_a, b_a, w_i, b_i, lam, *, seq, ctx):
    b, lt, w = u.shape
    chunk = ctx
    n_lat = seq // chunk
    n_chunks = lt // chunk
    sub = chunk // 8
    n_sub = lt // 8

    cur = lambda d, i: (0, _lru_chunk(d, i, n_lat), 0)
    prev = lambda d, i: (0, jnp.maximum(_lru_chunk(d, i, n_lat) * sub - 1, 0), 0)
    nxt = lambda d, i: (0, jnp.minimum((_lru_chunk(d, i, n_lat) + 1) * sub, n_sub - 1), 0)
    per_dir = lambda a: pl.BlockSpec((1,) + a.shape[1:], lambda d, i: (d,) + (0,) * (a.ndim - 1))
    full = lambda a: pl.BlockSpec(a.shape, lambda d, i: (0,) * a.ndim)
    body = functools.partial(_lru_body, n_lat=n_lat, chunk=chunk, n_batch=b)
    return pl.pallas_call(
        body,
        grid=(2, n_chunks),
        in_specs=[pl.BlockSpec((b, 8, w), prev), pl.BlockSpec((b, chunk, w), cur), pl.BlockSpec((b, 8, w), nxt),
                  full(conv_w), full(conv_b), per_dir(w_a), per_dir(b_a), per_dir(w_i), per_dir(b_i), per_dir(lam)],
        out_specs=pl.BlockSpec((1, b, chunk, w), lambda d, i: (d, 0, _lru_chunk(d, i, n_lat), 0)),
        out_shape=jax.ShapeDtypeStruct((2, b, lt, w), F32),
        scratch_shapes=[pltpu.VMEM((b, chunk, w), F32), pltpu.VMEM((b, chunk, w), F32), pltpu.VMEM((b, 8, w), F32)],
        compiler_params=_params(("arbitrary", "arbitrary")),
        name="lru_scan",
    )(u, u, u, conv_w, conv_b, w_a, b_a, w_i, b_i, lam)


def _mla_body(q_ref, k_ref, v_ref, o_ref, m_ref, acc_ref, *, tk, nk):
    m_ref[...] = jnp.full(m_ref.shape, NEG_BIAS, F32)
    acc_ref[...] = jnp.zeros(acc_ref.shape, F32)

    def chunk(ci, carry):
        start = pl.multiple_of(ci * tk, tk)
        for h in range(MLA_HEADS):
            cols = slice(h * HEAD_PAD, (h + 1) * HEAD_PAD)
            s = _dot_nt(q_ref[0, :, cols], k_ref[0, pl.ds(start, tk), cols])
            blocks = [s[:, j * LANES:(j + 1) * LANES] for j in range(tk // LANES)]
            m_old = m_ref[h]
            m_new = jnp.maximum(m_old, jnp.max(functools.reduce(jnp.maximum, blocks), axis=-1, keepdims=True))
            p = jnp.concatenate([jnp.exp((blk - m_new).astype(BF16)) for blk in blocks], axis=-1)
            acc_ref[h] = jnp.exp(m_old - m_new) * acc_ref[h] + _dot(p, v_ref[0, pl.ds(start, tk), cols])
            m_ref[h] = m_new
        return carry

    lax.fori_loop(0, nk, chunk, 0)
    outs = [acc_ref[h][:, :V_HEAD] / acc_ref[h][:, V_HEAD:V_HEAD + 1] for h in range(MLA_HEADS)]
    o_ref[0] = jnp.concatenate(outs, axis=-1).astype(o_ref.dtype)


def _mla_attention(mq, mk, mv, *, tq, q_block0, n_q, k_block, k_block0, tk):
    b, lt, nw = mq.shape
    body = functools.partial(_mla_body, tk=tk, nk=k_block // tk)
    return pl.pallas_call(
        body,
        grid=(b, n_q),
        in_specs=[pl.BlockSpec((1, tq, nw), lambda bi, i: (bi, q_block0 + i, 0)),
                  pl.BlockSpec((1, k_block, nw), lambda bi, i: (bi, k_block0, 0), pipeline_mode=pl.Buffered(1)),
                  pl.BlockSpec((1, k_block, nw), lambda bi, i: (bi, k_block0, 0), pipeline_mode=pl.Buffered(1))],
        out_specs=pl.BlockSpec((1, tq, MLA_W), lambda bi, i: (bi, i, 0)),
        out_shape=jax.ShapeDtypeStruct((b, n_q * tq, MLA_W), BF16),
        scratch_shapes=[pltpu.VMEM((MLA_HEADS, tq, LANES), F32), pltpu.VMEM((MLA_HEADS, tq, HEAD_PAD), F32)],
        compiler_params=_params(("parallel", "arbitrary")),
        name="mla_attention",
    )(mq, mk, mv)


def _route(logits, rb):
    sel = _sigmoid(logits) + rb
    sc = [sel[j:j + 1, :] for j in range(N_EXPERTS)]
    gscore = []
    for g in range(N_GROUPS):
        a, b, c, d = sc[4 * g:4 * g + 4]
        hi1, lo1 = jnp.maximum(a, b), jnp.minimum(a, b)
        hi2, lo2 = jnp.maximum(c, d), jnp.minimum(c, d)
        gscore.append(jnp.maximum(hi1, hi2) + jnp.maximum(jnp.minimum(hi1, hi2), jnp.maximum(lo1, lo2)))
    best = jnp.zeros_like(gscore[0], dtype=jnp.int32)
    bval = gscore[0]
    for g in range(1, N_GROUPS):
        upd = gscore[g] > bval
        best = jnp.where(upd, g, best)
        bval = jnp.where(upd, gscore[g], bval)

    def pick(cols, j):
        out = cols[j]
        for g in range(1, N_GROUPS):
            out = jnp.where(best == g, cols[4 * g + j], out)
        return out

    v = [pick(sc, j) for j in range(EXPERTS_PER_GROUP)]
    code = jnp.zeros_like(best)
    for i in range(EXPERTS_PER_GROUP):
        rank = jnp.zeros_like(best)
        for j in range(EXPERTS_PER_GROUP):
            if j == i:
                continue
            beats = (v[j] > v[i]) if j > i else (v[j] >= v[i])
            rank = rank + beats.astype(jnp.int32)
        code = code + jnp.where(rank < 2, 1 << i, 0)
    pair = jnp.zeros_like(best)
    for p, (lo, hi) in enumerate(_PAIRS):
        pair = jnp.where(code == (1 << lo) + (1 << hi), p, pair)
    return best * N_PAIRS + pair


def _proj_out_body(ya_ref, hf_ref, hr_ref, gg_ref, yc_ref, x_ref, wa_ref, wb_ref, wc_ref, g1_ref, g2_ref,
                   gate_s_ref, gate_c_ref, shift_s_ref, shift_c_ref, scale_s_ref, scale_c_ref, rwt_ref, rb_ref,
                   xo_ref, h2_ref, cls_ref, *, rt):
    yb = (hf_ref[0, 0] + hr_ref[0, 0]) * gg_ref[...].astype(F32)
    y = _dot(ya_ref[...], wa_ref[...]) + _dot(yb.astype(BF16), wb_ref[...]) + _dot(yc_ref[...], wc_ref[...])
    ga_t, ga_b = rt.mod_rows(gate_s_ref, gate_c_ref)
    sh_t, sh_b = rt.mod_rows(shift_s_ref, shift_c_ref)
    sc_t, sc_b = rt.mod_rows(scale_s_ref, scale_c_ref)
    x = x_ref[...] + rt.by_rows(_rms(y, g1_ref[...]), lambda v: ga_t * v, lambda v: ga_b * v)
    xo_ref[...] = x
    h2 = rt.by_rows(_rms(x, g2_ref[...]), lambda v: v * (1.0 + sc_t) + sh_t,
                    lambda v: v * (1.0 + sc_b) + sh_b).astype(BF16)
    _store_token_tiles(h2_ref, h2.astype(F32))
    cls_ref[0] = _route(_dot_nt(rwt_ref[...], h2), rb_ref[...])


def _proj_out(ya, hscan, gg, yc, x, w_a, w_b, w_c, g1, g2, gate, shift, scale, rw, rb, *, rt):
    t, d = x.shape
    tm = rt.tm
    nt = t // tm
    row = lambda n: pl.BlockSpec((tm, n), lambda i: (i, 0))
    full = lambda a: pl.BlockSpec(a.shape, lambda i: (0,) * a.ndim)
    scan_spec = lambda dd: pl.BlockSpec((1, 1, tm, LRU_W),
                                        lambda i: (dd, i // rt.tiles_per_sample, i % rt.tiles_per_sample, 0))
    return pl.pallas_call(
        functools.partial(_proj_out_body, rt=rt),
        grid=(nt,),
        in_specs=[row(NA_W), scan_spec(0), scan_spec(1), row(LRU_W), row(MLA_W), row(d),
                  full(w_a), full(w_b), full(w_c), full(g1), full(g2),
                  *rt.mod_specs(d), *rt.mod_specs(d), *rt.mod_specs(d), full(rw), full(rb)],
        out_specs=[row(d), pl.BlockSpec((tm * SUBLANES, LANES), lambda i: (i, 0)),
                   pl.BlockSpec((1, 1, tm), lambda i: (i, 0, 0))],
        out_shape=[jax.ShapeDtypeStruct((t, d), F32),
                   jax.ShapeDtypeStruct((t * SUBLANES, LANES), F32),
                   jax.ShapeDtypeStruct((nt, 1, tm), jnp.int32)],
        compiler_params=_params(("parallel",)),
        name="proj_out",
    )(ya, hscan, hscan, gg, yc, x, w_a, w_b, w_c, g1, g2, gate, gate, shift, shift, scale, scale, rw, rb)


def _store_token_tiles(ref, v, token0=0):
    n = v.shape[0]
    for s in range(SUBLANES):
        ref[pl.ds(token0 * SUBLANES + s, n, stride=SUBLANES), :] = v[:, s * LANES:(s + 1) * LANES]


def _load_token_tiles(ref, n, token0=0):
    return jnp.concatenate([ref[pl.ds(token0 * SUBLANES + s, n, stride=SUBLANES), :] for s in range(SUBLANES)],
                           axis=1)


def _gather_token_tiles(idx_ref, base, src_ref, buf_ref, n):
    def copy(r, carry):
        j = idx_ref[base + r]
        buf_ref[pl.ds(pl.multiple_of(r * SUBLANES, SUBLANES), SUBLANES), :] = (
            src_ref[0, pl.ds(pl.multiple_of(j * SUBLANES, SUBLANES), SUBLANES), :])
        return carry

    lax.fori_loop(0, n, copy, 0, unroll=8)


def _dispatch_body(pos_ref, h_ref, o_ref, buf_ref, src_ref, *, tokens_per_sample):
    @pl.when(pl.program_id(1) == 0)
    def _():
        base = pl.program_id(0) * tokens_per_sample

        def invert(t, carry):
            src_ref[pos_ref[base + t]] = t
            return carry

        lax.fori_loop(0, tokens_per_sample, invert, 0, unroll=8)

    tm = o_ref.shape[0]
    _gather_token_tiles(src_ref, pl.program_id(1) * tm, h_ref, buf_ref, tm)
    o_ref[...] = _load_token_tiles(buf_ref, tm).astype(o_ref.dtype)


def _dispatch(pos, h2t, *, n_batch, tokens_per_sample, tm):
    n_tiles = tokens_per_sample // tm
    return pl.pallas_call(
        functools.partial(_dispatch_body, tokens_per_sample=tokens_per_sample),
        grid_spec=pltpu.PrefetchScalarGridSpec(
            num_scalar_prefetch=1,
            grid=(n_batch, n_tiles),
            in_specs=[pl.BlockSpec((1, tokens_per_sample * SUBLANES, LANES), lambda b, i, pos: (b, 0, 0),
                                   pipeline_mode=pl.Buffered(1))],
            out_specs=pl.BlockSpec((tm, D_MODEL), lambda b, i, pos: (b * n_tiles + i, 0)),
            scratch_shapes=[pltpu.VMEM((tm * SUBLANES, LANES), F32), pltpu.SMEM((tokens_per_sample,), jnp.int32)]),
        out_shape=jax.ShapeDtypeStruct((n_batch * tokens_per_sample, D_MODEL), BF16),
        compiler_params=_params(("arbitrary", "arbitrary")),
        name="moe_dispatch",
    )(pos, h2t.reshape(n_batch, tokens_per_sample * SUBLANES, LANES))


def _moe_body(tile_ref, elo_ref, ehi_ref, lo_ref, hi_ref, valid_ref, first_ref,
              x_ref, rw_ref, wg0_ref, wu0_ref, wd0_ref, wg1_ref, wu1_ref, wd1_ref, o_ref):
    k = pl.program_id(0)

    @pl.when(valid_ref[k] == 1)
    def _():
        @pl.when(first_ref[k] == 1)
        def _():
            o_ref[...] = jnp.zeros(o_ref.shape, o_ref.dtype)

        def visit(row0, n):
            x = x_ref[row0:row0 + n, :]
            aff = _sigmoid(_dot(x, rw_ref[...]))
            lane = lax.broadcasted_iota(jnp.int32, aff.shape, 1)
            a_lo = jnp.sum(jnp.where(lane == elo_ref[k], aff, 0.0), axis=-1, keepdims=True)
            a_hi = jnp.sum(jnp.where(lane == ehi_ref[k], aff, 0.0), axis=-1, keepdims=True)
            den = a_lo + a_hi
            row = row0 + lax.broadcasted_iota(jnp.int32, (n, 1), 0)
            mine = jnp.logical_and(row >= lo_ref[k], row < hi_ref[k])

            def add_expert(a_e, wg, wu, wd):
                a = _dot(x, wg[0])
                he = (a * _sigmoid(a)) * _dot(x, wu[0])
                y = jnp.where(mine, a_e / den, 0.0) * _dot(he.astype(BF16), wd[0])
                _store_token_tiles(o_ref, _load_token_tiles(o_ref, n, row0) + y, row0)

            add_expert(a_lo, wg0_ref, wu0_ref, wd0_ref)
            add_expert(a_hi, wg1_ref, wu1_ref, wd1_ref)

        half = TM // 2
        in_top = hi_ref[k] <= half
        in_bottom = lo_ref[k] >= half

        @pl.when(in_top)
        def _():
            visit(0, half)

        @pl.when(in_bottom)
        def _():
            visit(half, half)

        @pl.when(jnp.logical_not(jnp.logical_or(in_top, in_bottom)))
        def _():
            visit(0, TM)


def _moe_routed(items, xs, rw, w_gate, w_up, w_down, *, expert_base):
    t, d = xs.shape
    _, _, de = w_gate.shape
    n_items = items[0].shape[0]
    xspec = pl.BlockSpec((TM, d), lambda k, tile, *_: (tile[k], 0))
    w_in_spec = lambda which: pl.BlockSpec(
        (1, d, de), lambda k, tile, elo, ehi, *_: (expert_base + (elo, ehi)[which][k], 0, 0))
    w_out_spec = lambda which: pl.BlockSpec(
        (1, de, d), lambda k, tile, elo, ehi, *_: (expert_base + (elo, ehi)[which][k], 0, 0))
    return pl.pallas_call(
        _moe_body,
        grid_spec=pltpu.PrefetchScalarGridSpec(
            num_scalar_prefetch=len(items),
            grid=(n_items,),
            in_specs=[xspec, pl.BlockSpec(rw.shape, lambda k, *_: (0, 0)),
                      w_in_spec(0), w_in_spec(0), w_out_spec(0), w_in_spec(1), w_in_spec(1), w_out_spec(1)],
            out_specs=pl.BlockSpec((TM * SUBLANES, LANES), lambda k, tile, *_: (tile[k], 0))),
        out_shape=jax.ShapeDtypeStruct((t * SUBLANES, LANES), F32),
        compiler_params=_params(("arbitrary",)),
        name="moe_routed",
    )(*items, xs, rw, w_gate, w_up, w_down, w_gate, w_up, w_down)


def _combine_body(pos_ref, y_ref, x_ref, gate_ref, g_ref, o_ref, buf_ref, *, tokens_per_sample):
    base = pl.program_id(0) * tokens_per_sample + pl.program_id(1) * TM
    _gather_token_tiles(pos_ref, base, y_ref, buf_ref, TM)
    o_ref[...] = x_ref[...] + gate_ref[0] * _rms(_load_token_tiles(buf_ref, TM), g_ref[...])


def _combine(pos, ys, x, gate, g, *, n_batch, tokens_per_sample, n_lat_tiles, n_out_tiles):
    n_tiles = tokens_per_sample // TM
    d = x.shape[1]
    return pl.pallas_call(
        functools.partial(_combine_body, tokens_per_sample=tokens_per_sample),
        grid_spec=pltpu.PrefetchScalarGridSpec(
            num_scalar_prefetch=1,
            grid=(n_batch, n_out_tiles),
            in_specs=[pl.BlockSpec((1, tokens_per_sample * SUBLANES, LANES), lambda b, i, pos: (b, 0, 0),
                                   pipeline_mode=pl.Buffered(1)),
                      pl.BlockSpec((TM, d), lambda b, i, pos: (b * n_tiles + i, 0)),
                      pl.BlockSpec((1, 1, d), lambda b, i, pos: (jnp.where(i >= n_lat_tiles, n_batch, b), 0, 0)),
                      pl.BlockSpec(g.shape, lambda b, i, pos: (0, 0))],
            out_specs=pl.BlockSpec((TM, d), lambda b, i, pos: (b * n_out_tiles + i, 0)),
            scratch_shapes=[pltpu.VMEM((TM * SUBLANES, LANES), F32)]),
        out_shape=jax.ShapeDtypeStruct((n_batch * n_out_tiles * TM, d), F32),
        compiler_params=_params(("arbitrary", "arbitrary")),
        name="moe_combine",
    )(pos, ys.reshape(n_batch, tokens_per_sample * SUBLANES, LANES), x, gate, g)


def _routing_plan(cls, *, n_tiles):
    b, l = cls.shape
    n_items = n_tiles + N_CLASSES - 1
    onehot = (cls[..., None] == jnp.arange(N_CLASSES, dtype=jnp.int32)).astype(jnp.int32)
    csum = jnp.cumsum(onehot, axis=1)
    counts = csum[:, -1]
    starts = jnp.cumsum(counts, axis=1) - counts
    pos = jnp.sum(onehot * (starts[:, None, :] + csum - 1), axis=-1)
    bi = jnp.arange(b, dtype=jnp.int32)[:, None]

    first_tile = starts // TM
    last_tile = (starts + counts - 1) // TM
    m = jnp.where(counts > 0, last_tile - first_tile + 1, 0)
    o_end = jnp.cumsum(m, axis=1)
    o_start = o_end - m
    total = o_end[:, -1:]
    k = jnp.arange(n_items, dtype=jnp.int32)[None, :]
    kk = jnp.minimum(k, total - 1)
    c_k = jnp.sum((kk[:, :, None] >= o_end[:, None, :]).astype(jnp.int32), axis=-1)
    take = lambda a: jnp.take_along_axis(a, c_k, axis=1)
    tile_k = take(first_tile) + kk - take(o_start)
    valid = (k < total).astype(jnp.int32)
    lo = jnp.maximum(take(starts), tile_k * TM) - tile_k * TM
    hi = jnp.minimum(take(starts + counts), (tile_k + 1) * TM) - tile_k * TM
    prev_tile = jnp.concatenate([jnp.full((b, 1), -1, jnp.int32), tile_k[:, :-1]], axis=1)
    first = (tile_k != prev_tile).astype(jnp.int32)
    pair_lo = jnp.asarray([p[0] for p in _PAIRS], jnp.int32)
    pair_hi = jnp.asarray([p[1] for p in _PAIRS], jnp.int32)
    e_lo = (c_k // N_PAIRS) * EXPERTS_PER_GROUP + pair_lo[c_k % N_PAIRS]
    e_hi = (c_k // N_PAIRS) * EXPERTS_PER_GROUP + pair_hi[c_k % N_PAIRS]
    tile_g = tile_k + bi * n_tiles
    items = tuple(a.reshape(-1).astype(jnp.int32) for a in (tile_g, e_lo, e_hi, lo, hi, valid, first))
    return pos.reshape(-1).astype(jnp.int32), items


def _rope_tables(seq, ctx):
    t = np.arange(seq)
    axis_dim = QK_ROPE // 2
    inv = np.float32(ROPE_BASE) ** (-np.arange(0, axis_dim, 2, dtype=np.float32) / np.float32(axis_dim))
    ang_r = (t // GRID_W).astype(np.float32)[:, None] * inv
    ang_c = (t % GRID_W).astype(np.float32)[:, None] * inv
    cos = jnp.asarray(np.concatenate([np.cos(ang_r), np.cos(ang_r), np.cos(ang_c), np.cos(ang_c)], axis=-1))
    sin = jnp.asarray(np.concatenate([-np.sin(ang_r), np.sin(ang_r), -np.sin(ang_c), np.sin(ang_c)], axis=-1))
    pad = HEAD_PAD - QK_NOPE - QK_ROPE
    cos = jnp.concatenate([jnp.ones((seq, QK_NOPE), F32), cos, jnp.ones((seq, pad), F32)], axis=-1)
    sin = jnp.concatenate([jnp.zeros((seq, QK_NOPE), F32), sin, jnp.zeros((seq, pad), F32)], axis=-1)
    cos = jnp.concatenate([cos, jnp.ones((ctx, HEAD_PAD), F32)], axis=0)
    sin = jnp.concatenate([sin, jnp.zeros((ctx, HEAD_PAD), F32)], axis=0)
    return cos, sin


_ROPE_SWAP = np.concatenate([np.arange(8, 16), np.arange(0, 8), np.arange(24, 32), np.arange(16, 24)])


def _layout_w_in(w_in):
    d = w_in.shape[0]
    z = lambda n: jnp.zeros((d, n), w_in.dtype)
    kr = w_in[:, _C_KR:_C_KR + QK_ROPE]
    pad = HEAD_PAD - QK_NOPE - QK_ROPE
    return jnp.concatenate([w_in[:, :_C_KR], z(QK_NOPE), kr, z(pad), z(QK_NOPE), kr[:, _ROPE_SWAP], z(pad)],
                           axis=1).astype(BF16)


def _layout_w_q(w_q):
    r = w_q.shape[0]
    wh = w_q.reshape(r, MLA_HEADS, QK_NOPE + QK_ROPE)
    pad = HEAD_PAD - QK_NOPE - QK_ROPE
    full = jnp.concatenate([wh, jnp.zeros((r, MLA_HEADS, pad), w_q.dtype)], axis=-1)
    swap = jnp.concatenate([jnp.zeros((r, MLA_HEADS, QK_NOPE), w_q.dtype), wh[:, :, QK_NOPE + _ROPE_SWAP],
                            jnp.zeros((r, MLA_HEADS, pad), w_q.dtype)], axis=-1)
    return jnp.concatenate([full.reshape(r, -1), swap.reshape(r, -1)], axis=1).astype(BF16)


def _layout_w_kv(w_kv):
    r = w_kv.shape[0]
    wh = w_kv.reshape(r, MLA_HEADS, QK_NOPE + V_HEAD)
    zk = jnp.zeros((r, MLA_HEADS, HEAD_PAD - QK_NOPE), w_kv.dtype)
    zv = jnp.zeros((r, MLA_HEADS, HEAD_PAD - V_HEAD), w_kv.dtype)
    k = jnp.concatenate([wh[:, :, :QK_NOPE], zk], axis=-1).reshape(r, -1)
    v = jnp.concatenate([wh[:, :, QK_NOPE:], zv], axis=-1).reshape(r, -1)
    return jnp.concatenate([k, v], axis=1).astype(BF16)


def _block_diag(w):
    nd, nb, k, _ = w.shape
    eye = jnp.eye(nb, dtype=w.dtype)
    return (w[:, :, :, None, :] * eye[None, :, None, :, None]).reshape(nd, nb * k, nb * k).astype(BF16)


def _pick_chunk(n, candidates):
    for c in candidates:
        if n % c == 0:
            return c
    raise ValueError(f"no chunk size for {n}")


def kernel(x, c, ctx, c_ctx, w_ada, b_ada, g_norm, w_in, w_out, na_rpb, conv_w, conv_b, lru_w_a, lru_b_a, lru_w_i,
           lru_b_i, lru_lam, mla_g_q, mla_w_q, mla_g_kv, mla_w_kv, router_w, router_b, exp_w_gate, exp_w_up,
           exp_w_down):
    b, seq, d = x.shape
    n_ctx = ctx.shape[1]
    depth = w_ada.shape[0]
    lt = seq + n_ctx
    assert d == D_MODEL == SUBLANES * LANES and n_ctx == TM and seq % TM == 0 and seq // GRID_W >= WIN_ROWS
    assert b + 1 <= 8
    tiles_per_sample = lt // TM
    n_lat_tiles = seq // TM
    rt = _RowTiling(seq, lt, b, _pick_chunk(lt, (PROJ_TM, TM)))

    xa = jnp.concatenate([x, ctx], axis=1).reshape(b * lt, d)

    c_rows = jnp.concatenate([c, c_ctx[None, :], jnp.zeros((8 - b - 1, d), F32)], axis=0)
    mod = _ada_tables(c_rows, w_ada, b_ada).reshape(depth, 8, N_MOD, 1, d)

    cos_t, sin_t = _rope_tables(seq, n_ctx)
    ones_row = jnp.zeros((MLA_HEADS, HEAD_PAD), F32).at[:, V_HEAD].set(1.0).reshape(1, -1)
    rw = jnp.concatenate([router_w, jnp.zeros((d, LANES - N_EXPERTS), F32)], axis=1).astype(BF16)
    wg_all, wu_all, wd_all = (w.astype(BF16).reshape((depth * N_EXPERTS,) + w.shape[2:])
                              for w in (exp_w_gate, exp_w_up, exp_w_down))
    rwt = router_w.T.astype(BF16)
    rb = router_b[:, None]
    tk = _pick_chunk(lt, (768, 512, 256))
    tq = _pick_chunk(seq, (2048, 1024, 512, 256))

    for l in range(depth):
        last = l == depth - 1
        m = lambda k: mod[l, :b + 1, k]
        g = g_norm[l]
        qkv, u, gg, mq, mk, mv = _proj_in(
            xa, m(0), m(1), g[0:1], _layout_w_in(w_in[l]), mla_g_q[l][None, :], _layout_w_q(mla_w_q[l]),
            mla_g_kv[l][None, :], _layout_w_kv(mla_w_kv[l]), cos_t, sin_t, ones_row, rt=rt)

        ya = _na_attention(qkv.reshape(b, lt, -1), _na_bias_tables(na_rpb[l]), seq=seq, ctx=n_ctx)
        hscan = _lru_scan(u.reshape(b, lt, -1), conv_w[l], conv_b[l][None, :], _block_diag(lru_w_a[l]),
                          lru_b_a[l][:, None, :], _block_diag(lru_w_i[l]), lru_b_i[l][:, None, :],
                          lru_lam[l][:, None, :], seq=seq, ctx=n_ctx)
        mq3, mk3, mv3 = (a.reshape(b, lt, -1) for a in (mq, mk, mv))
        yc = _mla_attention(mq3, mk3, mv3, tq=tq, q_block0=0, n_q=seq // tq, k_block=lt, k_block0=0, tk=tk)
        if last:
            yc = jnp.concatenate([yc, jnp.zeros((b, n_ctx, MLA_W), BF16)], axis=1)
        else:
            yc_ctx = _mla_attention(mq3, mk3, mv3, tq=n_ctx, q_block0=seq // n_ctx, n_q=1, k_block=n_ctx,
                                    k_block0=seq // n_ctx, tk=n_ctx)
            yc = jnp.concatenate([yc, yc_ctx], axis=1)

        wo = w_out[l].astype(BF16)
        xa, h2t, cls = _proj_out(
            ya.reshape(b * lt, -1), hscan, gg, yc.reshape(b * lt, -1), xa,
            wo[:NA_W], wo[NA_W:NA_W + LRU_W], wo[NA_W + LRU_W:], g[1:2], g[2:3], m(2), m(3), m(4), rwt, rb, rt=rt)

        pos, items = _routing_plan(cls.reshape(b, lt), n_tiles=tiles_per_sample)
        xs = _dispatch(pos, h2t, n_batch=b, tokens_per_sample=lt, tm=rt.tm)
        ys = _moe_routed(items, xs, rw, wg_all, wu_all, wd_all, expert_base=l * N_EXPERTS)
        xa = _combine(pos, ys, xa, m(5), g[3:4], n_batch=b, tokens_per_sample=lt, n_lat_tiles=n_lat_tiles,
                      n_out_tiles=n_lat_tiles if last else tiles_per_sample)

    return xa.reshape(b, seq, d)
```

```python
import functools

import numpy as np
import jax
import jax.numpy as jnp
from jax import lax
from jax.experimental import pallas as pl
from jax.experimental.pallas import tpu as pltpu

F32 = jnp.float32
BF16 = jnp.bfloat16

D_MODEL = 1024
GRID_W = 64
NA_HEADS = 4
NA_HEAD_DIM = 64
NA_W = NA_HEADS * NA_HEAD_DIM
WIN_ROWS = 8
WIN_COLS = 16
LRU_W = 512
LRU_BLOCKS = 8
CONV_W = 4
LRU_C = 8.0
MLA_HEADS = 4
Q_LORA = 256
KV_LORA = 128
QK_NOPE = 64
QK_ROPE = 32
V_HEAD = 64
MLA_W = MLA_HEADS * V_HEAD
ROPE_BASE = 10000.0
N_EXPERTS = 16
N_GROUPS = 4
EXPERTS_PER_GROUP = 4
D_EXPERT = 512
RMS_EPS = 1e-6
N_MOD = 6

_PAIRS = ((0, 1), (0, 2), (0, 3), (1, 2), (1, 3), (2, 3))
N_PAIRS = len(_PAIRS)
N_CLASSES = N_GROUPS * N_PAIRS

LANES = 128
SUBLANES = 8
TM = 256
PROJ_TM = 768
HEAD_PAD = 128
NEG_BIAS = -1e30
VMEM_LIMIT = 56 * 1024 * 1024

_C_QA, _C_KA, _C_VA = 0, 256, 512
_C_U = 768
_C_GATE = 1280
_C_CQ = 1792
_C_CKV = 2048
_C_KR = 2176
_C_KRP = 2304
_IN_COLS = 2432


def _params(sem, vmem=VMEM_LIMIT):
    return pltpu.CompilerParams(dimension_semantics=sem, vmem_limit_bytes=vmem)


def _sigmoid(v):
    return 1.0 / (1.0 + jnp.exp(-v))


def _rms(v, g):
    return v * lax.rsqrt(jnp.mean(v * v, axis=-1, keepdims=True) + RMS_EPS) * g


def _dot(a, b):
    return jnp.dot(a, b, preferred_element_type=F32)


def _dot_nt(a, b):
    return lax.dot_general(a, b, (((1,), (1,)), ((), ())), preferred_element_type=F32)


def _ada_body(c_ref, w_ref, b_ref, o_ref):
    c = c_ref[...]
    sc = c * _sigmoid(c)
    o_ref[0] = _dot(sc.astype(BF16), w_ref[0].astype(BF16)) + b_ref[0]


def _ada_tables(c_rows, w_ada, b_ada):
    depth, d, n = w_ada.shape
    tn = 1536
    return pl.pallas_call(
        _ada_body,
        grid=(depth, n // tn),
        in_specs=[pl.BlockSpec((8, d), lambda l, j: (0, 0)),
                  pl.BlockSpec((1, d, tn), lambda l, j: (l, 0, j)),
                  pl.BlockSpec((1, 1, tn), lambda l, j: (l, 0, j))],
        out_specs=pl.BlockSpec((1, 8, tn), lambda l, j: (l, 0, j)),
        out_shape=jax.ShapeDtypeStruct((depth, 8, n), F32),
        compiler_params=_params(("parallel", "parallel")),
        name="ada_tables",
    )(c_rows, w_ada, b_ada.reshape(depth, 1, n))


class _RowTiling:
    def __init__(self, seq, tokens_per_sample, n_batch, tm):
        assert tokens_per_sample % tm == 0
        self.tm = tm
        self.n_batch = n_batch
        self.tiles_per_sample = tokens_per_sample // tm
        self.ctx_tile = seq // tm
        self.split = seq % tm

    def mod_specs(self, d):
        return [pl.BlockSpec((1, 1, d), lambda i: (i // self.tiles_per_sample, 0, 0)),
                pl.BlockSpec((1, 1, d), lambda i: (self.n_batch, 0, 0))]

    def mod_rows(self, sample_ref, ctx_ref):
        j = pl.program_id(0) % self.tiles_per_sample
        top = jnp.where(j <= self.ctx_tile, sample_ref[0], ctx_ref[0])
        bot = jnp.where(j < self.ctx_tile, sample_ref[0], ctx_ref[0])
        return top, bot

    def by_rows(self, v, f_top, f_bot):
        if self.split == 0:
            return f_bot(v)
        return jnp.concatenate([f_top(v[:self.split]), f_bot(v[self.split:])], axis=0)


def _proj_in_body(x_ref, shift_s_ref, shift_c_ref, scale_s_ref, scale_c_ref, g_ref, w_ref, gq_ref, wq_ref,
                  gkv_ref, wkv_ref, cos_ref, sin_ref, ones_ref,
                  qkv_ref, u_ref, gg_ref, mq_ref, mk_ref, mv_ref, *, mla_scale, rt):
    x = x_ref[...]
    sh_t, sh_b = rt.mod_rows(shift_s_ref, shift_c_ref)
    sc_t, sc_b = rt.mod_rows(scale_s_ref, scale_c_ref)
    h = rt.by_rows(_rms(x, g_ref[...]), lambda v: v * (1.0 + sc_t) + sh_t, lambda v: v * (1.0 + sc_b) + sh_b)
    z = _dot(h.astype(BF16), w_ref[...])
    qkv_ref[:, 0:NA_W] = (z[:, _C_QA:_C_KA] * (NA_HEAD_DIM ** -0.5)).astype(BF16)
    qkv_ref[:, NA_W:3 * NA_W] = z[:, _C_KA:_C_U].astype(BF16)
    u_ref[...] = z[:, _C_U:_C_GATE]
    gg_ref[...] = jax.nn.gelu(z[:, _C_GATE:_C_CQ]).astype(BF16)

    cos = cos_ref[...]
    sin = sin_ref[...]
    cos4 = jnp.concatenate([cos] * MLA_HEADS, axis=-1)
    sin4 = jnp.concatenate([sin] * MLA_HEADS, axis=-1)
    nw = MLA_HEADS * HEAD_PAD

    nq = _rms(z[:, _C_CQ:_C_CKV], gq_ref[...])
    q2 = _dot(nq.astype(BF16), wq_ref[...])
    mq_ref[...] = ((q2[:, :nw] * cos4 + q2[:, nw:] * sin4) * mla_scale).astype(BF16)

    nkv = _rms(z[:, _C_CKV:_C_KR], gkv_ref[...])
    kv2 = _dot(nkv.astype(BF16), wkv_ref[...])
    k_rope = z[:, _C_KR:_C_KRP] * cos + z[:, _C_KRP:_IN_COLS] * sin
    mk_ref[...] = (kv2[:, :nw] + jnp.concatenate([k_rope] * MLA_HEADS, axis=-1)).astype(BF16)
    mv_ref[...] = (kv2[:, nw:] + ones_ref[...]).astype(BF16)


def _proj_in(x, shift, scale, g, w_big, gq, wq2, gkv, wkv2, cos_t, sin_t, ones_row, *, rt):
    t, d = x.shape
    tm = rt.tm
    nt = t // tm
    nw = MLA_HEADS * HEAD_PAD
    row = lambda n: pl.BlockSpec((tm, n), lambda i: (i, 0))
    full = lambda a: pl.BlockSpec(a.shape, lambda i: (0,) * a.ndim)
    tabspec = pl.BlockSpec((tm, HEAD_PAD), lambda i: (i % rt.tiles_per_sample, 0))
    mla_scale = (QK_NOPE + QK_ROPE) ** -0.5
    return pl.pallas_call(
        functools.partial(_proj_in_body, mla_scale=mla_scale, rt=rt),
        grid=(nt,),
        in_specs=[row(d), *rt.mod_specs(d), *rt.mod_specs(d), full(g), full(w_big), full(gq), full(wq2), full(gkv),
                  full(wkv2), tabspec, tabspec, full(ones_row)],
        out_specs=[row(3 * NA_W), row(LRU_W), row(LRU_W), row(nw), row(nw), row(nw)],
        out_shape=[jax.ShapeDtypeStruct((t, 3 * NA_W), BF16),
                   jax.ShapeDtypeStruct((t, LRU_W), F32),
                   jax.ShapeDtypeStruct((t, LRU_W), BF16),
                   jax.ShapeDtypeStruct((t, nw), BF16),
                   jax.ShapeDtypeStruct((t, nw), BF16),
                   jax.ShapeDtypeStruct((t, nw), BF16)],
        compiler_params=_params(("parallel",)),
        name="proj_in",
    )(x, shift, shift, scale, scale, g, w_big, gq, wq2, gkv, wkv2, cos_t, sin_t, ones_row)


def _head_block_mask():
    r = lax.broadcasted_iota(jnp.int32, (NA_W, NA_W), 0) // NA_HEAD_DIM
    c = lax.broadcasted_iota(jnp.int32, (NA_W, NA_W), 1) // NA_HEAD_DIM
    return r == c


def _na_attend(q, parts, mask):
    qbig = jnp.where(mask, jnp.concatenate([q] * NA_HEADS, axis=0), jnp.zeros((), q.dtype))
    scores = []
    for k, _, bias in parts:
        s = _dot_nt(qbig, k)
        if bias is not None:
            s = s + bias
        scores.append(s)
    m = functools.reduce(jnp.maximum, [jnp.max(s, axis=-1, keepdims=True) for s in scores])
    ps = [jnp.exp(s - m) for s in scores]
    l = functools.reduce(jnp.add, [jnp.sum(p, axis=-1, keepdims=True) for p in ps])
    o = functools.reduce(jnp.add, [_dot(p.astype(BF16), v) for p, (_, v, _) in zip(ps, parts)])
    o = jnp.where(mask, o / l, 0.0)
    out = o[0:NA_HEAD_DIM]
    for h in range(1, NA_HEADS):
        out = out + o[h * NA_HEAD_DIM:(h + 1) * NA_HEAD_DIM]
    return out


def _na_body(q_ref, k_ref, v_ref, bias_ref, o_ref, *, rows, rows_per_step, n_lat_steps, seq, ctx):
    i = pl.program_id(1)
    mask = _head_block_mask()
    kc = k_ref[0, seq:seq + ctx, :]
    vc = v_ref[0, seq:seq + ctx, :]

    @pl.when(i < n_lat_steps)
    def _():
        for j in range(rows_per_step):
            r = i * rows_per_step + j
            rs = jnp.clip(r - WIN_ROWS // 2, 0, rows - WIN_ROWS)
            off = rs - r + (WIN_ROWS - 1)
            start = pl.multiple_of(rs * GRID_W, GRID_W)
            kw = k_ref[0, pl.ds(start, WIN_ROWS * GRID_W), :]
            vw = v_ref[0, pl.ds(start, WIN_ROWS * GRID_W), :]
            q = q_ref[0, j * GRID_W:(j + 1) * GRID_W, :]
            out = _na_attend(q, [(kw, vw, bias_ref[off]), (kc, vc, None)], mask)
            o_ref[0, j * GRID_W:(j + 1) * GRID_W, :] = out.astype(o_ref.dtype)

    @pl.when(i >= n_lat_steps)
    def _():
        for j in range(rows_per_step):
            q = q_ref[0, j * GRID_W:(j + 1) * GRID_W, :]
            out = _na_attend(q, [(kc, vc, None)], mask)
            o_ref[0, j * GRID_W:(j + 1) * GRID_W, :] = out.astype(o_ref.dtype)


def _na_attention(qkv, bias, *, seq, ctx):
    b, lt, _ = qkv.shape
    rows = seq // GRID_W
    rows_per_step = TM // GRID_W
    n_steps = lt // TM
    n_lat_steps = seq // TM
    body = functools.partial(_na_body, rows=rows, rows_per_step=rows_per_step, n_lat_steps=n_lat_steps,
                             seq=seq, ctx=ctx)
    return pl.pallas_call(
        body,
        grid=(b, n_steps),
        in_specs=[pl.BlockSpec((1, TM, NA_W), lambda bi, i: (bi, i, 0)),
                  pl.BlockSpec((1, lt, NA_W), lambda bi, i: (bi, 0, 1)),
                  pl.BlockSpec((1, lt, NA_W), lambda bi, i: (bi, 0, 2)),
                  pl.BlockSpec(bias.shape, lambda bi, i: (0, 0, 0))],
        out_specs=pl.BlockSpec((1, TM, NA_W), lambda bi, i: (bi, i, 0)),
        out_shape=jax.ShapeDtypeStruct((b, lt, NA_W), BF16),
        compiler_params=_params(("parallel", "arbitrary")),
        name="na_attention",
    )(qkv, qkv, qkv, bias)


def _na_bias_tables(rpb):
    q = np.arange(GRID_W)
    cs = np.clip(q - WIN_COLS // 2, 0, GRID_W - WIN_COLS)
    c = np.arange(GRID_W)
    inside = (c[None, :] >= cs[:, None]) & (c[None, :] < cs[:, None] + WIN_COLS)
    dc = c[None, :] - q[:, None] + (WIN_COLS - 1)
    place = ((np.arange(2 * WIN_COLS - 1)[:, None, None] == dc[None]) & inside[None]).astype(np.float32)
    win = jnp.stack([rpb[:, o:o + WIN_ROWS, :] for o in range(WIN_ROWS)]).astype(F32)
    vals = jnp.einsum('ohik,kqc->ohqic', win, jnp.asarray(place), precision=lax.Precision.HIGHEST)
    vals = jnp.where(inside[None, None, :, None, :], vals, NEG_BIAS)
    return vals.reshape(WIN_ROWS, NA_HEADS * GRID_W, WIN_ROWS * GRID_W)


def _softplus(v):
    return jnp.maximum(v, 0.0) + jnp.log1p(jnp.exp(-jnp.abs(v)))


def _lru_body(up_ref, uc_ref, un_ref, cw_ref, cb_ref, wa_ref, ba_ref, wi_ref, bi_ref, lam_ref,
              o_ref, a_buf, x_buf, st_ref, *, n_lat, chunk, n_batch):
    d = pl.program_id(0)
    i = pl.program_id(1)
    c = _lru_chunk(d, i, n_lat)
    prev_ok = jnp.logical_and(c >= 1, c < n_lat)
    next_ok = c <= n_lat - 2

    row = lax.broadcasted_iota(jnp.int32, (chunk, 1), 0)
    cw = cw_ref[...]
    log_decay = -LRU_C * _softplus(-lam_ref[0])
    for b in range(n_batch):
        ub = uc_ref[b]
        p_row = jnp.where(prev_ok, up_ref[b, 7:8, :], 0.0)
        n0 = jnp.where(next_ok, un_ref[b, 0:1, :], 0.0)
        n1 = jnp.where(next_ok, un_ref[b, 1:2, :], 0.0)
        um1 = jnp.where(row == 0, p_row, pltpu.roll(ub, 1, 0))
        up1 = jnp.where(row == chunk - 1, n0, pltpu.roll(ub, chunk - 1, 0))
        up2 = jnp.where(row == chunk - 1, n1, jnp.where(row == chunk - 2, n0, pltpu.roll(ub, chunk - 2, 0)))
        cv = um1 * cw[0:1] + cb_ref[...]
        cv = cv + ub * cw[1:2]
        cv = cv + up1 * cw[2:3]
        cv = cv + up2 * cw[3:4]
        cv16 = cv.astype(BF16)
        r = _sigmoid(_dot(cv16, wa_ref[0]) + ba_ref[0])
        gi = _sigmoid(_dot(cv16, wi_ref[0]) + bi_ref[0])
        log_a = r * log_decay
        a = jnp.exp(log_a)
        a_buf[b] = a
        x_buf[b] = jnp.sqrt(1.0 - a * a) * (gi * cv)

    @pl.when(i == 0)
    def _():
        st_ref[...] = jnp.zeros_like(st_ref)

    def scan(reverse):
        n_groups = chunk // SUBLANES

        def group(gi, hs):
            base = pl.multiple_of((n_groups - 1 - gi if reverse else gi) * SUBLANES, SUBLANES)
            hs = list(hs)
            for k in range(SUBLANES):
                r = base + (SUBLANES - 1 - k if reverse else k)
                for b in range(n_batch):
                    hs[b] = a_buf[b, pl.ds(r, 1), :] * hs[b] + x_buf[b, pl.ds(r, 1), :]
                    o_ref[0, b, pl.ds(r, 1), :] = hs[b]
            return tuple(hs)

        hs = lax.fori_loop(0, n_groups, group, tuple(st_ref[b, 0:1, :] for b in range(n_batch)))
        for b in range(n_batch):
            st_ref[b, 0:1, :] = hs[b]

    @pl.when(d == 0)
    def _():
        scan(False)

    @pl.when(d == 1)
    def _():
        scan(True)


def _lru_chunk(d, i, n_lat):
    return jnp.where(i == 0, n_lat, jnp.where(d == 0, i - 1, n_lat - i))


def _lru_scan(u, conv_w, conv_b, w_a, b_a, w_i, b_i, lam, *, seq, ctx):
    b, lt, w = u.shape
    chunk = ctx
    n_lat = seq // chunk
    n_chunks = lt // chunk
    sub = chunk // 8
    n_sub = lt // 8

    cur = lambda d, i: (0, _lru_chunk(d, i, n_lat), 0)
    prev = lambda d, i: (0, jnp.maximum(_lru_chunk(d, i, n_lat) * sub - 1, 0), 0)
    nxt = lambda d, i: (0, jnp.minimum((_lru_chunk(d, i, n_lat) + 1) * sub, n_sub - 1), 0)
    per_dir = lambda a: pl.BlockSpec((1,) + a.shape[1:], lambda d, i: (d,) + (0,) * (a.ndim - 1))
    full = lambda a: pl.BlockSpec(a.shape, lambda d, i: (0,) * a.ndim)
    body = functools.partial(_lru_body, n_lat=n_lat, chunk=chunk, n_batch=b)
    return pl.pallas_call(
        body,
        grid=(2, n_chunks),
        in_specs=[pl.BlockSpec((b, 8, w), prev), pl.BlockSpec((b, chunk, w), cur), pl.BlockSpec((b, 8, w), nxt),
                  full(conv_w), full(conv_b), per_dir(w_a), per_dir(b_a), per_dir(w_i), per_dir(b_i), per_dir(lam)],
        out_specs=pl.BlockSpec((1, b, chunk, w), lambda d, i: (d, 0, _lru_chunk(d, i, n_lat), 0)),
        out_shape=jax.ShapeDtypeStruct((2, b, lt, w), F32),
        scratch_shapes=[pltpu.VMEM((b, chunk, w), F32), pltpu.VMEM((b, chunk, w), F32), pltpu.VMEM((b, 8, w), F32)],
        compiler_params=_params(("arbitrary", "arbitrary")),
        name="lru_scan",
    )(u, u, u, conv_w, conv_b, w_a, b_a, w_i, b_i, lam)


def _mla_body(q_ref, k_ref, v_ref, o_ref, m_ref, acc_ref, *, tk, nk):
    m_ref[...] = jnp.full(m_ref.shape, NEG_BIAS, F32)
    acc_ref[...] = jnp.zeros(acc_ref.shape, F32)

    def chunk(ci, carry):
        start = pl.multiple_of(ci * tk, tk)
        for h in range(MLA_HEADS):
            cols = slice(h * HEAD_PAD, (h + 1) * HEAD_PAD)
            s = _dot_nt(q_ref[0, :, cols], k_ref[0, pl.ds(start, tk), cols])
            blocks = [s[:, j * LANES:(j + 1) * LANES] for j in range(tk // LANES)]
            m_old = m_ref[h]
            m_new = jnp.maximum(m_old, jnp.max(functools.reduce(jnp.maximum, blocks), axis=-1, keepdims=True))
            p = jnp.concatenate([jnp.exp((blk - m_new).astype(BF16)) for blk in blocks], axis=-1)
            acc_ref[h] = jnp.exp(m_old - m_new) * acc_ref[h] + _dot(p, v_ref[0, pl.ds(start, tk), cols])
            m_ref[h] = m_new
        return carry

    lax.fori_loop(0, nk, chunk, 0)
    outs = [acc_ref[h][:, :V_HEAD] / acc_ref[h][:, V_HEAD:V_HEAD + 1] for h in range(MLA_HEADS)]
    o_ref[0] = jnp.concatenate(outs, axis=-1).astype(o_ref.dtype)


def _mla_attention(mq, mk, mv, *, tq, q_block0, n_q, k_block, k_block0, tk):
    b, lt, nw = mq.shape
    body = functools.partial(_mla_body, tk=tk, nk=k_block // tk)
    return pl.pallas_call(
        body,
        grid=(b, n_q),
        in_specs=[pl.BlockSpec((1, tq, nw), lambda bi, i: (bi, q_block0 + i, 0)),
                  pl.BlockSpec((1, k_block, nw), lambda bi, i: (bi, k_block0, 0), pipeline_mode=pl.Buffered(1)),
                  pl.BlockSpec((1, k_block, nw), lambda bi, i: (bi, k_block0, 0), pipeline_mode=pl.Buffered(1))],
        out_specs=pl.BlockSpec((1, tq, MLA_W), lambda bi, i: (bi, i, 0)),
        out_shape=jax.ShapeDtypeStruct((b, n_q * tq, MLA_W), BF16),
        scratch_shapes=[pltpu.VMEM((MLA_HEADS, tq, LANES), F32), pltpu.VMEM((MLA_HEADS, tq, HEAD_PAD), F32)],
        compiler_params=_params(("parallel", "arbitrary")),
        name="mla_attention",
    )(mq, mk, mv)


def _route(logits, rb):
    sel = _sigmoid(logits) + rb
    sc = [sel[j:j + 1, :] for j in range(N_EXPERTS)]
    gscore = []
    for g in range(N_GROUPS):
        a, b, c, d = sc[4 * g:4 * g + 4]
        hi1, lo1 = jnp.maximum(a, b), jnp.minimum(a, b)
        hi2, lo2 = jnp.maximum(c, d), jnp.minimum(c, d)
        gscore.append(jnp.maximum(hi1, hi2) + jnp.maximum(jnp.minimum(hi1, hi2), jnp.maximum(lo1, lo2)))
    best = jnp.zeros_like(gscore[0], dtype=jnp.int32)
    bval = gscore[0]
    for g in range(1, N_GROUPS):
        upd = gscore[g] > bval
        best = jnp.where(upd, g, best)
        bval = jnp.where(upd, gscore[g], bval)

    def pick(cols, j):
        out = cols[j]
        for g in range(1, N_GROUPS):
            out = jnp.where(best == g, cols[4 * g + j], out)
        return out

    v = [pick(sc, j) for j in range(EXPERTS_PER_GROUP)]
    code = jnp.zeros_like(best)
    for i in range(EXPERTS_PER_GROUP):
        rank = jnp.zeros_like(best)
        for j in range(EXPERTS_PER_GROUP):
            if j == i:
                continue
            beats = (v[j] > v[i]) if j > i else (v[j] >= v[i])
            rank = rank + beats.astype(jnp.int32)
        code = code + jnp.where(rank < 2, 1 << i, 0)
    pair = jnp.zeros_like(best)
    for p, (lo, hi) in enumerate(_PAIRS):
        pair = jnp.where(code == (1 << lo) + (1 << hi), p, pair)
    return best * N_PAIRS + pair


def _proj_out_body(ya_ref, hf_ref, hr_ref, gg_ref, yc_ref, x_ref, wa_ref, wb_ref, wc_ref, g1_ref, g2_ref,
                   gate_s_ref, gate_c_ref, shift_s_ref, shift_c_ref, scale_s_ref, scale_c_ref, rwt_ref, rb_ref,
                   xo_ref, h2_ref, cls_ref, *, rt):
    yb = (hf_ref[0, 0] + hr_ref[0, 0]) * gg_ref[...].astype(F32)
    y = _dot(ya_ref[...], wa_ref[...]) + _dot(yb.astype(BF16), wb_ref[...]) + _dot(yc_ref[...], wc_ref[...])
    ga_t, ga_b = rt.mod_rows(gate_s_ref, gate_c_ref)
    sh_t, sh_b = rt.mod_rows(shift_s_ref, shift_c_ref)
    sc_t, sc_b = rt.mod_rows(scale_s_ref, scale_c_ref)
    x = x_ref[...] + rt.by_rows(_rms(y, g1_ref[...]), lambda v: ga_t * v, lambda v: ga_b * v)
    xo_ref[...] = x
    h2 = rt.by_rows(_rms(x, g2_ref[...]), lambda v: v * (1.0 + sc_t) + sh_t,
                    lambda v: v * (1.0 + sc_b) + sh_b).astype(BF16)
    _store_token_tiles(h2_ref, h2.astype(F32))
    cls_ref[0] = _route(_dot_nt(rwt_ref[...], h2), rb_ref[...])


def _proj_out(ya, hscan, gg, yc, x, w_a, w_b, w_c, g1, g2, gate, shift, scale, rw, rb, *, rt):
    t, d = x.shape
    tm = rt.tm
    nt = t // tm
    row = lambda n: pl.BlockSpec((tm, n), lambda i: (i, 0))
    full = lambda a: pl.BlockSpec(a.shape, lambda i: (0,) * a.ndim)
    scan_spec = lambda dd: pl.BlockSpec((1, 1, tm, LRU_W),
                                        lambda i: (dd, i // rt.tiles_per_sample, i % rt.tiles_per_sample, 0))
    return pl.pallas_call(
        functools.partial(_proj_out_body, rt=rt),
        grid=(nt,),
        in_specs=[row(NA_W), scan_spec(0), scan_spec(1), row(LRU_W), row(MLA_W), row(d),
                  full(w_a), full(w_b), full(w_c), full(g1), full(g2),
                  *rt.mod_specs(d), *rt.mod_specs(d), *rt.mod_specs(d), full(rw), full(rb)],
        out_specs=[row(d), pl.BlockSpec((tm * SUBLANES, LANES), lambda i: (i, 0)),
                   pl.BlockSpec((1, 1, tm), lambda i: (i, 0, 0))],
        out_shape=[jax.ShapeDtypeStruct((t, d), F32),
                   jax.ShapeDtypeStruct((t * SUBLANES, LANES), F32),
                   jax.ShapeDtypeStruct((nt, 1, tm), jnp.int32)],
        compiler_params=_params(("parallel",)),
        name="proj_out",
    )(ya, hscan, hscan, gg, yc, x, w_a, w_b, w_c, g1, g2, gate, gate, shift, shift, scale, scale, rw, rb)


def _store_token_tiles(ref, v, token0=0):
    n = v.shape[0]
    for s in range(SUBLANES):
        ref[pl.ds(token0 * SUBLANES + s, n, stride=SUBLANES), :] = v[:, s * LANES:(s + 1) * LANES]


def _load_token_tiles(ref, n, token0=0):
    return jnp.concatenate([ref[pl.ds(token0 * SUBLANES + s, n, stride=SUBLANES), :] for s in range(SUBLANES)],
                           axis=1)


def _gather_token_tiles(idx_ref, base, src_ref, buf_ref, n):
    def copy(r, carry):
        j = idx_ref[base + r]
        buf_ref[pl.ds(pl.multiple_of(r * SUBLANES, SUBLANES), SUBLANES), :] = (
            src_ref[0, pl.ds(pl.multiple_of(j * SUBLANES, SUBLANES), SUBLANES), :])
        return carry

    lax.fori_loop(0, n, copy, 0, unroll=8)


def _dispatch_body(pos_ref, h_ref, o_ref, buf_ref, src_ref, *, tokens_per_sample):
    @pl.when(pl.program_id(1) == 0)
    def _():
        base = pl.program_id(0) * tokens_per_sample

        def invert(t, carry):
            src_ref[pos_ref[base + t]] = t
            return carry

        lax.fori_loop(0, tokens_per_sample, invert, 0, unroll=8)

    tm = o_ref.shape[0]
    _gather_token_tiles(src_ref, pl.program_id(1) * tm, h_ref, buf_ref, tm)
    o_ref[...] = _load_token_tiles(buf_ref, tm).astype(o_ref.dtype)


def _dispatch(pos, h2t, *, n_batch, tokens_per_sample, tm):
    n_tiles = tokens_per_sample // tm
    return pl.pallas_call(
        functools.partial(_dispatch_body, tokens_per_sample=tokens_per_sample),
        grid_spec=pltpu.PrefetchScalarGridSpec(
            num_scalar_prefetch=1,
            grid=(n_batch, n_tiles),
            in_specs=[pl.BlockSpec((1, tokens_per_sample * SUBLANES, LANES), lambda b, i, pos: (b, 0, 0),
                                   pipeline_mode=pl.Buffered(1))],
            out_specs=pl.BlockSpec((tm, D_MODEL), lambda b, i, pos: (b * n_tiles + i, 0)),
            scratch_shapes=[pltpu.VMEM((tm * SUBLANES, LANES), F32), pltpu.SMEM((tokens_per_sample,), jnp.int32)]),
        out_shape=jax.ShapeDtypeStruct((n_batch * tokens_per_sample, D_MODEL), BF16),
        compiler_params=_params(("arbitrary", "arbitrary")),
        name="moe_dispatch",
    )(pos, h2t.reshape(n_batch, tokens_per_sample * SUBLANES, LANES))


def _moe_body(tile_ref, elo_ref, ehi_ref, lo_ref, hi_ref, valid_ref, first_ref,
              x_ref, rw_ref, wg0_ref, wu0_ref, wd0_ref, wg1_ref, wu1_ref, wd1_ref, o_ref):
    k = pl.program_id(0)

    @pl.when(valid_ref[k] == 1)
    def _():
        @pl.when(first_ref[k] == 1)
        def _():
            o_ref[...] = jnp.zeros(o_ref.shape, o_ref.dtype)

        def visit(row0, n):
            x = x_ref[row0:row0 + n, :]
            aff = _sigmoid(_dot(x, rw_ref[...]))
            lane = lax.broadcasted_iota(jnp.int32, aff.shape, 1)
            a_lo = jnp.sum(jnp.where(lane == elo_ref[k], aff, 0.0), axis=-1, keepdims=True)
            a_hi = jnp.sum(jnp.where(lane == ehi_ref[k], aff, 0.0), axis=-1, keepdims=True)
            den = a_lo + a_hi
            row = row0 + lax.broadcasted_iota(jnp.int32, (n, 1), 0)
            mine = jnp.logical_and(row >= lo_ref[k], row < hi_ref[k])

            def add_expert(a_e, wg, wu, wd):
                a = _dot(x, wg[0])
                he = (a * _sigmoid(a)) * _dot(x, wu[0])
                y = jnp.where(mine, a_e / den, 0.0) * _dot(he.astype(BF16), wd[0])
                _store_token_tiles(o_ref, _load_token_tiles(o_ref, n, row0) + y, row0)

            add_expert(a_lo, wg0_ref, wu0_ref, wd0_ref)
            add_expert(a_hi, wg1_ref, wu1_ref, wd1_ref)

        half = TM // 2
        in_top = hi_ref[k] <= half
        in_bottom = lo_ref[k] >= half

        @pl.when(in_top)
        def _():
            visit(0, half)

        @pl.when(in_bottom)
        def _():
            visit(half, half)

        @pl.when(jnp.logical_not(jnp.logical_or(in_top, in_bottom)))
        def _():
            visit(0, TM)


def _moe_routed(items, xs, rw, w_gate, w_up, w_down, *, expert_base):
    t, d = xs.shape
    _, _, de = w_gate.shape
    n_items = items[0].shape[0]
    xspec = pl.BlockSpec((TM, d), lambda k, tile, *_: (tile[k], 0))
    w_in_spec = lambda which: pl.BlockSpec(
        (1, d, de), lambda k, tile, elo, ehi, *_: (expert_base + (elo, ehi)[which][k], 0, 0))
    w_out_spec = lambda which: pl.BlockSpec(
        (1, de, d), lambda k, tile, elo, ehi, *_: (expert_base + (elo, ehi)[which][k], 0, 0))
    return pl.pallas_call(
        _moe_body,
        grid_spec=pltpu.PrefetchScalarGridSpec(
            num_scalar_prefetch=len(items),
            grid=(n_items,),
            in_specs=[xspec, pl.BlockSpec(rw.shape, lambda k, *_: (0, 0)),
                      w_in_spec(0), w_in_spec(0), w_out_spec(0), w_in_spec(1), w_in_spec(1), w_out_spec(1)],
            out_specs=pl.BlockSpec((TM * SUBLANES, LANES), lambda k, tile, *_: (tile[k], 0))),
        out_shape=jax.ShapeDtypeStruct((t * SUBLANES, LANES), F32),
        compiler_params=_params(("arbitrary",)),
        name="moe_routed",
    )(*items, xs, rw, w_gate, w_up, w_down, w_gate, w_up, w_down)


def _combine_body(pos_ref, y_ref, x_ref, gate_ref, g_ref, o_ref, buf_ref, *, tokens_per_sample):
    base = pl.program_id(0) * tokens_per_sample + pl.program_id(1) * TM
    _gather_token_tiles(pos_ref, base, y_ref, buf_ref, TM)
    o_ref[...] = x_ref[...] + gate_ref[0] * _rms(_load_token_tiles(buf_ref, TM), g_ref[...])


def _combine(pos, ys, x, gate, g, *, n_batch, tokens_per_sample, n_lat_tiles, n_out_tiles):
    n_tiles = tokens_per_sample // TM
    d = x.shape[1]
    return pl.pallas_call(
        functools.partial(_combine_body, tokens_per_sample=tokens_per_sample),
        grid_spec=pltpu.PrefetchScalarGridSpec(
            num_scalar_prefetch=1,
            grid=(n_batch, n_out_tiles),
            in_specs=[pl.BlockSpec((1, tokens_per_sample * SUBLANES, LANES), lambda b, i, pos: (b, 0, 0),
                                   pipeline_mode=pl.Buffered(1)),
                      pl.BlockSpec((TM, d), lambda b, i, pos: (b * n_tiles + i, 0)),
                      pl.BlockSpec((1, 1, d), lambda b, i, pos: (jnp.where(i >= n_lat_tiles, n_batch, b), 0, 0)),
                      pl.BlockSpec(g.shape, lambda b, i, pos: (0, 0))],
            out_specs=pl.BlockSpec((TM, d), lambda b, i, pos: (b * n_out_tiles + i, 0)),
            scratch_shapes=[pltpu.VMEM((TM * SUBLANES, LANES), F32)]),
        out_shape=jax.ShapeDtypeStruct((n_batch * n_out_tiles * TM, d), F32),
        compiler_params=_params(("arbitrary", "arbitrary")),
        name="moe_combine",
    )(pos, ys.reshape(n_batch, tokens_per_sample * SUBLANES, LANES), x, gate, g)


def _routing_plan(cls, *, n_tiles):
    b, l = cls.shape
    n_items = n_tiles + N_CLASSES - 1
    onehot = (cls[..., None] == jnp.arange(N_CLASSES, dtype=jnp.int32)).astype(jnp.int32)
    csum = jnp.cumsum(onehot, axis=1)
    counts = csum[:, -1]
    starts = jnp.cumsum(counts, axis=1) - counts
    pos = jnp.sum(onehot * (starts[:, None, :] + csum - 1), axis=-1)
    bi = jnp.arange(b, dtype=jnp.int32)[:, None]

    first_tile = starts // TM
    last_tile = (starts + counts - 1) // TM
    m = jnp.where(counts > 0, last_tile - first_tile + 1, 0)
    o_end = jnp.cumsum(m, axis=1)
    o_start = o_end - m
    total = o_end[:, -1:]
    k = jnp.arange(n_items, dtype=jnp.int32)[None, :]
    kk = jnp.minimum(k, total - 1)
    c_k = jnp.sum((kk[:, :, None] >= o_end[:, None, :]).astype(jnp.int32), axis=-1)
    take = lambda a: jnp.take_along_axis(a, c_k, axis=1)
    tile_k = take(first_tile) + kk - take(o_start)
    valid = (k < total).astype(jnp.int32)
    lo = jnp.maximum(take(starts), tile_k * TM) - tile_k * TM
    hi = jnp.minimum(take(starts + counts), (tile_k + 1) * TM) - tile_k * TM
    prev_tile = jnp.concatenate([jnp.full((b, 1), -1, jnp.int32), tile_k[:, :-1]], axis=1)
    first = (tile_k != prev_tile).astype(jnp.int32)
    pair_lo = jnp.asarray([p[0] for p in _PAIRS], jnp.int32)
    pair_hi = jnp.asarray([p[1] for p in _PAIRS], jnp.int32)
    e_lo = (c_k // N_PAIRS) * EXPERTS_PER_GROUP + pair_lo[c_k % N_PAIRS]
    e_hi = (c_k // N_PAIRS) * EXPERTS_PER_GROUP + pair_hi[c_k % N_PAIRS]
    tile_g = tile_k + bi * n_tiles
    items = tuple(a.reshape(-1).astype(jnp.int32) for a in (tile_g, e_lo, e_hi, lo, hi, valid, first))
    return pos.reshape(-1).astype(jnp.int32), items


def _rope_tables(seq, ctx):
    t = np.arange(seq)
    axis_dim = QK_ROPE // 2
    inv = np.float32(ROPE_BASE) ** (-np.arange(0, axis_dim, 2, dtype=np.float32) / np.float32(axis_dim))
    ang_r = (t // GRID_W).astype(np.float32)[:, None] * inv
    ang_c = (t % GRID_W).astype(np.float32)[:, None] * inv
    cos = jnp.asarray(np.concatenate([np.cos(ang_r), np.cos(ang_r), np.cos(ang_c), np.cos(ang_c)], axis=-1))
    sin = jnp.asarray(np.concatenate([-np.sin(ang_r), np.sin(ang_r), -np.sin(ang_c), np.sin(ang_c)], axis=-1))
    pad = HEAD_PAD - QK_NOPE - QK_ROPE
    cos = jnp.concatenate([jnp.ones((seq, QK_NOPE), F32), cos, jnp.ones((seq, pad), F32)], axis=-1)
    sin = jnp.concatenate([jnp.zeros((seq, QK_NOPE), F32), sin, jnp.zeros((seq, pad), F32)], axis=-1)
    cos = jnp.concatenate([cos, jnp.ones((ctx, HEAD_PAD), F32)], axis=0)
    sin = jnp.concatenate([sin, jnp.zeros((ctx, HEAD_PAD), F32)], axis=0)
    return cos, sin


_ROPE_SWAP = np.concatenate([np.arange(8, 16), np.arange(0, 8), np.arange(24, 32), np.arange(16, 24)])


def _layout_w_in(w_in):
    d = w_in.shape[0]
    z = lambda n: jnp.zeros((d, n), w_in.dtype)
    kr = w_in[:, _C_KR:_C_KR + QK_ROPE]
    pad = HEAD_PAD - QK_NOPE - QK_ROPE
    return jnp.concatenate([w_in[:, :_C_KR], z(QK_NOPE), kr, z(pad), z(QK_NOPE), kr[:, _ROPE_SWAP], z(pad)],
                           axis=1).astype(BF16)


def _layout_w_q(w_q):
    r = w_q.shape[0]
    wh = w_q.reshape(r, MLA_HEADS, QK_NOPE + QK_ROPE)
    pad = HEAD_PAD - QK_NOPE - QK_ROPE
    full = jnp.concatenate([wh, jnp.zeros((r, MLA_HEADS, pad), w_q.dtype)], axis=-1)
    swap = jnp.concatenate([jnp.zeros((r, MLA_HEADS, QK_NOPE), w_q.dtype), wh[:, :, QK_NOPE + _ROPE_SWAP],
                            jnp.zeros((r, MLA_HEADS, pad), w_q.dtype)], axis=-1)
    return jnp.concatenate([full.reshape(r, -1), swap.reshape(r, -1)], axis=1).astype(BF16)


def _layout_w_kv(w_kv):
    r = w_kv.shape[0]
    wh = w_kv.reshape(r, MLA_HEADS, QK_NOPE + V_HEAD)
    zk = jnp.zeros((r, MLA_HEADS, HEAD_PAD - QK_NOPE), w_kv.dtype)
    zv = jnp.zeros((r, MLA_HEADS, HEAD_PAD - V_HEAD), w_kv.dtype)
    k = jnp.concatenate([wh[:, :, :QK_NOPE], zk], axis=-1).reshape(r, -1)
    v = jnp.concatenate([wh[:, :, QK_NOPE:], zv], axis=-1).reshape(r, -1)
    return jnp.concatenate([k, v], axis=1).astype(BF16)


def _block_diag(w):
    nd, nb, k, _ = w.shape
    eye = jnp.eye(nb, dtype=w.dtype)
    return (w[:, :, :, None, :] * eye[None, :, None, :, None]).reshape(nd, nb * k, nb * k).astype(BF16)


def _pick_chunk(n, candidates):
    for c in candidates:
        if n % c == 0:
            return c
    raise ValueError(f"no chunk size for {n}")


def kernel(x, c, ctx, c_ctx, w_ada, b_ada, g_norm, w_in, w_out, na_rpb, conv_w, conv_b, lru_w_a, lru_b_a, lru_w_i,
           lru_b_i, lru_lam, mla_g_q, mla_w_q, mla_g_kv, mla_w_kv, router_w, router_b, exp_w_gate, exp_w_up,
           exp_w_down):
    b, seq, d = x.shape
    n_ctx = ctx.shape[1]
    depth = w_ada.shape[0]
    lt = seq + n_ctx
    assert d == D_MODEL == SUBLANES * LANES and n_ctx == TM and seq % TM == 0 and seq // GRID_W >= WIN_ROWS
    assert b + 1 <= 8
    tiles_per_sample = lt // TM
    n_lat_tiles = seq // TM
    rt = _RowTiling(seq, lt, b, _pick_chunk(lt, (PROJ_TM, TM)))

    xa = jnp.concatenate([x, ctx], axis=1).reshape(b * lt, d)

    c_rows = jnp.concatenate([c, c_ctx[None, :], jnp.zeros((8 - b - 1, d), F32)], axis=0)
    mod = _ada_tables(c_rows, w_ada, b_ada).reshape(depth, 8, N_MOD, 1, d)

    cos_t, sin_t = _rope_tables(seq, n_ctx)
    ones_row = jnp.zeros((MLA_HEADS, HEAD_PAD), F32).at[:, V_HEAD].set(1.0).reshape(1, -1)
    rw = jnp.concatenate([router_w, jnp.zeros((d, LANES - N_EXPERTS), F32)], axis=1).astype(BF16)
    wg_all, wu_all, wd_all = (w.astype(BF16).reshape((depth * N_EXPERTS,) + w.shape[2:])
                              for w in (exp_w_gate, exp_w_up, exp_w_down))
    rwt = router_w.T.astype(BF16)
    rb = router_b[:, None]
    tk = _pick_chunk(lt, (768, 512, 256))
    tq = _pick_chunk(seq, (2048, 1024, 512, 256))
    w_big_all = jax.vmap(_layout_w_in)(w_in)
    wq2_all = jax.vmap(_layout_w_q)(mla_w_q)
    wkv2_all = jax.vmap(_layout_w_kv)(mla_w_kv)
    bias_all = jax.vmap(_na_bias_tables)(na_rpb)
    lru_wa_all = jax.vmap(_block_diag)(lru_w_a)
    lru_wi_all = jax.vmap(_block_diag)(lru_w_i)
    wo_all = w_out.astype(BF16)

    for l in range(depth):
        last = l == depth - 1
        m = lambda k: mod[l, :b + 1, k]
        g = g_norm[l]
        qkv, u, gg, mq, mk, mv = _proj_in(
            xa, m(0), m(1), g[0:1], w_big_all[l], mla_g_q[l][None, :], wq2_all[l],
            mla_g_kv[l][None, :], wkv2_all[l], cos_t, sin_t, ones_row, rt=rt)

        ya = _na_attention(qkv.reshape(b, lt, -1), bias_all[l], seq=seq, ctx=n_ctx)
        hscan = _lru_scan(u.reshape(b, lt, -1), conv_w[l], conv_b[l][None, :], lru_wa_all[l],
                          lru_b_a[l][:, None, :], lru_wi_all[l], lru_b_i[l][:, None, :],
                          lru_lam[l][:, None, :], seq=seq, ctx=n_ctx)
        mq3, mk3, mv3 = (a.reshape(b, lt, -1) for a in (mq, mk, mv))
        yc = _mla_attention(mq3, mk3, mv3, tq=tq, q_block0=0, n_q=seq // tq, k_block=lt, k_block0=0, tk=tk)
        if last:
            yc = jnp.concatenate([yc, jnp.zeros((b, n_ctx, MLA_W), BF16)], axis=1)
        else:
            yc_ctx = _mla_attention(mq3, mk3, mv3, tq=n_ctx, q_block0=seq // n_ctx, n_q=1, k_block=n_ctx,
                                    k_block0=seq // n_ctx, tk=n_ctx)
            yc = jnp.concatenate([yc, yc_ctx], axis=1)

        wo = wo_all[l]
        xa, h2t, cls = _proj_out(
            ya.reshape(b * lt, -1), hscan, gg, yc.reshape(b * lt, -1), xa,
            wo[:NA_W], wo[NA_W:NA_W + LRU_W], wo[NA_W + LRU_W:], g[1:2], g[2:3], m(2), m(3), m(4), rwt, rb, rt=rt)

        pos, items = _routing_plan(cls.reshape(b, lt), n_tiles=tiles_per_sample)
        xs = _dispatch(pos, h2t, n_batch=b, tokens_per_sample=lt, tm=rt.tm)
        ys = _moe_routed(items, xs, rw, wg_all, wu_all, wd_all, expert_base=l * N_EXPERTS)
        xa = _combine(pos, ys, xa, m(5), g[3:4], n_batch=b, tokens_per_sample=lt, n_lat_tiles=n_lat_tiles,
                      n_out_tiles=n_lat_tiles if last else tiles_per_sample)

    return xa.reshape(b, seq, d)
```

```python
import functools

import numpy as np
import jax
import jax.numpy as jnp
from jax import lax
from jax.experimental import pallas as pl
from jax.experimental.pallas import tpu as pltpu

F32 = jnp.float32
BF16 = jnp.bfloat16

D_MODEL = 1024
GRID_W = 64
NA_HEADS = 4
NA_HEAD_DIM = 64
NA_W = NA_HEADS * NA_HEAD_DIM
WIN_ROWS = 8
WIN_COLS = 16
LRU_W = 512
LRU_BLOCKS = 8
CONV_W = 4
LRU_C = 8.0
MLA_HEADS = 4
Q_LORA = 256
KV_LORA = 128
QK_NOPE = 64
QK_ROPE = 32
V_HEAD = 64
MLA_W = MLA_HEADS * V_HEAD
ROPE_BASE = 10000.0
N_EXPERTS = 16
N_GROUPS = 4
EXPERTS_PER_GROUP = 4
D_EXPERT = 512
RMS_EPS = 1e-6
N_MOD = 6

_PAIRS = ((0, 1), (0, 2), (0, 3), (1, 3), (1, 2), (3, 2))
N_PAIRS = len(_PAIRS)
N_CLASSES = N_GROUPS * N_PAIRS

LANES = 128
SUBLANES = 8
TM = 256
PROJ_TM = 768
HEAD_PAD = 128
NEG_BIAS = -1e30
VMEM_LIMIT = 56 * 1024 * 1024

_C_QA, _C_KA, _C_VA = 0, 256, 512
_C_U = 768
_C_GATE = 1280
_C_CQ = 1792
_C_CKV = 2048
_C_KR = 2176
_C_KRP = 2304
_IN_COLS = 2432


def _params(sem, vmem=VMEM_LIMIT):
    return pltpu.CompilerParams(dimension_semantics=sem, vmem_limit_bytes=vmem)


def _sigmoid(v):
    return 1.0 / (1.0 + jnp.exp(-v))


def _rms(v, g):
    return v * lax.rsqrt(jnp.mean(v * v, axis=-1, keepdims=True) + RMS_EPS) * g


def _dot(a, b):
    return jnp.dot(a, b, preferred_element_type=F32)


def _dot_nt(a, b):
    return lax.dot_general(a, b, (((1,), (1,)), ((), ())), preferred_element_type=F32)


def _ada_body(c_ref, w_ref, b_ref, o_ref):
    c = c_ref[...]
    sc = c * _sigmoid(c)
    o_ref[0] = _dot(sc.astype(BF16), w_ref[0].astype(BF16)) + b_ref[0]


def _ada_tables(c_rows, w_ada, b_ada):
    depth, d, n = w_ada.shape
    tn = 1536
    return pl.pallas_call(
        _ada_body,
        grid=(depth, n // tn),
        in_specs=[pl.BlockSpec((8, d), lambda l, j: (0, 0)),
                  pl.BlockSpec((1, d, tn), lambda l, j: (l, 0, j)),
                  pl.BlockSpec((1, 1, tn), lambda l, j: (l, 0, j))],
        out_specs=pl.BlockSpec((1, 8, tn), lambda l, j: (l, 0, j)),
        out_shape=jax.ShapeDtypeStruct((depth, 8, n), F32),
        compiler_params=_params(("parallel", "parallel")),
        name="ada_tables",
    )(c_rows, w_ada, b_ada.reshape(depth, 1, n))


class _RowTiling:
    def __init__(self, seq, tokens_per_sample, n_batch, tm):
        assert tokens_per_sample % tm == 0
        self.tm = tm
        self.n_batch = n_batch
        self.tiles_per_sample = tokens_per_sample // tm
        self.ctx_tile = seq // tm
        self.split = seq % tm

    def mod_specs(self, d):
        return [pl.BlockSpec((1, 1, d), lambda i: (i // self.tiles_per_sample, 0, 0)),
                pl.BlockSpec((1, 1, d), lambda i: (self.n_batch, 0, 0))]

    def mod_rows(self, sample_ref, ctx_ref):
        j = pl.program_id(0) % self.tiles_per_sample
        top = jnp.where(j <= self.ctx_tile, sample_ref[0], ctx_ref[0])
        bot = jnp.where(j < self.ctx_tile, sample_ref[0], ctx_ref[0])
        return top, bot

    def by_rows(self, v, f_top, f_bot):
        if self.split == 0:
            return f_bot(v)
        return jnp.concatenate([f_top(v[:self.split]), f_bot(v[self.split:])], axis=0)


def _proj_in_body(x_ref, xp_ref, xn_ref, shift_s_ref, shift_c_ref, scale_s_ref, scale_c_ref, g_ref, w_ref,
                  cw_ref, cb_ref, gq_ref, wq_ref, gkv_ref, wkv_ref, cos_ref, sin_ref, ones_ref,
                  qkv_ref, u_ref, gg_ref, mq_ref, mk_ref, mv_ref, *, mla_scale, rt, seq, tokens_per_sample):
    x = x_ref[...]
    sh_t, sh_b = rt.mod_rows(shift_s_ref, shift_c_ref)
    sc_t, sc_b = rt.mod_rows(scale_s_ref, scale_c_ref)
    h = rt.by_rows(_rms(x, g_ref[...]), lambda v: v * (1.0 + sc_t) + sh_t, lambda v: v * (1.0 + sc_b) + sh_b)
    z = _dot(h.astype(BF16), w_ref[...])
    qkv_ref[:, 0:NA_W] = (z[:, _C_QA:_C_KA] * (NA_HEAD_DIM ** -0.5)).astype(BF16)
    qkv_ref[:, NA_W:3 * NA_W] = z[:, _C_KA:_C_U].astype(BF16)
    gg_ref[...] = jax.nn.gelu(z[:, _C_GATE:_C_CQ]).astype(BF16)

    tm = rt.tm
    u = z[:, _C_U:_C_GATE]
    w_u = w_ref[:, _C_U:_C_GATE]
    u_prev = _dot((_rms(xp_ref[...], g_ref[...]) * (1.0 + sc_t) + sh_t).astype(BF16), w_u)[SUBLANES - 1:SUBLANES]
    u_next = _dot((_rms(xn_ref[...], g_ref[...]) * (1.0 + sc_b) + sh_b).astype(BF16), w_u)
    row = lax.broadcasted_iota(jnp.int32, (tm, 1), 0)
    pos = (pl.program_id(0) % rt.tiles_per_sample) * tm + row
    at = lambda *ps: functools.reduce(jnp.logical_or, [pos == p for p in ps])
    um1 = jnp.where(row == 0, u_prev, pltpu.roll(u, 1, 0))
    um1 = jnp.where(at(0, seq), 0.0, um1)
    up1 = jnp.where(row == tm - 1, u_next[0:1], pltpu.roll(u, tm - 1, 0))
    up1 = jnp.where(at(seq - 1, tokens_per_sample - 1), 0.0, up1)
    up2 = jnp.where(row == tm - 1, u_next[1:2], jnp.where(row == tm - 2, u_next[0:1], pltpu.roll(u, tm - 2, 0)))
    up2 = jnp.where(at(seq - 2, seq - 1, tokens_per_sample - 2, tokens_per_sample - 1), 0.0, up2)
    cw = cw_ref[...]
    cv = um1 * cw[0:1] + cb_ref[...]
    cv = cv + u * cw[1:2]
    cv = cv + up1 * cw[2:3]
    u_ref[...] = cv + up2 * cw[3:4]

    cos = cos_ref[...]
    sin = sin_ref[...]
    cos4 = jnp.concatenate([cos] * MLA_HEADS, axis=-1)
    sin4 = jnp.concatenate([sin] * MLA_HEADS, axis=-1)
    nw = MLA_HEADS * HEAD_PAD

    nq = _rms(z[:, _C_CQ:_C_CKV], gq_ref[...])
    q2 = _dot(nq.astype(BF16), wq_ref[...])
    mq_ref[...] = ((q2[:, :nw] * cos4 + q2[:, nw:] * sin4) * mla_scale).astype(BF16)

    nkv = _rms(z[:, _C_CKV:_C_KR], gkv_ref[...])
    kv2 = _dot(nkv.astype(BF16), wkv_ref[...])
    k_rope = z[:, _C_KR:_C_KRP] * cos + z[:, _C_KRP:_IN_COLS] * sin
    mk_ref[...] = (kv2[:, :nw] + jnp.concatenate([k_rope] * MLA_HEADS, axis=-1)).astype(BF16)
    mv_ref[...] = (kv2[:, nw:] + ones_ref[...]).astype(BF16)


def _proj_in(x, shift, scale, g, w_big, conv_w, conv_b, gq, wq2, gkv, wkv2, cos_t, sin_t, ones_row, *, rt, seq,
             tokens_per_sample):
    t, d = x.shape
    tm = rt.tm
    nt = t // tm
    nw = MLA_HEADS * HEAD_PAD
    sub = tm // SUBLANES
    row = lambda n: pl.BlockSpec((tm, n), lambda i: (i, 0))
    full = lambda a: pl.BlockSpec(a.shape, lambda i: (0,) * a.ndim)
    tabspec = pl.BlockSpec((tm, HEAD_PAD), lambda i: (i % rt.tiles_per_sample, 0))
    before = pl.BlockSpec((SUBLANES, d), lambda i: (jnp.maximum(i * sub - 1, 0), 0))
    after = pl.BlockSpec((SUBLANES, d), lambda i: (jnp.minimum((i + 1) * sub, t // SUBLANES - 1), 0))
    mla_scale = (QK_NOPE + QK_ROPE) ** -0.5
    return pl.pallas_call(
        functools.partial(_proj_in_body, mla_scale=mla_scale, rt=rt, seq=seq, tokens_per_sample=tokens_per_sample),
        grid=(nt,),
        in_specs=[row(d), before, after, *rt.mod_specs(d), *rt.mod_specs(d), full(g), full(w_big), full(conv_w),
                  full(conv_b), full(gq), full(wq2), full(gkv), full(wkv2), tabspec, tabspec, full(ones_row)],
        out_specs=[row(3 * NA_W), row(LRU_W), row(LRU_W), row(nw), row(nw), row(nw)],
        out_shape=[jax.ShapeDtypeStruct((t, 3 * NA_W), BF16),
                   jax.ShapeDtypeStruct((t, LRU_W), F32),
                   jax.ShapeDtypeStruct((t, LRU_W), BF16),
                   jax.ShapeDtypeStruct((t, nw), BF16),
                   jax.ShapeDtypeStruct((t, nw), BF16),
                   jax.ShapeDtypeStruct((t, nw), BF16)],
        compiler_params=_params(("parallel",)),
        name="proj_in",
    )(x, x, x, shift, shift, scale, scale, g, w_big, conv_w, conv_b, gq, wq2, gkv, wkv2, cos_t, sin_t, ones_row)


def _head_block_mask():
    r = lax.broadcasted_iota(jnp.int32, (NA_W, NA_W), 0) // NA_HEAD_DIM
    c = lax.broadcasted_iota(jnp.int32, (NA_W, NA_W), 1) // NA_HEAD_DIM
    return r == c


def _na_attend(q, parts, mask):
    qbig = jnp.where(mask, jnp.concatenate([q] * NA_HEADS, axis=0), jnp.zeros((), q.dtype))
    scores = []
    for k, _, bias in parts:
        s = _dot_nt(qbig, k)
        if bias is not None:
            s = s + bias
        scores.append(s)
    m = functools.reduce(jnp.maximum, [jnp.max(s, axis=-1, keepdims=True) for s in scores])
    ps = [jnp.exp(s - m) for s in scores]
    l = functools.reduce(jnp.add, [jnp.sum(p, axis=-1, keepdims=True) for p in ps])
    o = functools.reduce(jnp.add, [_dot(p.astype(BF16), v) for p, (_, v, _) in zip(ps, parts)])
    o = jnp.where(mask, o / l, 0.0)
    out = o[0:NA_HEAD_DIM]
    for h in range(1, NA_HEADS):
        out = out + o[h * NA_HEAD_DIM:(h + 1) * NA_HEAD_DIM]
    return out


def _na_body(q_ref, k_ref, v_ref, bias_ref, o_ref, *, rows, rows_per_step, n_lat_steps, seq, ctx):
    i = pl.program_id(1)
    mask = _head_block_mask()
    kc = k_ref[0, seq:seq + ctx, :]
    vc = v_ref[0, seq:seq + ctx, :]

    @pl.when(i < n_lat_steps)
    def _():
        for j in range(rows_per_step):
            r = i * rows_per_step + j
            rs = jnp.clip(r - WIN_ROWS // 2, 0, rows - WIN_ROWS)
            off = rs - r + (WIN_ROWS - 1)
            start = pl.multiple_of(rs * GRID_W, GRID_W)
            kw = k_ref[0, pl.ds(start, WIN_ROWS * GRID_W), :]
            vw = v_ref[0, pl.ds(start, WIN_ROWS * GRID_W), :]
            q = q_ref[0, j * GRID_W:(j + 1) * GRID_W, :]
            out = _na_attend(q, [(kw, vw, bias_ref[off]), (kc, vc, None)], mask)
            o_ref[0, j * GRID_W:(j + 1) * GRID_W, :] = out.astype(o_ref.dtype)

    @pl.when(i >= n_lat_steps)
    def _():
        for j in range(rows_per_step):
            q = q_ref[0, j * GRID_W:(j + 1) * GRID_W, :]
            out = _na_attend(q, [(kc, vc, None)], mask)
            o_ref[0, j * GRID_W:(j + 1) * GRID_W, :] = out.astype(o_ref.dtype)


def _na_attention(qkv, bias, *, seq, ctx):
    b, lt, _ = qkv.shape
    rows = seq // GRID_W
    rows_per_step = TM // GRID_W
    n_steps = lt // TM
    n_lat_steps = seq // TM
    body = functools.partial(_na_body, rows=rows, rows_per_step=rows_per_step, n_lat_steps=n_lat_steps,
                             seq=seq, ctx=ctx)
    return pl.pallas_call(
        body,
        grid=(b, n_steps),
        in_specs=[pl.BlockSpec((1, TM, NA_W), lambda bi, i: (bi, i, 0)),
                  pl.BlockSpec((1, lt, NA_W), lambda bi, i: (bi, 0, 1)),
                  pl.BlockSpec((1, lt, NA_W), lambda bi, i: (bi, 0, 2)),
                  pl.BlockSpec(bias.shape, lambda bi, i: (0, 0, 0))],
        out_specs=pl.BlockSpec((1, TM, NA_W), lambda bi, i: (bi, i, 0)),
        out_shape=jax.ShapeDtypeStruct((b, lt, NA_W), BF16),
        compiler_params=_params(("parallel", "arbitrary")),
        name="na_attention",
    )(qkv, qkv, qkv, bias)


def _na_bias_tables(rpb):
    q = np.arange(GRID_W)
    cs = np.clip(q - WIN_COLS // 2, 0, GRID_W - WIN_COLS)
    c = np.arange(GRID_W)
    inside = (c[None, :] >= cs[:, None]) & (c[None, :] < cs[:, None] + WIN_COLS)
    dc = c[None, :] - q[:, None] + (WIN_COLS - 1)
    place = ((np.arange(2 * WIN_COLS - 1)[:, None, None] == dc[None]) & inside[None]).astype(np.float32)
    win = jnp.stack([rpb[:, o:o + WIN_ROWS, :] for o in range(WIN_ROWS)]).astype(F32)
    vals = jnp.einsum('ohik,kqc->ohqic', win, jnp.asarray(place), precision=lax.Precision.HIGHEST)
    vals = jnp.where(inside[None, None, :, None, :], vals, NEG_BIAS)
    return vals.reshape(WIN_ROWS, NA_HEADS * GRID_W, WIN_ROWS * GRID_W)


def _softplus(v):
    return jnp.maximum(v, 0.0) + jnp.log1p(jnp.exp(-jnp.abs(v)))


def _lru_body(cv_ref, wa_ref, ba_ref, wi_ref, bi_ref, lam_ref, o_ref, a_buf, x_buf, st_ref, *, chunk, n_batch):
    d = pl.program_id(0)
    i = pl.program_id(1)
    log_decay = -LRU_C * _softplus(-lam_ref[0])
    for b in range(n_batch):
        cv = cv_ref[b]
        cv16 = cv.astype(BF16)
        r = _sigmoid(_dot(cv16, wa_ref[0]) + ba_ref[0])
        gi = _sigmoid(_dot(cv16, wi_ref[0]) + bi_ref[0])
        log_a = r * log_decay
        a = jnp.exp(log_a)
        a_buf[b] = a
        x_buf[b] = jnp.sqrt(1.0 - a * a) * (gi * cv)

    @pl.when(i == 0)
    def _():
        st_ref[...] = jnp.zeros_like(st_ref)

    def scan(reverse):
        n_groups = chunk // SUBLANES

        def group(gi, hs):
            base = pl.multiple_of((n_groups - 1 - gi if reverse else gi) * SUBLANES, SUBLANES)
            hs = list(hs)
            for k in range(SUBLANES):
                r = base + (SUBLANES - 1 - k if reverse else k)
                for b in range(n_batch):
                    hs[b] = a_buf[b, pl.ds(r, 1), :] * hs[b] + x_buf[b, pl.ds(r, 1), :]
                    o_ref[0, b, pl.ds(r, 1), :] = hs[b]
            return tuple(hs)

        hs = lax.fori_loop(0, n_groups, group, tuple(st_ref[b, 0:1, :] for b in range(n_batch)))
        for b in range(n_batch):
            st_ref[b, 0:1, :] = hs[b]

    @pl.when(d == 0)
    def _():
        scan(False)

    @pl.when(d == 1)
    def _():
        scan(True)


def _lru_chunk(d, i, n_lat):
    return jnp.where(i == 0, n_lat, jnp.where(d == 0, i - 1, n_lat - i))


def _lru_scan(cv, w_a, b_a, w_i, b_i, lam, *, seq, ctx):
    b, lt, w = cv.shape
    chunk = ctx
    n_lat = seq // chunk
    n_chunks = lt // chunk
    cur = lambda d, i: (0, _lru_chunk(d, i, n_lat), 0)
    per_dir = lambda a: pl.BlockSpec((1,) + a.shape[1:], lambda d, i: (d,) + (0,) * (a.ndim - 1))
    body = functools.partial(_lru_body, chunk=chunk, n_batch=b)
    return pl.pallas_call(
        body,
        grid=(2, n_chunks),
        in_specs=[pl.BlockSpec((b, chunk, w), cur),
                  per_dir(w_a), per_dir(b_a), per_dir(w_i), per_dir(b_i), per_dir(lam)],
        out_specs=pl.BlockSpec((1, b, chunk, w), lambda d, i: (d, 0, _lru_chunk(d, i, n_lat), 0)),
        out_shape=jax.ShapeDtypeStruct((2, b, lt, w), F32),
        scratch_shapes=[pltpu.VMEM((b, chunk, w), F32), pltpu.VMEM((b, chunk, w), F32), pltpu.VMEM((b, 8, w), F32)],
        compiler_params=_params(("arbitrary", "arbitrary")),
        name="lru_scan",
    )(cv, w_a, b_a, w_i, b_i, lam)


def _mla_body(q_ref, k_ref, v_ref, o_ref, m_ref, acc_ref, *, tk, nk):
    m_ref[...] = jnp.full(m_ref.shape, NEG_BIAS, F32)
    acc_ref[...] = jnp.zeros(acc_ref.shape, F32)

    def chunk(ci, carry):
        start = pl.multiple_of(ci * tk, tk)
        for h in range(MLA_HEADS):
            cols = slice(h * HEAD_PAD, (h + 1) * HEAD_PAD)
            s = _dot_nt(q_ref[0, :, cols], k_ref[0, pl.ds(start, tk), cols])
            blocks = [s[:, j * LANES:(j + 1) * LANES] for j in range(tk // LANES)]
            m_old = m_ref[h]
            m_new = jnp.maximum(m_old, jnp.max(functools.reduce(jnp.maximum, blocks), axis=-1, keepdims=True))
            p = jnp.concatenate([jnp.exp((blk - m_new).astype(BF16)) for blk in blocks], axis=-1)
            acc_ref[h] = jnp.exp(m_old - m_new) * acc_ref[h] + _dot(p, v_ref[0, pl.ds(start, tk), cols])
            m_ref[h] = m_new
        return carry

    lax.fori_loop(0, nk, chunk, 0)
    outs = [acc_ref[h][:, :V_HEAD] / acc_ref[h][:, V_HEAD:V_HEAD + 1] for h in range(MLA_HEADS)]
    o_ref[0] = jnp.concatenate(outs, axis=-1).astype(o_ref.dtype)


def _mla_attention(mq, mk, mv, *, tq, q_block0, n_q, k_block, k_block0, tk):
    b, lt, nw = mq.shape
    body = functools.partial(_mla_body, tk=tk, nk=k_block // tk)
    return pl.pallas_call(
        body,
        grid=(b, n_q),
        in_specs=[pl.BlockSpec((1, tq, nw), lambda bi, i: (bi, q_block0 + i, 0)),
                  pl.BlockSpec((1, k_block, nw), lambda bi, i: (bi, k_block0, 0), pipeline_mode=pl.Buffered(1)),
                  pl.BlockSpec((1, k_block, nw), lambda bi, i: (bi, k_block0, 0), pipeline_mode=pl.Buffered(1))],
        out_specs=pl.BlockSpec((1, tq, MLA_W), lambda bi, i: (bi, i, 0)),
        out_shape=jax.ShapeDtypeStruct((b, n_q * tq, MLA_W), BF16),
        scratch_shapes=[pltpu.VMEM((MLA_HEADS, tq, LANES), F32), pltpu.VMEM((MLA_HEADS, tq, HEAD_PAD), F32)],
        compiler_params=_params(("parallel", "arbitrary")),
        name="mla_attention",
    )(mq, mk, mv)


def _route(logits, rb):
    sel = _sigmoid(logits) + rb
    sc = [sel[j:j + 1, :] for j in range(N_EXPERTS)]
    gscore = []
    for g in range(N_GROUPS):
        a, b, c, d = sc[4 * g:4 * g + 4]
        hi1, lo1 = jnp.maximum(a, b), jnp.minimum(a, b)
        hi2, lo2 = jnp.maximum(c, d), jnp.minimum(c, d)
        gscore.append(jnp.maximum(hi1, hi2) + jnp.maximum(jnp.minimum(hi1, hi2), jnp.maximum(lo1, lo2)))
    best = jnp.zeros_like(gscore[0], dtype=jnp.int32)
    bval = gscore[0]
    for g in range(1, N_GROUPS):
        upd = gscore[g] > bval
        best = jnp.where(upd, g, best)
        bval = jnp.where(upd, gscore[g], bval)

    def pick(cols, j):
        out = cols[j]
        for g in range(1, N_GROUPS):
            out = jnp.where(best == g, cols[4 * g + j], out)
        return out

    v = [pick(sc, j) for j in range(EXPERTS_PER_GROUP)]
    code = jnp.zeros_like(best)
    for i in range(EXPERTS_PER_GROUP):
        rank = jnp.zeros_like(best)
        for j in range(EXPERTS_PER_GROUP):
            if j == i:
                continue
            beats = (v[j] > v[i]) if j > i else (v[j] >= v[i])
            rank = rank + beats.astype(jnp.int32)
        code = code + jnp.where(rank < 2, 1 << i, 0)
    pair = jnp.zeros_like(best)
    for p, (lo, hi) in enumerate(_PAIRS):
        pair = jnp.where(code == (1 << lo) + (1 << hi), p, pair)
    return best * N_PAIRS + pair


def _proj_out_body(ya_ref, hf_ref, hr_ref, gg_ref, yc_ref, x_ref, wa_ref, wb_ref, wc_ref, g1_ref, g2_ref,
                   gate_s_ref, gate_c_ref, shift_s_ref, shift_c_ref, scale_s_ref, scale_c_ref, rwt_ref, rb_ref,
                   xo_ref, h2_ref, cls_ref, *, rt):
    yb = (hf_ref[0, 0] + hr_ref[0, 0]) * gg_ref[...].astype(F32)
    y = _dot(ya_ref[...], wa_ref[...]) + _dot(yb.astype(BF16), wb_ref[...]) + _dot(yc_ref[...], wc_ref[...])
    ga_t, ga_b = rt.mod_rows(gate_s_ref, gate_c_ref)
    sh_t, sh_b = rt.mod_rows(shift_s_ref, shift_c_ref)
    sc_t, sc_b = rt.mod_rows(scale_s_ref, scale_c_ref)
    x = x_ref[...] + rt.by_rows(_rms(y, g1_ref[...]), lambda v: ga_t * v, lambda v: ga_b * v)
    xo_ref[...] = x
    h2 = rt.by_rows(_rms(x, g2_ref[...]), lambda v: v * (1.0 + sc_t) + sh_t,
                    lambda v: v * (1.0 + sc_b) + sh_b).astype(BF16)
    _store_token_tiles(h2_ref, h2.astype(F32))
    cls_ref[0] = _route(_dot_nt(rwt_ref[...], h2), rb_ref[...])


def _proj_out(ya, hscan, gg, yc, x, w_a, w_b, w_c, g1, g2, gate, shift, scale, rw, rb, *, rt):
    t, d = x.shape
    tm = rt.tm
    nt = t // tm
    row = lambda n: pl.BlockSpec((tm, n), lambda i: (i, 0))
    full = lambda a: pl.BlockSpec(a.shape, lambda i: (0,) * a.ndim)
    scan_spec = lambda dd: pl.BlockSpec((1, 1, tm, LRU_W),
                                        lambda i: (dd, i // rt.tiles_per_sample, i % rt.tiles_per_sample, 0))
    return pl.pallas_call(
        functools.partial(_proj_out_body, rt=rt),
        grid=(nt,),
        in_specs=[row(NA_W), scan_spec(0), scan_spec(1), row(LRU_W), row(MLA_W), row(d),
                  full(w_a), full(w_b), full(w_c), full(g1), full(g2),
                  *rt.mod_specs(d), *rt.mod_specs(d), *rt.mod_specs(d), full(rw), full(rb)],
        out_specs=[row(d), pl.BlockSpec((tm * SUBLANES, LANES), lambda i: (i, 0)),
                   pl.BlockSpec((1, 1, tm), lambda i: (i, 0, 0))],
        out_shape=[jax.ShapeDtypeStruct((t, d), F32),
                   jax.ShapeDtypeStruct((t * SUBLANES, LANES), F32),
                   jax.ShapeDtypeStruct((nt, 1, tm), jnp.int32)],
        compiler_params=_params(("parallel",)),
        name="proj_out",
    )(ya, hscan, hscan, gg, yc, x, w_a, w_b, w_c, g1, g2, gate, gate, shift, shift, scale, scale, rw, rb)


def _store_token_tiles(ref, v, token0=0):
    n = v.shape[0]
    for s in range(SUBLANES):
        ref[pl.ds(token0 * SUBLANES + s, n, stride=SUBLANES), :] = v[:, s * LANES:(s + 1) * LANES]


def _load_token_tiles(ref, n, token0=0):
    return jnp.concatenate([ref[pl.ds(token0 * SUBLANES + s, n, stride=SUBLANES), :] for s in range(SUBLANES)],
                           axis=1)


def _gather_token_tiles(idx_ref, base, src_ref, buf_ref, n):
    def copy(r, carry):
        j = idx_ref[base + r]
        buf_ref[pl.ds(pl.multiple_of(r * SUBLANES, SUBLANES), SUBLANES), :] = (
            src_ref[0, pl.ds(pl.multiple_of(j * SUBLANES, SUBLANES), SUBLANES), :])
        return carry

    lax.fori_loop(0, n, copy, 0, unroll=8)


def _dispatch_body(pos_ref, h_ref, o_ref, buf_ref, src_ref, *, tokens_per_sample):
    @pl.when(pl.program_id(1) == 0)
    def _():
        base = pl.program_id(0) * tokens_per_sample

        def invert(t, carry):
            src_ref[pos_ref[base + t]] = t
            return carry

        lax.fori_loop(0, tokens_per_sample, invert, 0, unroll=8)

    tm = o_ref.shape[0]
    _gather_token_tiles(src_ref, pl.program_id(1) * tm, h_ref, buf_ref, tm)
    o_ref[...] = _load_token_tiles(buf_ref, tm).astype(o_ref.dtype)


def _dispatch(pos, h2t, *, n_batch, tokens_per_sample, tm):
    n_tiles = tokens_per_sample // tm
    return pl.pallas_call(
        functools.partial(_dispatch_body, tokens_per_sample=tokens_per_sample),
        grid_spec=pltpu.PrefetchScalarGridSpec(
            num_scalar_prefetch=1,
            grid=(n_batch, n_tiles),
            in_specs=[pl.BlockSpec((1, tokens_per_sample * SUBLANES, LANES), lambda b, i, pos: (b, 0, 0),
                                   pipeline_mode=pl.Buffered(1))],
            out_specs=pl.BlockSpec((tm, D_MODEL), lambda b, i, pos: (b * n_tiles + i, 0)),
            scratch_shapes=[pltpu.VMEM((tm * SUBLANES, LANES), F32), pltpu.SMEM((tokens_per_sample,), jnp.int32)]),
        out_shape=jax.ShapeDtypeStruct((n_batch * tokens_per_sample, D_MODEL), BF16),
        compiler_params=_params(("arbitrary", "arbitrary")),
        name="moe_dispatch",
    )(pos, h2t.reshape(n_batch, tokens_per_sample * SUBLANES, LANES))


def _moe_body(tile_ref, elo_ref, ehi_ref, lo_ref, hi_ref, valid_ref, first_ref,
              x_ref, rw_ref, wg0_ref, wu0_ref, wd0_ref, wg1_ref, wu1_ref, wd1_ref, o_ref):
    k = pl.program_id(0)

    @pl.when(valid_ref[k] == 1)
    def _():
        @pl.when(first_ref[k] == 1)
        def _():
            o_ref[...] = jnp.zeros(o_ref.shape, o_ref.dtype)

        def visit(row0, n):
            x = x_ref[row0:row0 + n, :]
            aff = _sigmoid(_dot(x, rw_ref[...]))
            lane = lax.broadcasted_iota(jnp.int32, aff.shape, 1)
            a_lo = jnp.sum(jnp.where(lane == elo_ref[k], aff, 0.0), axis=-1, keepdims=True)
            a_hi = jnp.sum(jnp.where(lane == ehi_ref[k], aff, 0.0), axis=-1, keepdims=True)
            den = a_lo + a_hi
            row = row0 + lax.broadcasted_iota(jnp.int32, (n, 1), 0)
            mine = jnp.logical_and(row >= lo_ref[k], row < hi_ref[k])

            def add_expert(a_e, wg, wu, wd):
                a = _dot(x, wg[0])
                he = (a * _sigmoid(a)) * _dot(x, wu[0])
                y = jnp.where(mine, a_e / den, 0.0) * _dot(he.astype(BF16), wd[0])
                _store_token_tiles(o_ref, _load_token_tiles(o_ref, n, row0) + y, row0)

            add_expert(a_lo, wg0_ref, wu0_ref, wd0_ref)
            add_expert(a_hi, wg1_ref, wu1_ref, wd1_ref)

        half = TM // 2
        in_top = hi_ref[k] <= half
        in_bottom = lo_ref[k] >= half

        @pl.when(in_top)
        def _():
            visit(0, half)

        @pl.when(in_bottom)
        def _():
            visit(half, half)

        @pl.when(jnp.logical_not(jnp.logical_or(in_top, in_bottom)))
        def _():
            visit(0, TM)


def _moe_routed(items, xs, rw, w_gate, w_up, w_down, *, expert_base):
    t, d = xs.shape
    _, _, de = w_gate.shape
    n_items = items[0].shape[0]
    xspec = pl.BlockSpec((TM, d), lambda k, tile, *_: (tile[k], 0))
    w_in_spec = lambda which: pl.BlockSpec(
        (1, d, de), lambda k, tile, elo, ehi, *_: (expert_base + (elo, ehi)[which][k], 0, 0))
    w_out_spec = lambda which: pl.BlockSpec(
        (1, de, d), lambda k, tile, elo, ehi, *_: (expert_base + (elo, ehi)[which][k], 0, 0))
    return pl.pallas_call(
        _moe_body,
        grid_spec=pltpu.PrefetchScalarGridSpec(
            num_scalar_prefetch=len(items),
            grid=(n_items,),
            in_specs=[xspec, pl.BlockSpec(rw.shape, lambda k, *_: (0, 0)),
                      w_in_spec(0), w_in_spec(0), w_out_spec(0), w_in_spec(1), w_in_spec(1), w_out_spec(1)],
            out_specs=pl.BlockSpec((TM * SUBLANES, LANES), lambda k, tile, *_: (tile[k], 0))),
        out_shape=jax.ShapeDtypeStruct((t * SUBLANES, LANES), F32),
        compiler_params=_params(("arbitrary",)),
        name="moe_routed",
    )(*items, xs, rw, w_gate, w_up, w_down, w_gate, w_up, w_down)


def _combine_body(pos_ref, y_ref, x_ref, gate_ref, g_ref, o_ref, buf_ref, *, tokens_per_sample):
    base = pl.program_id(0) * tokens_per_sample + pl.program_id(1) * TM
    _gather_token_tiles(pos_ref, base, y_ref, buf_ref, TM)
    o_ref[...] = x_ref[...] + gate_ref[0] * _rms(_load_token_tiles(buf_ref, TM), g_ref[...])


def _combine(pos, ys, x, gate, g, *, n_batch, tokens_per_sample, n_lat_tiles, n_out_tiles):
    n_tiles = tokens_per_sample // TM
    d = x.shape[1]
    return pl.pallas_call(
        functools.partial(_combine_body, tokens_per_sample=tokens_per_sample),
        grid_spec=pltpu.PrefetchScalarGridSpec(
            num_scalar_prefetch=1,
            grid=(n_batch, n_out_tiles),
            in_specs=[pl.BlockSpec((1, tokens_per_sample * SUBLANES, LANES), lambda b, i, pos: (b, 0, 0),
                                   pipeline_mode=pl.Buffered(1)),
                      pl.BlockSpec((TM, d), lambda b, i, pos: (b * n_tiles + i, 0)),
                      pl.BlockSpec((1, 1, d), lambda b, i, pos: (jnp.where(i >= n_lat_tiles, n_batch, b), 0, 0)),
                      pl.BlockSpec(g.shape, lambda b, i, pos: (0, 0))],
            out_specs=pl.BlockSpec((TM, d), lambda b, i, pos: (b * n_out_tiles + i, 0)),
            scratch_shapes=[pltpu.VMEM((TM * SUBLANES, LANES), F32)]),
        out_shape=jax.ShapeDtypeStruct((n_batch * n_out_tiles * TM, d), F32),
        compiler_params=_params(("arbitrary", "arbitrary")),
        name="moe_combine",
    )(pos, ys.reshape(n_batch, tokens_per_sample * SUBLANES, LANES), x, gate, g)


def _routing_plan(cls, *, n_tiles):
    b, l = cls.shape
    n_items = n_tiles + N_CLASSES - 1
    onehot = (cls[..., None] == jnp.arange(N_CLASSES, dtype=jnp.int32)).astype(jnp.int32)
    csum = jnp.cumsum(onehot, axis=1)
    counts = csum[:, -1]
    starts = jnp.cumsum(counts, axis=1) - counts
    pos = jnp.sum(onehot * (starts[:, None, :] + csum - 1), axis=-1)
    bi = jnp.arange(b, dtype=jnp.int32)[:, None]

    first_tile = starts // TM
    last_tile = (starts + counts - 1) // TM
    m = jnp.where(counts > 0, last_tile - first_tile + 1, 0)
    o_end = jnp.cumsum(m, axis=1)
    o_start = o_end - m
    total = o_end[:, -1:]
    k = jnp.arange(n_items, dtype=jnp.int32)[None, :]
    kk = jnp.minimum(k, total - 1)
    c_k = jnp.sum((kk[:, :, None] >= o_end[:, None, :]).astype(jnp.int32), axis=-1)
    take = lambda a: jnp.take_along_axis(a, c_k, axis=1)
    tile_k = take(first_tile) + kk - take(o_start)
    valid = (k < total).astype(jnp.int32)
    lo = jnp.maximum(take(starts), tile_k * TM) - tile_k * TM
    hi = jnp.minimum(take(starts + counts), (tile_k + 1) * TM) - tile_k * TM
    prev_tile = jnp.concatenate([jnp.full((b, 1), -1, jnp.int32), tile_k[:, :-1]], axis=1)
    first = (tile_k != prev_tile).astype(jnp.int32)
    pair_lo = jnp.asarray([p[0] for p in _PAIRS], jnp.int32)
    pair_hi = jnp.asarray([p[1] for p in _PAIRS], jnp.int32)
    e_lo = (c_k // N_PAIRS) * EXPERTS_PER_GROUP + pair_lo[c_k % N_PAIRS]
    e_hi = (c_k // N_PAIRS) * EXPERTS_PER_GROUP + pair_hi[c_k % N_PAIRS]
    tile_g = tile_k + bi * n_tiles
    items = tuple(a.reshape(-1).astype(jnp.int32) for a in (tile_g, e_lo, e_hi, lo, hi, valid, first))
    return pos.reshape(-1).astype(jnp.int32), items


def _rope_tables(seq, ctx):
    t = np.arange(seq)
    axis_dim = QK_ROPE // 2
    inv = np.float32(ROPE_BASE) ** (-np.arange(0, axis_dim, 2, dtype=np.float32) / np.float32(axis_dim))
    ang_r = (t // GRID_W).astype(np.float32)[:, None] * inv
    ang_c = (t % GRID_W).astype(np.float32)[:, None] * inv
    cos = jnp.asarray(np.concatenate([np.cos(ang_r), np.cos(ang_r), np.cos(ang_c), np.cos(ang_c)], axis=-1))
    sin = jnp.asarray(np.concatenate([-np.sin(ang_r), np.sin(ang_r), -np.sin(ang_c), np.sin(ang_c)], axis=-1))
    pad = HEAD_PAD - QK_NOPE - QK_ROPE
    cos = jnp.concatenate([jnp.ones((seq, QK_NOPE), F32), cos, jnp.ones((seq, pad), F32)], axis=-1)
    sin = jnp.concatenate([jnp.zeros((seq, QK_NOPE), F32), sin, jnp.zeros((seq, pad), F32)], axis=-1)
    cos = jnp.concatenate([cos, jnp.ones((ctx, HEAD_PAD), F32)], axis=0)
    sin = jnp.concatenate([sin, jnp.zeros((ctx, HEAD_PAD), F32)], axis=0)
    return cos, sin


_ROPE_SWAP = np.concatenate([np.arange(8, 16), np.arange(0, 8), np.arange(24, 32), np.arange(16, 24)])


def _layout_w_in(w_in):
    d = w_in.shape[0]
    z = lambda n: jnp.zeros((d, n), w_in.dtype)
    kr = w_in[:, _C_KR:_C_KR + QK_ROPE]
    pad = HEAD_PAD - QK_NOPE - QK_ROPE
    return jnp.concatenate([w_in[:, :_C_KR], z(QK_NOPE), kr, z(pad), z(QK_NOPE), kr[:, _ROPE_SWAP], z(pad)],
                           axis=1).astype(BF16)


def _layout_w_q(w_q):
    r = w_q.shape[0]
    wh = w_q.reshape(r, MLA_HEADS, QK_NOPE + QK_ROPE)
    pad = HEAD_PAD - QK_NOPE - QK_ROPE
    full = jnp.concatenate([wh, jnp.zeros((r, MLA_HEADS, pad), w_q.dtype)], axis=-1)
    swap = jnp.concatenate([jnp.zeros((r, MLA_HEADS, QK_NOPE), w_q.dtype), wh[:, :, QK_NOPE + _ROPE_SWAP],
                            jnp.zeros((r, MLA_HEADS, pad), w_q.dtype)], axis=-1)
    return jnp.concatenate([full.reshape(r, -1), swap.reshape(r, -1)], axis=1).astype(BF16)


def _layout_w_kv(w_kv):
    r = w_kv.shape[0]
    wh = w_kv.reshape(r, MLA_HEADS, QK_NOPE + V_HEAD)
    zk = jnp.zeros((r, MLA_HEADS, HEAD_PAD - QK_NOPE), w_kv.dtype)
    zv = jnp.zeros((r, MLA_HEADS, HEAD_PAD - V_HEAD), w_kv.dtype)
    k = jnp.concatenate([wh[:, :, :QK_NOPE], zk], axis=-1).reshape(r, -1)
    v = jnp.concatenate([wh[:, :, QK_NOPE:], zv], axis=-1).reshape(r, -1)
    return jnp.concatenate([k, v], axis=1).astype(BF16)


def _block_diag(w):
    nd, nb, k, _ = w.shape
    eye = jnp.eye(nb, dtype=w.dtype)
    return (w[:, :, :, None, :] * eye[None, :, None, :, None]).reshape(nd, nb * k, nb * k).astype(BF16)


def _pick_chunk(n, candidates):
    for c in candidates:
        if n % c == 0:
            return c
    raise ValueError(f"no chunk size for {n}")


def kernel(x, c, ctx, c_ctx, w_ada, b_ada, g_norm, w_in, w_out, na_rpb, conv_w, conv_b, lru_w_a, lru_b_a, lru_w_i,
           lru_b_i, lru_lam, mla_g_q, mla_w_q, mla_g_kv, mla_w_kv, router_w, router_b, exp_w_gate, exp_w_up,
           exp_w_down):
    b, seq, d = x.shape
    n_ctx = ctx.shape[1]
    depth = w_ada.shape[0]
    lt = seq + n_ctx
    assert d == D_MODEL == SUBLANES * LANES and n_ctx == TM and seq % TM == 0 and seq // GRID_W >= WIN_ROWS
    assert b + 1 <= 8
    tiles_per_sample = lt // TM
    n_lat_tiles = seq // TM
    rt = _RowTiling(seq, lt, b, _pick_chunk(lt, (PROJ_TM, TM)))

    xa = jnp.concatenate([x, ctx], axis=1).reshape(b * lt, d)

    c_rows = jnp.concatenate([c, c_ctx[None, :], jnp.zeros((8 - b - 1, d), F32)], axis=0)
    mod = _ada_tables(c_rows, w_ada, b_ada).reshape(depth, 8, N_MOD, 1, d)

    cos_t, sin_t = _rope_tables(seq, n_ctx)
    ones_row = jnp.zeros((MLA_HEADS, HEAD_PAD), F32).at[:, V_HEAD].set(1.0).reshape(1, -1)
    rw = jnp.concatenate([router_w, jnp.zeros((d, LANES - N_EXPERTS), F32)], axis=1).astype(BF16)
    wg_all, wu_all, wd_all = (w.astype(BF16).reshape((depth * N_EXPERTS,) + w.shape[2:])
                              for w in (exp_w_gate, exp_w_up, exp_w_down))
    rwt = router_w.T.astype(BF16)
    rb = router_b[:, None]
    tk = _pick_chunk(lt, (768, 512, 256))
    tq = _pick_chunk(seq, (2048, 1024, 512, 256))

    for l in range(depth):
        last = l == depth - 1
        m = lambda k: mod[l, :b + 1, k]
        g = g_norm[l]
        qkv, u, gg, mq, mk, mv = _proj_in(
            xa, m(0), m(1), g[0:1], _layout_w_in(w_in[l]), conv_w[l], conv_b[l][None, :], mla_g_q[l][None, :],
            _layout_w_q(mla_w_q[l]), mla_g_kv[l][None, :], _layout_w_kv(mla_w_kv[l]), cos_t, sin_t, ones_row,
            rt=rt, seq=seq, tokens_per_sample=lt)

        ya = _na_attention(qkv.reshape(b, lt, -1), _na_bias_tables(na_rpb[l]), seq=seq, ctx=n_ctx)
        hscan = _lru_scan(u.reshape(b, lt, -1), _block_diag(lru_w_a[l]), lru_b_a[l][:, None, :],
                          _block_diag(lru_w_i[l]), lru_b_i[l][:, None, :], lru_lam[l][:, None, :], seq=seq, ctx=n_ctx)
        mq3, mk3, mv3 = (a.reshape(b, lt, -1) for a in (mq, mk, mv))
        yc = _mla_attention(mq3, mk3, mv3, tq=tq, q_block0=0, n_q=seq // tq, k_block=lt, k_block0=0, tk=tk)
        if last:
            yc = jnp.concatenate([yc, jnp.zeros((b, n_ctx, MLA_W), BF16)], axis=1)
        else:
            yc_ctx = _mla_attention(mq3, mk3, mv3, tq=n_ctx, q_block0=seq // n_ctx, n_q=1, k_block=n_ctx,
                                    k_block0=seq // n_ctx, tk=n_ctx)
            yc = jnp.concatenate([yc, yc_ctx], axis=1)

        wo = w_out[l].astype(BF16)
        xa, h2t, cls = _proj_out(
            ya.reshape(b * lt, -1), hscan, gg, yc.reshape(b * lt, -1), xa,
            wo[:NA_W], wo[NA_W:NA_W + LRU_W], wo[NA_W + LRU_W:], g[1:2], g[2:3], m(2), m(3), m(4), rwt, rb, rt=rt)

        pos, items = _routing_plan(cls.reshape(b, lt), n_tiles=tiles_per_sample)
        xs = _dispatch(pos, h2t, n_batch=b, tokens_per_sample=lt, tm=rt.tm)
        ys = _moe_routed(items, xs, rw, wg_all, wu_all, wd_all, expert_base=l * N_EXPERTS)
        xa = _combine(pos, ys, xa, m(5), g[3:4], n_batch=b, tokens_per_sample=lt, n_lat_tiles=n_lat_tiles,
                      n_out_tiles=n_lat_tiles if last else tiles_per_sample)

    return xa.reshape(b, seq, d)
```

```python
import functools

import numpy as np
import jax
import jax.numpy as jnp
from jax import lax
from jax.experimental import pallas as pl
from jax.experimental.pallas import tpu as pltpu

F32 = jnp.float32
BF16 = jnp.bfloat16

D_MODEL = 1024
GRID_W = 64
NA_HEADS = 4
NA_HEAD_DIM = 64
NA_W = NA_HEADS * NA_HEAD_DIM
WIN_ROWS = 8
WIN_COLS = 16
LRU_W = 512
LRU_BLOCKS = 8
CONV_W = 4
LRU_C = 8.0
MLA_HEADS = 4
Q_LORA = 256
KV_LORA = 128
QK_NOPE = 64
QK_ROPE = 32
V_HEAD = 64
MLA_W = MLA_HEADS * V_HEAD
ROPE_BASE = 10000.0
N_EXPERTS = 16
N_GROUPS = 4
EXPERTS_PER_GROUP = 4
D_EXPERT = 512
RMS_EPS = 1e-6
N_MOD = 6

_PAIRS = ((0, 1), (0, 2), (0, 3), (1, 3), (1, 2), (3, 2))
N_PAIRS = len(_PAIRS)
N_CLASSES = N_GROUPS * N_PAIRS

LANES = 128
SUBLANES = 8
TM = 256
PROJ_TM = 768
HEAD_PAD = 128
NEG_BIAS = -1e30
VMEM_LIMIT = 56 * 1024 * 1024

_C_QA, _C_KA, _C_VA = 0, 256, 512
_C_U = 768
_C_GATE = 1280
_C_CQ = 1792
_C_CKV = 2048
_C_KR = 2176
_C_KRP = 2304
_IN_COLS = 2432


def _params(sem, vmem=VMEM_LIMIT):
    return pltpu.CompilerParams(dimension_semantics=sem, vmem_limit_bytes=vmem)


def _sigmoid(v):
    return 1.0 / (1.0 + jnp.exp(-v))


def _rms(v, g):
    return v * lax.rsqrt(jnp.mean(v * v, axis=-1, keepdims=True) + RMS_EPS) * g


def _dot(a, b):
    return jnp.dot(a, b, preferred_element_type=F32)


def _dot_nt(a, b):
    return lax.dot_general(a, b, (((1,), (1,)), ((), ())), preferred_element_type=F32)


def _ada_body(c_ref, w_ref, b_ref, o_ref):
    c = c_ref[...]
    sc = c * _sigmoid(c)
    o_ref[0] = _dot(sc.astype(BF16), w_ref[0].astype(BF16)) + b_ref[0]


def _ada_tables(c_rows, w_ada, b_ada):
    depth, d, n = w_ada.shape
    tn = 1536
    return pl.pallas_call(
        _ada_body,
        grid=(depth, n // tn),
        in_specs=[pl.BlockSpec((8, d), lambda l, j: (0, 0)),
                  pl.BlockSpec((1, d, tn), lambda l, j: (l, 0, j)),
                  pl.BlockSpec((1, 1, tn), lambda l, j: (l, 0, j))],
        out_specs=pl.BlockSpec((1, 8, tn), lambda l, j: (l, 0, j)),
        out_shape=jax.ShapeDtypeStruct((depth, 8, n), F32),
        compiler_params=_params(("parallel", "parallel")),
        name="ada_tables",
    )(c_rows, w_ada, b_ada.reshape(depth, 1, n))


class _RowTiling:
    def __init__(self, seq, tokens_per_sample, n_batch, tm):
        assert tokens_per_sample % tm == 0
        self.tm = tm
        self.n_batch = n_batch
        self.tiles_per_sample = tokens_per_sample // tm
        self.ctx_tile = seq // tm
        self.split = seq % tm

    def mod_specs(self, d):
        return [pl.BlockSpec((1, 1, d), lambda i: (i // self.tiles_per_sample, 0, 0)),
                pl.BlockSpec((1, 1, d), lambda i: (self.n_batch, 0, 0))]

    def mod_rows(self, sample_ref, ctx_ref):
        j = pl.program_id(0) % self.tiles_per_sample
        top = jnp.where(j <= self.ctx_tile, sample_ref[0], ctx_ref[0])
        bot = jnp.where(j < self.ctx_tile, sample_ref[0], ctx_ref[0])
        return top, bot

    def by_rows(self, v, f_top, f_bot):
        if self.split == 0:
            return f_bot(v)
        return jnp.concatenate([f_top(v[:self.split]), f_bot(v[self.split:])], axis=0)


def _proj_in_body(x_ref, shift_s_ref, shift_c_ref, scale_s_ref, scale_c_ref, g_ref, w_ref, gq_ref, wq_ref,
                  gkv_ref, wkv_ref, cos_ref, sin_ref, ones_ref,
                  qkv_ref, u_ref, gg_ref, mq_ref, mk_ref, mv_ref, *, mla_scale, rt):
    x = x_ref[...]
    sh_t, sh_b = rt.mod_rows(shift_s_ref, shift_c_ref)
    sc_t, sc_b = rt.mod_rows(scale_s_ref, scale_c_ref)
    h = rt.by_rows(_rms(x, g_ref[...]), lambda v: v * (1.0 + sc_t) + sh_t, lambda v: v * (1.0 + sc_b) + sh_b)
    z = _dot(h.astype(BF16), w_ref[...])
    qkv_ref[:, 0:NA_W] = (z[:, _C_QA:_C_KA] * (NA_HEAD_DIM ** -0.5)).astype(BF16)
    qkv_ref[:, NA_W:3 * NA_W] = z[:, _C_KA:_C_U].astype(BF16)
    u_ref[...] = z[:, _C_U:_C_GATE]
    gg_ref[...] = jax.nn.gelu(z[:, _C_GATE:_C_CQ]).astype(BF16)

    cos = cos_ref[...]
    sin = sin_ref[...]
    cos4 = jnp.concatenate([cos] * MLA_HEADS, axis=-1)
    sin4 = jnp.concatenate([sin] * MLA_HEADS, axis=-1)
    nw = MLA_HEADS * HEAD_PAD

    nq = _rms(z[:, _C_CQ:_C_CKV], gq_ref[...])
    q2 = _dot(nq.astype(BF16), wq_ref[...])
    mq_ref[...] = ((q2[:, :nw] * cos4 + q2[:, nw:] * sin4) * mla_scale).astype(BF16)

    nkv = _rms(z[:, _C_CKV:_C_KR], gkv_ref[...])
    kv2 = _dot(nkv.astype(BF16), wkv_ref[...])
    k_rope = z[:, _C_KR:_C_KRP] * cos + z[:, _C_KRP:_IN_COLS] * sin
    mk_ref[...] = (kv2[:, :nw] + jnp.concatenate([k_rope] * MLA_HEADS, axis=-1)).astype(BF16)
    mv_ref[...] = (kv2[:, nw:] + ones_ref[...]).astype(BF16)


def _proj_in(x, shift, scale, g, w_big, gq, wq2, gkv, wkv2, cos_t, sin_t, ones_row, *, rt):
    t, d = x.shape
    tm = rt.tm
    nt = t // tm
    nw = MLA_HEADS * HEAD_PAD
    row = lambda n: pl.BlockSpec((tm, n), lambda i: (i, 0))
    full = lambda a: pl.BlockSpec(a.shape, lambda i: (0,) * a.ndim)
    tabspec = pl.BlockSpec((tm, HEAD_PAD), lambda i: (i % rt.tiles_per_sample, 0))
    mla_scale = (QK_NOPE + QK_ROPE) ** -0.5
    return pl.pallas_call(
        functools.partial(_proj_in_body, mla_scale=mla_scale, rt=rt),
        grid=(nt,),
        in_specs=[row(d), *rt.mod_specs(d), *rt.mod_specs(d), full(g), full(w_big), full(gq), full(wq2), full(gkv),
                  full(wkv2), tabspec, tabspec, full(ones_row)],
        out_specs=[row(3 * NA_W), row(LRU_W), row(LRU_W), row(nw), row(nw), row(nw)],
        out_shape=[jax.ShapeDtypeStruct((t, 3 * NA_W), BF16),
                   jax.ShapeDtypeStruct((t, LRU_W), F32),
                   jax.ShapeDtypeStruct((t, LRU_W), BF16),
                   jax.ShapeDtypeStruct((t, nw), BF16),
                   jax.ShapeDtypeStruct((t, nw), BF16),
                   jax.ShapeDtypeStruct((t, nw), BF16)],
        compiler_params=_params(("parallel",)),
        name="proj_in",
    )(x, shift, shift, scale, scale, g, w_big, gq, wq2, gkv, wkv2, cos_t, sin_t, ones_row)


def _head_block_mask():
    r = lax.broadcasted_iota(jnp.int32, (NA_W, NA_W), 0) // NA_HEAD_DIM
    c = lax.broadcasted_iota(jnp.int32, (NA_W, NA_W), 1) // NA_HEAD_DIM
    return r == c


def _na_attend(q, parts, mask):
    qbig = jnp.where(mask, jnp.concatenate([q] * NA_HEADS, axis=0), jnp.zeros((), q.dtype))
    scores = []
    for k, _, bias in parts:
        s = _dot_nt(qbig, k)
        if bias is not None:
            s = s + bias
        scores.append(s)
    m = functools.reduce(jnp.maximum, [jnp.max(s, axis=-1, keepdims=True) for s in scores])
    ps = [jnp.exp(s - m) for s in scores]
    l = functools.reduce(jnp.add, [jnp.sum(p, axis=-1, keepdims=True) for p in ps])
    o = functools.reduce(jnp.add, [_dot(p.astype(BF16), v) for p, (_, v, _) in zip(ps, parts)])
    o = jnp.where(mask, o / l, 0.0)
    out = o[0:NA_HEAD_DIM]
    for h in range(1, NA_HEADS):
        out = out + o[h * NA_HEAD_DIM:(h + 1) * NA_HEAD_DIM]
    return out


def _na_body(q_ref, k_ref, v_ref, bias_ref, o_ref, *, rows, rows_per_step, n_lat_steps, seq, ctx):
    i = pl.program_id(1)
    mask = _head_block_mask()
    kc = k_ref[0, seq:seq + ctx, :]
    vc = v_ref[0, seq:seq + ctx, :]

    @pl.when(i < n_lat_steps)
    def _():
        for j in range(rows_per_step):
            r = i * rows_per_step + j
            rs = jnp.clip(r - WIN_ROWS // 2, 0, rows - WIN_ROWS)
            off = rs - r + (WIN_ROWS - 1)
            start = pl.multiple_of(rs * GRID_W, GRID_W)
            kw = k_ref[0, pl.ds(start, WIN_ROWS * GRID_W), :]
            vw = v_ref[0, pl.ds(start, WIN_ROWS * GRID_W), :]
            q = q_ref[0, j * GRID_W:(j + 1) * GRID_W, :]
            out = _na_attend(q, [(kw, vw, bias_ref[off]), (kc, vc, None)], mask)
            o_ref[0, j * GRID_W:(j + 1) * GRID_W, :] = out.astype(o_ref.dtype)

    @pl.when(i >= n_lat_steps)
    def _():
        for j in range(rows_per_step):
            q = q_ref[0, j * GRID_W:(j + 1) * GRID_W, :]
            out = _na_attend(q, [(kc, vc, None)], mask)
            o_ref[0, j * GRID_W:(j + 1) * GRID_W, :] = out.astype(o_ref.dtype)


def _na_attention(qkv, bias, *, seq, ctx):
    b, lt, _ = qkv.shape
    rows = seq // GRID_W
    rows_per_step = TM // GRID_W
    n_steps = lt // TM
    n_lat_steps = seq // TM
    body = functools.partial(_na_body, rows=rows, rows_per_step=rows_per_step, n_lat_steps=n_lat_steps,
                             seq=seq, ctx=ctx)
    return pl.pallas_call(
        body,
        grid=(b, n_steps),
        in_specs=[pl.BlockSpec((1, TM, NA_W), lambda bi, i: (bi, i, 0)),
                  pl.BlockSpec((1, lt, NA_W), lambda bi, i: (bi, 0, 1)),
                  pl.BlockSpec((1, lt, NA_W), lambda bi, i: (bi, 0, 2)),
                  pl.BlockSpec(bias.shape, lambda bi, i: (0, 0, 0))],
        out_specs=pl.BlockSpec((1, TM, NA_W), lambda bi, i: (bi, i, 0)),
        out_shape=jax.ShapeDtypeStruct((b, lt, NA_W), BF16),
        compiler_params=_params(("parallel", "arbitrary")),
        name="na_attention",
    )(qkv, qkv, qkv, bias)


def _na_bias_tables(rpb):
    q = np.arange(GRID_W)
    cs = np.clip(q - WIN_COLS // 2, 0, GRID_W - WIN_COLS)
    c = np.arange(GRID_W)
    inside = (c[None, :] >= cs[:, None]) & (c[None, :] < cs[:, None] + WIN_COLS)
    dc = c[None, :] - q[:, None] + (WIN_COLS - 1)
    place = ((np.arange(2 * WIN_COLS - 1)[:, None, None] == dc[None]) & inside[None]).astype(np.float32)
    win = jnp.stack([rpb[:, o:o + WIN_ROWS, :] for o in range(WIN_ROWS)]).astype(F32)
    vals = jnp.einsum('ohik,kqc->ohqic', win, jnp.asarray(place), precision=lax.Precision.HIGHEST)
    vals = jnp.where(inside[None, None, :, None, :], vals, NEG_BIAS)
    return vals.reshape(WIN_ROWS, NA_HEADS * GRID_W, WIN_ROWS * GRID_W)


def _softplus(v):
    return jnp.maximum(v, 0.0) + jnp.log1p(jnp.exp(-jnp.abs(v)))


def _lru_body(up_ref, uc_ref, un_ref, cw_ref, cb_ref, wa_ref, ba_ref, wi_ref, bi_ref, lam_ref,
              o_ref, a_buf, x_buf, st_ref, *, n_lat, chunk, n_batch):
    d = pl.program_id(0)
    i = pl.program_id(1)
    c = _lru_chunk(d, i, n_lat)
    prev_ok = jnp.logical_and(c >= 1, c < n_lat)
    next_ok = c <= n_lat - 2

    row = lax.broadcasted_iota(jnp.int32, (chunk, 1), 0)
    cw = cw_ref[...]
    log_decay = -LRU_C * _softplus(-lam_ref[0])
    for b in range(n_batch):
        ub = uc_ref[b]
        p_row = jnp.where(prev_ok, up_ref[b, 7:8, :], 0.0)
        n0 = jnp.where(next_ok, un_ref[b, 0:1, :], 0.0)
        n1 = jnp.where(next_ok, un_ref[b, 1:2, :], 0.0)
        um1 = jnp.where(row == 0, p_row, pltpu.roll(ub, 1, 0))
        up1 = jnp.where(row == chunk - 1, n0, pltpu.roll(ub, chunk - 1, 0))
        up2 = jnp.where(row == chunk - 1, n1, jnp.where(row == chunk - 2, n0, pltpu.roll(ub, chunk - 2, 0)))
        cv = um1 * cw[0:1] + cb_ref[...]
        cv = cv + ub * cw[1:2]
        cv = cv + up1 * cw[2:3]
        cv = cv + up2 * cw[3:4]
        cv16 = cv.astype(BF16)
        r = _sigmoid(_dot(cv16, wa_ref[0]) + ba_ref[0])
        gi = _sigmoid(_dot(cv16, wi_ref[0]) + bi_ref[0])
        log_a = r * log_decay
        a = jnp.exp(log_a)
        a_buf[b] = a
        x_buf[b] = jnp.sqrt(1.0 - a * a) * (gi * cv)

    @pl.when(i == 0)
    def _():
        st_ref[...] = jnp.zeros_like(st_ref)

    def scan(reverse):
        n_groups = chunk // SUBLANES

        def group(gi, hs):
            base = pl.multiple_of((n_groups - 1 - gi if reverse else gi) * SUBLANES, SUBLANES)
            hs = list(hs)
            for k in range(SUBLANES):
                r = base + (SUBLANES - 1 - k if reverse else k)
                for b in range(n_batch):
                    hs[b] = a_buf[b, pl.ds(r, 1), :] * hs[b] + x_buf[b, pl.ds(r, 1), :]
                    o_ref[0, b, pl.ds(r, 1), :] = hs[b]
            return tuple(hs)

        hs = lax.fori_loop(0, n_groups, group, tuple(st_ref[b, 0:1, :] for b in range(n_batch)))
        for b in range(n_batch):
            st_ref[b, 0:1, :] = hs[b]

    @pl.when(d == 0)
    def _():
        scan(False)

    @pl.when(d == 1)
    def _():
        scan(True)


def _lru_chunk(d, i, n_lat):
    return jnp.where(i == 0, n_lat, jnp.where(d == 0, i - 1, n_lat - i))


def _lru_scan(u, conv_w, conv_b, w_a, b_a, w_i, b_i, lam, *, seq, ctx):
    b, lt, w = u.shape
    chunk = ctx
    n_lat = seq // chunk
    n_chunks = lt // chunk
    sub = chunk // 8
    n_sub = lt // 8

    cur = lambda d, i: (0, _lru_chunk(d, i, n_lat), 0)
    prev = lambda d, i: (0, jnp.maximum(_lru_chunk(d, i, n_lat) * sub - 1, 0), 0)
    nxt = lambda d, i: (0, jnp.minimum((_lru_chunk(d, i, n_lat) + 1) * sub, n_sub - 1), 0)
    per_dir = lambda a: pl.BlockSpec((1,) + a.shape[1:], lambda d, i: (d,) + (0,) * (a.ndim - 1))
    full = lambda a: pl.BlockSpec(a.shape, lambda d, i: (0,) * a.ndim)
    body = functools.partial(_lru_body, n_lat=n_lat, chunk=chunk, n_batch=b)
    return pl.pallas_call(
        body,
        grid=(2, n_chunks),
        in_specs=[pl.BlockSpec((b, 8, w), prev), pl.BlockSpec((b, chunk, w), cur), pl.BlockSpec((b, 8, w), nxt),
                  full(conv_w), full(conv_b), per_dir(w_a), per_dir(b_a), per_dir(w_i), per_dir(b_i), per_dir(lam)],
        out_specs=pl.BlockSpec((1, b, chunk, w), lambda d, i: (d, 0, _lru_chunk(d, i, n_lat), 0)),
        out_shape=jax.ShapeDtypeStruct((2, b, lt, w), F32),
        scratch_shapes=[pltpu.VMEM((b, chunk, w), F32), pltpu.VMEM((b, chunk, w), F32), pltpu.VMEM((b, 8, w), F32)],
        compiler_params=_params(("arbitrary", "arbitrary")),
        name="lru_scan",
    )(u, u, u, conv_w, conv_b, w_a, b_a, w_i, b_i, lam)


def _mla_body(q_ref, k_ref, v_ref, o_ref, m_ref, acc_ref, *, tk, nk):
    m_ref[...] = jnp.full(m_ref.shape, NEG_BIAS, F32)
    acc_ref[...] = jnp.zeros(acc_ref.shape, F32)

    def chunk(ci, carry):
        start = pl.multiple_of(ci * tk, tk)
        for h in range(MLA_HEADS):
            cols = slice(h * HEAD_PAD, (h + 1) * HEAD_PAD)
            s = _dot_nt(q_ref[0, :, cols], k_ref[0, pl.ds(start, tk), cols])
            blocks = [s[:, j * LANES:(j + 1) * LANES] for j in range(tk // LANES)]
            m_old = m_ref[h]
            m_new = jnp.maximum(m_old, jnp.max(functools.reduce(jnp.maximum, blocks), axis=-1, keepdims=True))
            p = jnp.concatenate([jnp.exp((blk - m_new).astype(BF16)) for blk in blocks], axis=-1)
            acc_ref[h] = jnp.exp(m_old - m_new) * acc_ref[h] + _dot(p, v_ref[0, pl.ds(start, tk), cols])
            m_ref[h] = m_new
        return carry

    lax.fori_loop(0, nk, chunk, 0)
    outs = [acc_ref[h][:, :V_HEAD] / acc_ref[h][:, V_HEAD:V_HEAD + 1] for h in range(MLA_HEADS)]
    o_ref[0] = jnp.concatenate(outs, axis=-1).astype(o_ref.dtype)


def _mla_attention(mq, mk, mv, *, tq, q_block0, n_q, k_block, k_block0, tk):
    b, lt, nw = mq.shape
    body = functools.partial(_mla_body, tk=tk, nk=k_block // tk)
    return pl.pallas_call(
        body,
        grid=(b, n_q),
        in_specs=[pl.BlockSpec((1, tq, nw), lambda bi, i: (bi, q_block0 + i, 0)),
                  pl.BlockSpec((1, k_block, nw), lambda bi, i: (bi, k_block0, 0), pipeline_mode=pl.Buffered(1)),
                  pl.BlockSpec((1, k_block, nw), lambda bi, i: (bi, k_block0, 0), pipeline_mode=pl.Buffered(1))],
        out_specs=pl.BlockSpec((1, tq, MLA_W), lambda bi, i: (bi, i, 0)),
        out_shape=jax.ShapeDtypeStruct((b, n_q * tq, MLA_W), BF16),
        scratch_shapes=[pltpu.VMEM((MLA_HEADS, tq, LANES), F32), pltpu.VMEM((MLA_HEADS, tq, HEAD_PAD), F32)],
        compiler_params=_params(("parallel", "arbitrary")),
        name="mla_attention",
    )(mq, mk, mv)


def _route(logits, rb):
    sel = _sigmoid(logits) + rb
    sc = [sel[j:j + 1, :] for j in range(N_EXPERTS)]
    gscore = []
    for g in range(N_GROUPS):
        a, b, c, d = sc[4 * g:4 * g + 4]
        hi1, lo1 = jnp.maximum(a, b), jnp.minimum(a, b)
        hi2, lo2 = jnp.maximum(c, d), jnp.minimum(c, d)
        gscore.append(jnp.maximum(hi1, hi2) + jnp.maximum(jnp.minimum(hi1, hi2), jnp.maximum(lo1, lo2)))
    best = jnp.zeros_like(gscore[0], dtype=jnp.int32)
    bval = gscore[0]
    for g in range(1, N_GROUPS):
        upd = gscore[g] > bval
        best = jnp.where(upd, g, best)
        bval = jnp.where(upd, gscore[g], bval)

    def pick(cols, j):
        out = cols[j]
        for g in range(1, N_GROUPS):
            out = jnp.where(best == g, cols[4 * g + j], out)
        return out

    v = [pick(sc, j) for j in range(EXPERTS_PER_GROUP)]
    code = jnp.zeros_like(best)
    for i in range(EXPERTS_PER_GROUP):
        rank = jnp.zeros_like(best)
        for j in range(EXPERTS_PER_GROUP):
            if j == i:
                continue
            beats = (v[j] > v[i]) if j > i else (v[j] >= v[i])
            rank = rank + beats.astype(jnp.int32)
        code = code + jnp.where(rank < 2, 1 << i, 0)
    pair = jnp.zeros_like(best)
    for p, (lo, hi) in enumerate(_PAIRS):
        pair = jnp.where(code == (1 << lo) + (1 << hi), p, pair)
    return best * N_PAIRS + pair


def _proj_out_body(ya_ref, hf_ref, hr_ref, gg_ref, yc_ref, x_ref, wa_ref, wb_ref, wc_ref, g1_ref, g2_ref,
                   gate_s_ref, gate_c_ref, shift_s_ref, shift_c_ref, scale_s_ref, scale_c_ref, rwt_ref, rb_ref,
                   xo_ref, h2_ref, cls_ref, *, rt):
    yb = (hf_ref[0, 0] + hr_ref[0, 0]) * gg_ref[...].astype(F32)
    y = _dot(ya_ref[...], wa_ref[...]) + _dot(yb.astype(BF16), wb_ref[...]) + _dot(yc_ref[...], wc_ref[...])
    ga_t, ga_b = rt.mod_rows(gate_s_ref, gate_c_ref)
    sh_t, sh_b = rt.mod_rows(shift_s_ref, shift_c_ref)
    sc_t, sc_b = rt.mod_rows(scale_s_ref, scale_c_ref)
    x = x_ref[...] + rt.by_rows(_rms(y, g1_ref[...]), lambda v: ga_t * v, lambda v: ga_b * v)
    xo_ref[...] = x
    h2 = rt.by_rows(_rms(x, g2_ref[...]), lambda v: v * (1.0 + sc_t) + sh_t,
                    lambda v: v * (1.0 + sc_b) + sh_b).astype(BF16)
    _store_token_tiles(h2_ref, h2.astype(F32))
    cls_ref[0] = _route(_dot_nt(rwt_ref[...], h2), rb_ref[...])


def _proj_out(ya, hscan, gg, yc, x, w_a, w_b, w_c, g1, g2, gate, shift, scale, rw, rb, *, rt):
    t, d = x.shape
    tm = rt.tm
    nt = t // tm
    row = lambda n: pl.BlockSpec((tm, n), lambda i: (i, 0))
    full = lambda a: pl.BlockSpec(a.shape, lambda i: (0,) * a.ndim)
    scan_spec = lambda dd: pl.BlockSpec((1, 1, tm, LRU_W),
                                        lambda i: (dd, i // rt.tiles_per_sample, i % rt.tiles_per_sample, 0))
    return pl.pallas_call(
        functools.partial(_proj_out_body, rt=rt),
        grid=(nt,),
        in_specs=[row(NA_W), scan_spec(0), scan_spec(1), row(LRU_W), row(MLA_W), row(d),
                  full(w_a), full(w_b), full(w_c), full(g1), full(g2),
                  *rt.mod_specs(d), *rt.mod_specs(d), *rt.mod_specs(d), full(rw), full(rb)],
        out_specs=[row(d), pl.BlockSpec((tm * SUBLANES, LANES), lambda i: (i, 0)),
                   pl.BlockSpec((1, 1, tm), lambda i: (i, 0, 0))],
        out_shape=[jax.ShapeDtypeStruct((t, d), F32),
                   jax.ShapeDtypeStruct((t * SUBLANES, LANES), F32),
                   jax.ShapeDtypeStruct((nt, 1, tm), jnp.int32)],
        compiler_params=_params(("parallel",)),
        name="proj_out",
    )(ya, hscan, hscan, gg, yc, x, w_a, w_b, w_c, g1, g2, gate, gate, shift, shift, scale, scale, rw, rb)


def _store_token_tiles(ref, v, token0=0):
    n = v.shape[0]
    for s in range(SUBLANES):
        ref[pl.ds(token0 * SUBLANES + s, n, stride=SUBLANES), :] = v[:, s * LANES:(s + 1) * LANES]


def _load_token_tiles(ref, n, token0=0):
    return jnp.concatenate([ref[pl.ds(token0 * SUBLANES + s, n, stride=SUBLANES), :] for s in range(SUBLANES)],
                           axis=1)


def _gather_token_tiles(idx_ref, base, src_ref, buf_ref, n):
    def copy(r, carry):
        j = idx_ref[base + r]
        buf_ref[pl.ds(pl.multiple_of(r * SUBLANES, SUBLANES), SUBLANES), :] = (
            src_ref[0, pl.ds(pl.multiple_of(j * SUBLANES, SUBLANES), SUBLANES), :])
        return carry

    lax.fori_loop(0, n, copy, 0, unroll=8)


def _dispatch_body(pos_ref, h_ref, o_ref, buf_ref, src_ref, *, tokens_per_sample):
    @pl.when(pl.program_id(1) == 0)
    def _():
        base = pl.program_id(0) * tokens_per_sample

        def invert(t, carry):
            src_ref[pos_ref[base + t]] = t
            return carry

        lax.fori_loop(0, tokens_per_sample, invert, 0, unroll=8)

    tm = o_ref.shape[0]
    _gather_token_tiles(src_ref, pl.program_id(1) * tm, h_ref, buf_ref, tm)
    o_ref[...] = _load_token_tiles(buf_ref, tm).astype(o_ref.dtype)


def _dispatch(pos, h2t, *, n_batch, tokens_per_sample, tm):
    n_tiles = tokens_per_sample // tm
    return pl.pallas_call(
        functools.partial(_dispatch_body, tokens_per_sample=tokens_per_sample),
        grid_spec=pltpu.PrefetchScalarGridSpec(
            num_scalar_prefetch=1,
            grid=(n_batch, n_tiles),
            in_specs=[pl.BlockSpec((1, tokens_per_sample * SUBLANES, LANES), lambda b, i, pos: (b, 0, 0),
                                   pipeline_mode=pl.Buffered(1))],
            out_specs=pl.BlockSpec((tm, D_MODEL), lambda b, i, pos: (b * n_tiles + i, 0)),
            scratch_shapes=[pltpu.VMEM((tm * SUBLANES, LANES), F32), pltpu.SMEM((tokens_per_sample,), jnp.int32)]),
        out_shape=jax.ShapeDtypeStruct((n_batch * tokens_per_sample, D_MODEL), BF16),
        compiler_params=_params(("arbitrary", "arbitrary")),
        name="moe_dispatch",
    )(pos, h2t.reshape(n_batch, tokens_per_sample * SUBLANES, LANES))


def _moe_body(tile_ref, elo_ref, ehi_ref, lo_ref, hi_ref, valid_ref, first_ref,
              x_ref, rw_ref, wg0_ref, wu0_ref, wd0_ref, wg1_ref, wu1_ref, wd1_ref, o_ref):
    k = pl.program_id(0)

    @pl.when(valid_ref[k] == 1)
    def _():
        @pl.when(first_ref[k] == 1)
        def _():
            o_ref[...] = jnp.zeros(o_ref.shape, o_ref.dtype)

        def visit(row0, n):
            x = x_ref[row0:row0 + n, :]
            aff = _sigmoid(_dot(x, rw_ref[...]))
            lane = lax.broadcasted_iota(jnp.int32, aff.shape, 1)
            a_lo = jnp.sum(jnp.where(lane == elo_ref[k], aff, 0.0), axis=-1, keepdims=True)
            a_hi = jnp.sum(jnp.where(lane == ehi_ref[k], aff, 0.0), axis=-1, keepdims=True)
            den = a_lo + a_hi
            row = row0 + lax.broadcasted_iota(jnp.int32, (n, 1), 0)
            mine = jnp.logical_and(row >= lo_ref[k], row < hi_ref[k])

            def add_expert(a_e, wg, wu, wd):
                a = _dot(x, wg[0])
                he = (a * _sigmoid(a)) * _dot(x, wu[0])
                y = jnp.where(mine, a_e / den, 0.0) * _dot(he.astype(BF16), wd[0])
                _store_token_tiles(o_ref, _load_token_tiles(o_ref, n, row0) + y, row0)

            add_expert(a_lo, wg0_ref, wu0_ref, wd0_ref)
            add_expert(a_hi, wg1_ref, wu1_ref, wd1_ref)

        half = TM // 2
        in_top = hi_ref[k] <= half
        in_bottom = lo_ref[k] >= half

        @pl.when(in_top)
        def _():
            visit(0, half)

        @pl.when(in_bottom)
        def _():
            visit(half, half)

        @pl.when(jnp.logical_not(jnp.logical_or(in_top, in_bottom)))
        def _():
            visit(0, TM)


def _moe_routed(items, xs, rw, w_gate, w_up, w_down, *, expert_base):
    t, d = xs.shape
    _, _, de = w_gate.shape
    n_items = items[0].shape[0]
    xspec = pl.BlockSpec((TM, d), lambda k, tile, *_: (tile[k], 0))
    w_in_spec = lambda which: pl.BlockSpec(
        (1, d, de), lambda k, tile, elo, ehi, *_: (expert_base + (elo, ehi)[which][k], 0, 0))
    w_out_spec = lambda which: pl.BlockSpec(
        (1, de, d), lambda k, tile, elo, ehi, *_: (expert_base + (elo, ehi)[which][k], 0, 0))
    return pl.pallas_call(
        _moe_body,
        grid_spec=pltpu.PrefetchScalarGridSpec(
            num_scalar_prefetch=len(items),
            grid=(n_items,),
            in_specs=[xspec, pl.BlockSpec(rw.shape, lambda k, *_: (0, 0)),
                      w_in_spec(0), w_in_spec(0), w_out_spec(0), w_in_spec(1), w_in_spec(1), w_out_spec(1)],
            out_specs=pl.BlockSpec((TM * SUBLANES, LANES), lambda k, tile, *_: (tile[k], 0))),
        out_shape=jax.ShapeDtypeStruct((t * SUBLANES, LANES), F32),
        compiler_params=_params(("arbitrary",)),
        name="moe_routed",
    )(*items, xs, rw, w_gate, w_up, w_down, w_gate, w_up, w_down)


def _combine_body(pos_ref, y_ref, x_ref, gate_ref, g_ref, o_ref, buf_ref, *, tokens_per_sample):
    base = pl.program_id(0) * tokens_per_sample + pl.program_id(1) * TM
    _gather_token_tiles(pos_ref, base, y_ref, buf_ref, TM)
    o_ref[...] = x_ref[...] + gate_ref[0] * _rms(_load_token_tiles(buf_ref, TM), g_ref[...])


def _combine(pos, ys, x, gate, g, *, n_batch, tokens_per_sample, n_lat_tiles, n_out_tiles):
    n_tiles = tokens_per_sample // TM
    d = x.shape[1]
    return pl.pallas_call(
        functools.partial(_combine_body, tokens_per_sample=tokens_per_sample),
        grid_spec=pltpu.PrefetchScalarGridSpec(
            num_scalar_prefetch=1,
            grid=(n_batch, n_out_tiles),
            in_specs=[pl.BlockSpec((1, tokens_per_sample * SUBLANES, LANES), lambda b, i, pos: (b, 0, 0),
                                   pipeline_mode=pl.Buffered(1)),
                      pl.BlockSpec((TM, d), lambda b, i, pos: (b * n_tiles + i, 0)),
                      pl.BlockSpec((1, 1, d), lambda b, i, pos: (jnp.where(i >= n_lat_tiles, n_batch, b), 0, 0)),
                      pl.BlockSpec(g.shape, lambda b, i, pos: (0, 0))],
            out_specs=pl.BlockSpec((TM, d), lambda b, i, pos: (b * n_out_tiles + i, 0)),
            scratch_shapes=[pltpu.VMEM((TM * SUBLANES, LANES), F32)]),
        out_shape=jax.ShapeDtypeStruct((n_batch * n_out_tiles * TM, d), F32),
        compiler_params=_params(("arbitrary", "arbitrary")),
        name="moe_combine",
    )(pos, ys.reshape(n_batch, tokens_per_sample * SUBLANES, LANES), x, gate, g)


def _routing_plan(cls, *, n_tiles):
    b, l = cls.shape
    n_items = n_tiles + N_CLASSES - 1
    onehot = (cls[..., None] == jnp.arange(N_CLASSES, dtype=jnp.int32)).astype(jnp.int32)
    csum = jnp.cumsum(onehot, axis=1)
    counts = csum[:, -1]
    starts = jnp.cumsum(counts, axis=1) - counts
    pos = jnp.sum(onehot * (starts[:, None, :] + csum - 1), axis=-1)
    bi = jnp.arange(b, dtype=jnp.int32)[:, None]

    first_tile = starts // TM
    last_tile = (starts + counts - 1) // TM
    m = jnp.where(counts > 0, last_tile - first_tile + 1, 0)
    o_end = jnp.cumsum(m, axis=1)
    o_start = o_end - m
    total = o_end[:, -1:]
    k = jnp.arange(n_items, dtype=jnp.int32)[None, :]
    kk = jnp.minimum(k, total - 1)
    c_k = jnp.sum((kk[:, :, None] >= o_end[:, None, :]).astype(jnp.int32), axis=-1)
    take = lambda a: jnp.take_along_axis(a, c_k, axis=1)
    tile_k = take(first_tile) + kk - take(o_start)
    valid = (k < total).astype(jnp.int32)
    lo = jnp.maximum(take(starts), tile_k * TM) - tile_k * TM
    hi = jnp.minimum(take(starts + counts), (tile_k + 1) * TM) - tile_k * TM
    prev_tile = jnp.concatenate([jnp.full((b, 1), -1, jnp.int32), tile_k[:, :-1]], axis=1)
    first = (tile_k != prev_tile).astype(jnp.int32)
    pair_lo = jnp.asarray([p[0] for p in _PAIRS], jnp.int32)
    pair_hi = jnp.asarray([p[1] for p in _PAIRS], jnp.int32)
    e_lo = (c_k // N_PAIRS) * EXPERTS_PER_GROUP + pair_lo[c_k % N_PAIRS]
    e_hi = (c_k // N_PAIRS) * EXPERTS_PER_GROUP + pair_hi[c_k % N_PAIRS]
    tile_g = tile_k + bi * n_tiles
    items = tuple(a.reshape(-1).astype(jnp.int32) for a in (tile_g, e_lo, e_hi, lo, hi, valid, first))
    return pos.reshape(-1).astype(jnp.int32), items


def _rope_tables(seq, ctx):
    t = np.arange(seq)
    axis_dim = QK_ROPE // 2
    inv = np.float32(ROPE_BASE) ** (-np.arange(0, axis_dim, 2, dtype=np.float32) / np.float32(axis_dim))
    ang_r = (t // GRID_W).astype(np.float32)[:, None] * inv
    ang_c = (t % GRID_W).astype(np.float32)[:, None] * inv
    cos = jnp.asarray(np.concatenate([np.cos(ang_r), np.cos(ang_r), np.cos(ang_c), np.cos(ang_c)], axis=-1))
    sin = jnp.asarray(np.concatenate([-np.sin(ang_r), np.sin(ang_r), -np.sin(ang_c), np.sin(ang_c)], axis=-1))
    pad = HEAD_PAD - QK_NOPE - QK_ROPE
    cos = jnp.concatenate([jnp.ones((seq, QK_NOPE), F32), cos, jnp.ones((seq, pad), F32)], axis=-1)
    sin = jnp.concatenate([jnp.zeros((seq, QK_NOPE), F32), sin, jnp.zeros((seq, pad), F32)], axis=-1)
    cos = jnp.concatenate([cos, jnp.ones((ctx, HEAD_PAD), F32)], axis=0)
    sin = jnp.concatenate([sin, jnp.zeros((ctx, HEAD_PAD), F32)], axis=0)
    return cos, sin


_ROPE_SWAP = np.concatenate([np.arange(8, 16), np.arange(0, 8), np.arange(24, 32), np.arange(16, 24)])


def _layout_w_in(w_in):
    d = w_in.shape[0]
    z = lambda n: jnp.zeros((d, n), w_in.dtype)
    kr = w_in[:, _C_KR:_C_KR + QK_ROPE]
    pad = HEAD_PAD - QK_NOPE - QK_ROPE
    return jnp.concatenate([w_in[:, :_C_KR], z(QK_NOPE), kr, z(pad), z(QK_NOPE), kr[:, _ROPE_SWAP], z(pad)],
                           axis=1).astype(BF16)


def _layout_w_q(w_q):
    r = w_q.shape[0]
    wh = w_q.reshape(r, MLA_HEADS, QK_NOPE + QK_ROPE)
    pad = HEAD_PAD - QK_NOPE - QK_ROPE
    full = jnp.concatenate([wh, jnp.zeros((r, MLA_HEADS, pad), w_q.dtype)], axis=-1)
    swap = jnp.concatenate([jnp.zeros((r, MLA_HEADS, QK_NOPE), w_q.dtype), wh[:, :, QK_NOPE + _ROPE_SWAP],
                            jnp.zeros((r, MLA_HEADS, pad), w_q.dtype)], axis=-1)
    return jnp.concatenate([full.reshape(r, -1), swap.reshape(r, -1)], axis=1).astype(BF16)


def _layout_w_kv(w_kv):
    r = w_kv.shape[0]
    wh = w_kv.reshape(r, MLA_HEADS, QK_NOPE + V_HEAD)
    zk = jnp.zeros((r, MLA_HEADS, HEAD_PAD - QK_NOPE), w_kv.dtype)
    zv = jnp.zeros((r, MLA_HEADS, HEAD_PAD - V_HEAD), w_kv.dtype)
    k = jnp.concatenate([wh[:, :, :QK_NOPE], zk], axis=-1).reshape(r, -1)
    v = jnp.concatenate([wh[:, :, QK_NOPE:], zv], axis=-1).reshape(r, -1)
    return jnp.concatenate([k, v], axis=1).astype(BF16)


def _block_diag(w):
    nd, nb, k, _ = w.shape
    eye = jnp.eye(nb, dtype=w.dtype)
    return (w[:, :, :, None, :] * eye[None, :, None, :, None]).reshape(nd, nb * k, nb * k).astype(BF16)


def _pick_chunk(n, candidates):
    for c in candidates:
        if n % c == 0:
            return c
    raise ValueError(f"no chunk size for {n}")


def kernel(x, c, ctx, c_ctx, w_ada, b_ada, g_norm, w_in, w_out, na_rpb, conv_w, conv_b, lru_w_a, lru_b_a, lru_w_i,
           lru_b_i, lru_lam, mla_g_q, mla_w_q, mla_g_kv, mla_w_kv, router_w, router_b, exp_w_gate, exp_w_up,
           exp_w_down):
    b, seq, d = x.shape
    n_ctx = ctx.shape[1]
    depth = w_ada.shape[0]
    lt = seq + n_ctx
    assert d == D_MODEL == SUBLANES * LANES and n_ctx == TM and seq % TM == 0 and seq // GRID_W >= WIN_ROWS
    assert b + 1 <= 8
    tiles_per_sample = lt // TM
    n_lat_tiles = seq // TM
    rt = _RowTiling(seq, lt, b, _pick_chunk(lt, (PROJ_TM, TM)))

    xa = jnp.concatenate([x, ctx], axis=1).reshape(b * lt, d)

    c_rows = jnp.concatenate([c, c_ctx[None, :], jnp.zeros((8 - b - 1, d), F32)], axis=0)
    mod = _ada_tables(c_rows, w_ada, b_ada).reshape(depth, 8, N_MOD, 1, d)

    cos_t, sin_t = _rope_tables(seq, n_ctx)
    ones_row = jnp.zeros((MLA_HEADS, HEAD_PAD), F32).at[:, V_HEAD].set(1.0).reshape(1, -1)
    rw = jnp.concatenate([router_w, jnp.zeros((d, LANES - N_EXPERTS), F32)], axis=1).astype(BF16)
    wg_all, wu_all, wd_all = (w.astype(BF16).reshape((depth * N_EXPERTS,) + w.shape[2:])
                              for w in (exp_w_gate, exp_w_up, exp_w_down))
    rwt = router_w.T.astype(BF16)
    rb = router_b[:, None]
    tk = _pick_chunk(lt, (768, 512, 256))
    tq = _pick_chunk(seq, (2048, 1024, 512, 256))

    for l in range(depth):
        last = l == depth - 1
        m = lambda k: mod[l, :b + 1, k]
        g = g_norm[l]
        qkv, u, gg, mq, mk, mv = _proj_in(
            xa, m(0), m(1), g[0:1], _layout_w_in(w_in[l]), mla_g_q[l][None, :], _layout_w_q(mla_w_q[l]),
            mla_g_kv[l][None, :], _layout_w_kv(mla_w_kv[l]), cos_t, sin_t, ones_row, rt=rt)

        ya = _na_attention(qkv.reshape(b, lt, -1), _na_bias_tables(na_rpb[l]), seq=seq, ctx=n_ctx)
        hscan = _lru_scan(u.reshape(b, lt, -1), conv_w[l], conv_b[l][None, :], _block_diag(lru_w_a[l]),
                          lru_b_a[l][:, None, :], _block_diag(lru_w_i[l]), lru_b_i[l][:, None, :],
                          lru_lam[l][:, None, :], seq=seq, ctx=n_ctx)
        mq3, mk3, mv3 = (a.reshape(b, lt, -1) for a in (mq, mk, mv))
        yc = _mla_attention(mq3, mk3, mv3, tq=tq, q_block0=0, n_q=seq // tq, k_block=lt, k_block0=0, tk=tk)
        if last:
            yc = jnp.concatenate([yc, jnp.zeros((b, n_ctx, MLA_W), BF16)], axis=1)
        else:
            yc_ctx = _mla_attention(mq3, mk3, mv3, tq=n_ctx, q_block0=seq // n_ctx, n_q=1, k_block=n_ctx,
                                    k_block0=seq // n_ctx, tk=n_ctx)
            yc = jnp.concatenate([yc, yc_ctx], axis=1)

        wo = w_out[l].astype(BF16)
        xa, h2t, cls = _proj_out(
            ya.reshape(b * lt, -1), hscan, gg, yc.reshape(b * lt, -1), xa,
            wo[:NA_W], wo[NA_W:NA_W + LRU_W], wo[NA_W + LRU_W:], g[1:2], g[2:3], m(2), m(3), m(4), rwt, rb, rt=rt)

        pos, items = _routing_plan(cls.reshape(b, lt), n_tiles=tiles_per_sample)
        xs = _dispatch(pos, h2t, n_batch=b, tokens_per_sample=lt, tm=rt.tm)
        ys = _moe_routed(items, xs, rw, wg_all, wu_all, wd_all, expert_base=l * N_EXPERTS)
        xa = _combine(pos, ys, xa, m(5), g[3:4], n_batch=b, tokens_per_sample=lt, n_lat_tiles=n_lat_tiles,
                      n_out_tiles=n_lat_tiles if last else tiles_per_sample)

    return xa.reshape(b, seq, d)
```

```python
import functools

import numpy as np
import jax
import jax.numpy as jnp
from jax import lax
from jax.experimental import pallas as pl
from jax.experimental.pallas import tpu as pltpu

F32 = jnp.float32
BF16 = jnp.bfloat16

D_MODEL = 1024
GRID_W = 64
NA_HEADS = 4
NA_HEAD_DIM = 64
NA_W = NA_HEADS * NA_HEAD_DIM
WIN_ROWS = 8
WIN_COLS = 16
LRU_W = 512
LRU_BLOCKS = 8
CONV_W = 4
LRU_C = 8.0
MLA_HEADS = 4
Q_LORA = 256
KV_LORA = 128
QK_NOPE = 64
QK_ROPE = 32
V_HEAD = 64
MLA_W = MLA_HEADS * V_HEAD
ROPE_BASE = 10000.0
N_EXPERTS = 16
N_GROUPS = 4
EXPERTS_PER_GROUP = 4
D_EXPERT = 512
RMS_EPS = 1e-6
N_MOD = 6

_PAIRS = ((0, 1), (0, 2), (0, 3), (1, 3), (1, 2), (3, 2))
N_PAIRS = len(_PAIRS)
N_CLASSES = N_GROUPS * N_PAIRS

LANES = 128
SUBLANES = 8
TM = 256
PROJ_TM = 768
HEAD_PAD = 128
NEG_BIAS = -1e30
VMEM_LIMIT = 56 * 1024 * 1024

_C_QA, _C_KA, _C_VA = 0, 256, 512
_C_U = 768
_C_GATE = 1280
_C_CQ = 1792
_C_CKV = 2048
_C_KR = 2176
_C_KRP = 2304
_IN_COLS = 2432


def _params(sem, vmem=VMEM_LIMIT):
    return pltpu.CompilerParams(dimension_semantics=sem, vmem_limit_bytes=vmem)


def _sigmoid(v):
    return 1.0 / (1.0 + jnp.exp(-v))


def _rms(v, g):
    return v * lax.rsqrt(jnp.mean(v * v, axis=-1, keepdims=True) + RMS_EPS) * g


def _dot(a, b):
    return jnp.dot(a, b, preferred_element_type=F32)


def _dot_nt(a, b):
    return lax.dot_general(a, b, (((1,), (1,)), ((), ())), preferred_element_type=F32)


def _ada_body(c_ref, w_ref, b_ref, o_ref):
    c = c_ref[...]
    sc = c * _sigmoid(c)
    o_ref[0] = _dot(sc.astype(BF16), w_ref[0].astype(BF16)) + b_ref[0]


def _ada_tables(c_rows, w_ada, b_ada):
    depth, d, n = w_ada.shape
    tn = 1536
    return pl.pallas_call(
        _ada_body,
        grid=(depth, n // tn),
        in_specs=[pl.BlockSpec((8, d), lambda l, j: (0, 0)),
                  pl.BlockSpec((1, d, tn), lambda l, j: (l, 0, j)),
                  pl.BlockSpec((1, 1, tn), lambda l, j: (l, 0, j))],
        out_specs=pl.BlockSpec((1, 8, tn), lambda l, j: (l, 0, j)),
        out_shape=jax.ShapeDtypeStruct((depth, 8, n), F32),
        compiler_params=_params(("parallel", "parallel")),
        name="ada_tables",
    )(c_rows, w_ada, b_ada.reshape(depth, 1, n))


class _RowTiling:
    def __init__(self, seq, tokens_per_sample, n_batch, tm):
        assert tokens_per_sample % tm == 0
        self.tm = tm
        self.n_batch = n_batch
        self.tiles_per_sample = tokens_per_sample // tm
        self.ctx_tile = seq // tm
        self.split = seq % tm

    def mod_specs(self, d):
        return [pl.BlockSpec((1, 1, d), lambda i: (i // self.tiles_per_sample, 0, 0)),
                pl.BlockSpec((1, 1, d), lambda i: (self.n_batch, 0, 0))]

    def mod_rows(self, sample_ref, ctx_ref):
        j = pl.program_id(0) % self.tiles_per_sample
        top = jnp.where(j <= self.ctx_tile, sample_ref[0], ctx_ref[0])
        bot = jnp.where(j < self.ctx_tile, sample_ref[0], ctx_ref[0])
        return top, bot

    def by_rows(self, v, f_top, f_bot):
        if self.split == 0:
            return f_bot(v)
        return jnp.concatenate([f_top(v[:self.split]), f_bot(v[self.split:])], axis=0)


def _proj_in_body(x_ref, shift_s_ref, shift_c_ref, scale_s_ref, scale_c_ref, g_ref, w_ref, gq_ref, wq_ref,
                  gkv_ref, wkv_ref, cos_ref, sin_ref, ones_ref,
                  qkv_ref, u_ref, gg_ref, mq_ref, mk_ref, mv_ref, *, mla_scale, rt):
    x = x_ref[...]
    sh_t, sh_b = rt.mod_rows(shift_s_ref, shift_c_ref)
    sc_t, sc_b = rt.mod_rows(scale_s_ref, scale_c_ref)
    h = rt.by_rows(_rms(x, g_ref[...]), lambda v: v * (1.0 + sc_t) + sh_t, lambda v: v * (1.0 + sc_b) + sh_b)
    z = _dot(h.astype(BF16), w_ref[...])
    qkv_ref[:, 0:NA_W] = (z[:, _C_QA:_C_KA] * (NA_HEAD_DIM ** -0.5)).astype(BF16)
    qkv_ref[:, NA_W:3 * NA_W] = z[:, _C_KA:_C_U].astype(BF16)
    u_ref[...] = z[:, _C_U:_C_GATE]
    gg_ref[...] = jax.nn.gelu(z[:, _C_GATE:_C_CQ]).astype(BF16)

    cos = cos_ref[...]
    sin = sin_ref[...]
    cos4 = jnp.concatenate([cos] * MLA_HEADS, axis=-1)
    sin4 = jnp.concatenate([sin] * MLA_HEADS, axis=-1)
    nw = MLA_HEADS * HEAD_PAD

    nq = _rms(z[:, _C_CQ:_C_CKV], gq_ref[...])
    q2 = _dot(nq.astype(BF16), wq_ref[...])
    mq_ref[...] = ((q2[:, :nw] * cos4 + q2[:, nw:] * sin4) * mla_scale).astype(BF16)

    nkv = _rms(z[:, _C_CKV:_C_KR], gkv_ref[...])
    kv2 = _dot(nkv.astype(BF16), wkv_ref[...])
    k_rope = z[:, _C_KR:_C_KRP] * cos + z[:, _C_KRP:_IN_COLS] * sin
    mk_ref[...] = (kv2[:, :nw] + jnp.concatenate([k_rope] * MLA_HEADS, axis=-1)).astype(BF16)
    mv_ref[...] = (kv2[:, nw:] + ones_ref[...]).astype(BF16)


def _proj_in(x, shift, scale, g, w_big, gq, wq2, gkv, wkv2, cos_t, sin_t, ones_row, *, rt):
    t, d = x.shape
    tm = rt.tm
    nt = t // tm
    nw = MLA_HEADS * HEAD_PAD
    row = lambda n: pl.BlockSpec((tm, n), lambda i: (i, 0))
    full = lambda a: pl.BlockSpec(a.shape, lambda i: (0,) * a.ndim)
    tabspec = pl.BlockSpec((tm, HEAD_PAD), lambda i: (i % rt.tiles_per_sample, 0))
    mla_scale = (QK_NOPE + QK_ROPE) ** -0.5
    return pl.pallas_call(
        functools.partial(_proj_in_body, mla_scale=mla_scale, rt=rt),
        grid=(nt,),
        in_specs=[row(d), *rt.mod_specs(d), *rt.mod_specs(d), full(g), full(w_big), full(gq), full(wq2), full(gkv),
                  full(wkv2), tabspec, tabspec, full(ones_row)],
        out_specs=[row(3 * NA_W), row(LRU_W), row(LRU_W), row(nw), row(nw), row(nw)],
        out_shape=[jax.ShapeDtypeStruct((t, 3 * NA_W), BF16),
                   jax.ShapeDtypeStruct((t, LRU_W), F32),
                   jax.ShapeDtypeStruct((t, LRU_W), BF16),
                   jax.ShapeDtypeStruct((t, nw), BF16),
                   jax.ShapeDtypeStruct((t, nw), BF16),
                   jax.ShapeDtypeStruct((t, nw), BF16)],
        compiler_params=_params(("parallel",)),
        name="proj_in",
    )(x, shift, shift, scale, scale, g, w_big, gq, wq2, gkv, wkv2, cos_t, sin_t, ones_row)


def _head_block_mask():
    r = lax.broadcasted_iota(jnp.int32, (NA_W, NA_W), 0) // NA_HEAD_DIM
    c = lax.broadcasted_iota(jnp.int32, (NA_W, NA_W), 1) // NA_HEAD_DIM
    return r == c


def _na_attend(q, parts, mask):
    qbig = jnp.where(mask, jnp.concatenate([q] * NA_HEADS, axis=0), jnp.zeros((), q.dtype))
    scores = []
    for k, _, bias in parts:
        s = _dot_nt(qbig, k)
        if bias is not None:
            s = s + bias
        scores.append(s)
    m = functools.reduce(jnp.maximum, [jnp.max(s, axis=-1, keepdims=True) for s in scores])
    ps = [jnp.exp(s - m) for s in scores]
    l = functools.reduce(jnp.add, [jnp.sum(p, axis=-1, keepdims=True) for p in ps])
    o = functools.reduce(jnp.add, [_dot(p.astype(BF16), v) for p, (_, v, _) in zip(ps, parts)])
    o = jnp.where(mask, o / l, 0.0)
    out = o[0:NA_HEAD_DIM]
    for h in range(1, NA_HEADS):
        out = out + o[h * NA_HEAD_DIM:(h + 1) * NA_HEAD_DIM]
    return out


def _na_body(q_ref, k_ref, v_ref, bias_ref, o_ref, *, rows, rows_per_step, n_lat_steps, seq, ctx):
    i = pl.program_id(1)
    mask = _head_block_mask()
    kc = k_ref[0, seq:seq + ctx, :]
    vc = v_ref[0, seq:seq + ctx, :]

    @pl.when(i < n_lat_steps)
    def _():
        for j in range(rows_per_step):
            r = i * rows_per_step + j
            rs = jnp.clip(r - WIN_ROWS // 2, 0, rows - WIN_ROWS)
            off = rs - r + (WIN_ROWS - 1)
            start = pl.multiple_of(rs * GRID_W, GRID_W)
            kw = k_ref[0, pl.ds(start, WIN_ROWS * GRID_W), :]
            vw = v_ref[0, pl.ds(start, WIN_ROWS * GRID_W), :]
            q = q_ref[0, j * GRID_W:(j + 1) * GRID_W, :]
            out = _na_attend(q, [(kw, vw, bias_ref[off]), (kc, vc, None)], mask)
            o_ref[0, j * GRID_W:(j + 1) * GRID_W, :] = out.astype(o_ref.dtype)

    @pl.when(i >= n_lat_steps)
    def _():
        for j in range(rows_per_step):
            q = q_ref[0, j * GRID_W:(j + 1) * GRID_W, :]
            out = _na_attend(q, [(kc, vc, None)], mask)
            o_ref[0, j * GRID_W:(j + 1) * GRID_W, :] = out.astype(o_ref.dtype)


def _na_attention(qkv, bias, *, seq, ctx):
    b, lt, _ = qkv.shape
    rows = seq // GRID_W
    rows_per_step = TM // GRID_W
    n_steps = lt // TM
    n_lat_steps = seq // TM
    body = functools.partial(_na_body, rows=rows, rows_per_step=rows_per_step, n_lat_steps=n_lat_steps,
                             seq=seq, ctx=ctx)
    return pl.pallas_call(
        body,
        grid=(b, n_steps),
        in_specs=[pl.BlockSpec((1, TM, NA_W), lambda bi, i: (bi, i, 0)),
                  pl.BlockSpec((1, lt, NA_W), lambda bi, i: (bi, 0, 1)),
                  pl.BlockSpec((1, lt, NA_W), lambda bi, i: (bi, 0, 2)),
                  pl.BlockSpec(bias.shape, lambda bi, i: (0, 0, 0))],
        out_specs=pl.BlockSpec((1, TM, NA_W), lambda bi, i: (bi, i, 0)),
        out_shape=jax.ShapeDtypeStruct((b, lt, NA_W), BF16),
        compiler_params=_params(("parallel", "arbitrary")),
        name="na_attention",
    )(qkv, qkv, qkv, bias)


def _na_bias_tables(rpb):
    q = np.arange(GRID_W)
    cs = np.clip(q - WIN_COLS // 2, 0, GRID_W - WIN_COLS)
    c = np.arange(GRID_W)
    inside = (c[None, :] >= cs[:, None]) & (c[None, :] < cs[:, None] + WIN_COLS)
    dc = c[None, :] - q[:, None] + (WIN_COLS - 1)
    place = ((np.arange(2 * WIN_COLS - 1)[:, None, None] == dc[None]) & inside[None]).astype(np.float32)
    win = jnp.stack([rpb[:, o:o + WIN_ROWS, :] for o in range(WIN_ROWS)]).astype(F32)
    vals = jnp.einsum('ohik,kqc->ohqic', win, jnp.asarray(place), precision=lax.Precision.HIGHEST)
    vals = jnp.where(inside[None, None, :, None, :], vals, NEG_BIAS)
    return vals.reshape(WIN_ROWS, NA_HEADS * GRID_W, WIN_ROWS * GRID_W)


def _softplus(v):
    return jnp.maximum(v, 0.0) + jnp.log1p(jnp.exp(-jnp.abs(v)))


def _lru_body(up_ref, uc_ref, un_ref, cw_ref, cb_ref, wa_ref, ba_ref, wi_ref, bi_ref, lam_ref,
              o_ref, a_buf, x_buf, st_ref, *, n_lat, chunk, n_batch):
    d = pl.program_id(0)
    i = pl.program_id(1)
    c = _lru_chunk(d, i, n_lat)
    prev_ok = jnp.logical_and(c >= 1, c < n_lat)
    next_ok = c <= n_lat - 2

    row = lax.broadcasted_iota(jnp.int32, (chunk, 1), 0)
    cw = cw_ref[...]
    log_decay = -LRU_C * _softplus(-lam_ref[0])
    for b in range(n_batch):
        ub = uc_ref[b]
        p_row = jnp.where(prev_ok, up_ref[b, 7:8, :], 0.0)
        n0 = jnp.where(next_ok, un_ref[b, 0:1, :], 0.0)
        n1 = jnp.where(next_ok, un_ref[b, 1:2, :], 0.0)
        um1 = jnp.where(row == 0, p_row, pltpu.roll(ub, 1, 0))
        up1 = jnp.where(row == chunk - 1, n0, pltpu.roll(ub, chunk - 1, 0))
        up2 = jnp.where(row == chunk - 1, n1, jnp.where(row == chunk - 2, n0, pltpu.roll(ub, chunk - 2, 0)))
        cv = um1 * cw[0:1] + cb_ref[...]
        cv = cv + ub * cw[1:2]
        cv = cv + up1 * cw[2:3]
        cv = cv + up2 * cw[3:4]
        cv16 = cv.astype(BF16)
        r = _sigmoid(_dot(cv16, wa_ref[0]) + ba_ref[0])
        gi = _sigmoid(_dot(cv16, wi_ref[0]) + bi_ref[0])
        log_a = r * log_decay
        a = jnp.exp(log_a)
        a_buf[b] = a
        x_buf[b] = jnp.sqrt(1.0 - a * a) * (gi * cv)

    @pl.when(i == 0)
    def _():
        st_ref[...] = jnp.zeros_like(st_ref)

    def scan(reverse):
        n_groups = chunk // SUBLANES

        def group(gi, hs):
            base = pl.multiple_of((n_groups - 1 - gi if reverse else gi) * SUBLANES, SUBLANES)
            hs = list(hs)
            for k in range(SUBLANES):
                r = base + (SUBLANES - 1 - k if reverse else k)
                for b in range(n_batch):
                    hs[b] = a_buf[b, pl.ds(r, 1), :] * hs[b] + x_buf[b, pl.ds(r, 1), :]
                    o_ref[0, b, pl.ds(r, 1), :] = hs[b]
            return tuple(hs)

        hs = lax.fori_loop(0, n_groups, group, tuple(st_ref[b, 0:1, :] for b in range(n_batch)))
        for b in range(n_batch):
            st_ref[b, 0:1, :] = hs[b]

    @pl.when(d == 0)
    def _():
        scan(False)

    @pl.when(d == 1)
    def _():
        scan(True)


def _lru_chunk(d, i, n_lat):
    return jnp.where(i == 0, n_lat, jnp.where(d == 0, i - 1, n_lat - i))


def _lru_scan(u, conv_w, conv_b, w_a, b_a, w_i, b_i, lam, *, seq, ctx):
    b, lt, w = u.shape
    chunk = ctx
    n_lat = seq // chunk
    n_chunks = lt // chunk
    sub = chunk // 8
    n_sub = lt // 8

    cur = lambda d, i: (0, _lru_chunk(d, i, n_lat), 0)
    prev = lambda d, i: (0, jnp.maximum(_lru_chunk(d, i, n_lat) * sub - 1, 0), 0)
    nxt = lambda d, i: (0, jnp.minimum((_lru_chunk(d, i, n_lat) + 1) * sub, n_sub - 1), 0)
    per_dir = lambda a: pl.BlockSpec((1,) + a.shape[1:], lambda d, i: (d,) + (0,) * (a.ndim - 1))
    full = lambda a: pl.BlockSpec(a.shape, lambda d, i: (0,) * a.ndim)
    body = functools.partial(_lru_body, n_lat=n_lat, chunk=chunk, n_batch=b)
    return pl.pallas_call(
        body,
        grid=(2, n_chunks),
        in_specs=[pl.BlockSpec((b, 8, w), prev), pl.BlockSpec((b, chunk, w), cur), pl.BlockSpec((b, 8, w), nxt),
                  full(conv_w), full(conv_b), per_dir(w_a), per_dir(b_a), per_dir(w_i), per_dir(b_i), per_dir(lam)],
        out_specs=pl.BlockSpec((1, b, chunk, w), lambda d, i: (d, 0, _lru_chunk(d, i, n_lat), 0)),
        out_shape=jax.ShapeDtypeStruct((2, b, lt, w), F32),
        scratch_shapes=[pltpu.VMEM((b, chunk, w), F32), pltpu.VMEM((b, chunk, w), F32), pltpu.VMEM((b, 8, w), F32)],
        compiler_params=_params(("arbitrary", "arbitrary")),
        name="lru_scan",
    )(u, u, u, conv_w, conv_b, w_a, b_a, w_i, b_i, lam)


def _mla_body(q_ref, k_ref, v_ref, o_ref, m_ref, acc_ref, *, tk, nk):
    m_ref[...] = jnp.full(m_ref.shape, NEG_BIAS, F32)
    acc_ref[...] = jnp.zeros(acc_ref.shape, F32)

    def chunk(ci, carry):
        start = pl.multiple_of(ci * tk, tk)
        for h in range(MLA_HEADS):
            cols = slice(h * HEAD_PAD, (h + 1) * HEAD_PAD)
            s = _dot_nt(q_ref[0, :, cols], k_ref[0, pl.ds(start, tk), cols])
            blocks = [s[:, j * LANES:(j + 1) * LANES] for j in range(tk // LANES)]
            m_old = m_ref[h]
            m_new = jnp.maximum(m_old, jnp.max(functools.reduce(jnp.maximum, blocks), axis=-1, keepdims=True))
            p = jnp.concatenate([jnp.exp((blk - m_new).astype(BF16)) for blk in blocks], axis=-1)
            acc_ref[h] = jnp.exp(m_old - m_new) * acc_ref[h] + _dot(p, v_ref[0, pl.ds(start, tk), cols])
            m_ref[h] = m_new
        return carry

    lax.fori_loop(0, nk, chunk, 0)
    outs = [acc_ref[h][:, :V_HEAD] / acc_ref[h][:, V_HEAD:V_HEAD + 1] for h in range(MLA_HEADS)]
    o_ref[0] = jnp.concatenate(outs, axis=-1).astype(o_ref.dtype)


def _mla_attention(mq, mk, mv, *, tq, q_block0, n_q, k_block, k_block0, tk):
    b, lt, nw = mq.shape
    body = functools.partial(_mla_body, tk=tk, nk=k_block // tk)
    return pl.pallas_call(
        body,
        grid=(b, n_q),
        in_specs=[pl.BlockSpec((1, tq, nw), lambda bi, i: (bi, q_block0 + i, 0)),
                  pl.BlockSpec((1, k_block, nw), lambda bi, i: (bi, k_block0, 0), pipeline_mode=pl.Buffered(1)),
                  pl.BlockSpec((1, k_block, nw), lambda bi, i: (bi, k_block0, 0), pipeline_mode=pl.Buffered(1))],
        out_specs=pl.BlockSpec((1, tq, MLA_W), lambda bi, i: (bi, i, 0)),
        out_shape=jax.ShapeDtypeStruct((b, n_q * tq, MLA_W), BF16),
        scratch_shapes=[pltpu.VMEM((MLA_HEADS, tq, LANES), F32), pltpu.VMEM((MLA_HEADS, tq, HEAD_PAD), F32)],
        compiler_params=_params(("parallel", "arbitrary")),
        name="mla_attention",
    )(mq, mk, mv)


def _route(logits, rb):
    sel = _sigmoid(logits) + rb
    sc = [sel[j:j + 1, :] for j in range(N_EXPERTS)]
    gscore = []
    for g in range(N_GROUPS):
        a, b, c, d = sc[4 * g:4 * g + 4]
        hi1, lo1 = jnp.maximum(a, b), jnp.minimum(a, b)
        hi2, lo2 = jnp.maximum(c, d), jnp.minimum(c, d)
        gscore.append(jnp.maximum(hi1, hi2) + jnp.maximum(jnp.minimum(hi1, hi2), jnp.maximum(lo1, lo2)))
    best = jnp.zeros_like(gscore[0], dtype=jnp.int32)
    bval = gscore[0]
    for g in range(1, N_GROUPS):
        upd = gscore[g] > bval
        best = jnp.where(upd, g, best)
        bval = jnp.where(upd, gscore[g], bval)

    def pick(cols, j):
        out = cols[j]
        for g in range(1, N_GROUPS):
            out = jnp.where(best == g, cols[4 * g + j], out)
        return out

    v = [pick(sc, j) for j in range(EXPERTS_PER_GROUP)]
    code = jnp.zeros_like(best)
    for i in range(EXPERTS_PER_GROUP):
        rank = jnp.zeros_like(best)
        for j in range(EXPERTS_PER_GROUP):
            if j == i:
                continue
            beats = (v[j] > v[i]) if j > i else (v[j] >= v[i])
            rank = rank + beats.astype(jnp.int32)
        code = code + jnp.where(rank < 2, 1 << i, 0)
    pair = jnp.zeros_like(best)
    for p, (lo, hi) in enumerate(_PAIRS):
        pair = jnp.where(code == (1 << lo) + (1 << hi), p, pair)
    return best * N_PAIRS + pair


def _proj_out_body(ya_ref, hf_ref, hr_ref, gg_ref, yc_ref, x_ref, wa_ref, wb_ref, wc_ref, g1_ref, g2_ref,
                   gate_s_ref, gate_c_ref, shift_s_ref, shift_c_ref, scale_s_ref, scale_c_ref, rwt_ref, rb_ref,
                   xo_ref, h2_ref, cls_ref, *, rt):
    yb = (hf_ref[0, 0] + hr_ref[0, 0]) * gg_ref[...].astype(F32)
    y = _dot(ya_ref[...], wa_ref[...]) + _dot(yb.astype(BF16), wb_ref[...]) + _dot(yc_ref[...], wc_ref[...])
    ga_t, ga_b = rt.mod_rows(gate_s_ref, gate_c_ref)
    sh_t, sh_b = rt.mod_rows(shift_s_ref, shift_c_ref)
    sc_t, sc_b = rt.mod_rows(scale_s_ref, scale_c_ref)
    x = x_ref[...] + rt.by_rows(_rms(y, g1_ref[...]), lambda v: ga_t * v, lambda v: ga_b * v)
    xo_ref[...] = x
    h2 = rt.by_rows(_rms(x, g2_ref[...]), lambda v: v * (1.0 + sc_t) + sh_t,
                    lambda v: v * (1.0 + sc_b) + sh_b).astype(BF16)
    _store_token_tiles(h2_ref, h2.astype(F32))
    cls_ref[0] = _route(_dot_nt(rwt_ref[...], h2), rb_ref[...])


def _proj_out(ya, hscan, gg, yc, x, w_a, w_b, w_c, g1, g2, gate, shift, scale, rw, rb, *, rt):
    t, d = x.shape
    tm = rt.tm
    nt = t // tm
    row = lambda n: pl.BlockSpec((tm, n), lambda i: (i, 0))
    full = lambda a: pl.BlockSpec(a.shape, lambda i: (0,) * a.ndim)
    scan_spec = lambda dd: pl.BlockSpec((1, 1, tm, LRU_W),
                                        lambda i: (dd, i // rt.tiles_per_sample, i % rt.tiles_per_sample, 0))
    return pl.pallas_call(
        functools.partial(_proj_out_body, rt=rt),
        grid=(nt,),
        in_specs=[row(NA_W), scan_spec(0), scan_spec(1), row(LRU_W), row(MLA_W), row(d),
                  full(w_a), full(w_b), full(w_c), full(g1), full(g2),
                  *rt.mod_specs(d), *rt.mod_specs(d), *rt.mod_specs(d), full(rw), full(rb)],
        out_specs=[row(d), pl.BlockSpec((tm * SUBLANES, LANES), lambda i: (i, 0)),
                   pl.BlockSpec((1, 1, tm), lambda i: (i, 0, 0))],
        out_shape=[jax.ShapeDtypeStruct((t, d), F32),
                   jax.ShapeDtypeStruct((t * SUBLANES, LANES), F32),
                   jax.ShapeDtypeStruct((nt, 1, tm), jnp.int32)],
        compiler_params=_params(("parallel",)),
        name="proj_out",
    )(ya, hscan, hscan, gg, yc, x, w_a, w_b, w_c, g1, g2, gate, gate, shift, shift, scale, scale, rw, rb)


def _store_token_tiles(ref, v, token0=0):
    n = v.shape[0]
    for s in range(SUBLANES):
        ref[pl.ds(token0 * SUBLANES + s, n, stride=SUBLANES), :] = v[:, s * LANES:(s + 1) * LANES]


def _load_token_tiles(ref, n, token0=0):
    return jnp.concatenate([ref[pl.ds(token0 * SUBLANES + s, n, stride=SUBLANES), :] for s in range(SUBLANES)],
                           axis=1)


def _gather_token_tiles(idx_ref, base, src_ref, buf_ref, n):
    def copy(r, carry):
        j = idx_ref[base + r]
        buf_ref[pl.ds(pl.multiple_of(r * SUBLANES, SUBLANES), SUBLANES), :] = (
            src_ref[0, pl.ds(pl.multiple_of(j * SUBLANES, SUBLANES), SUBLANES), :])
        return carry

    lax.fori_loop(0, n, copy, 0, unroll=8)


def _dispatch_body(pos_ref, h_hbm, o_ref, h_ref, buf_ref, src_ref, sem, *, tokens_per_sample):
    @pl.when(pl.program_id(1) == 0)
    def _():
        b = pl.program_id(0)
        fetch = pltpu.make_async_copy(h_hbm.at[pl.ds(b, 1)], h_ref, sem)
        fetch.start()
        base = b * tokens_per_sample

        def invert(t, carry):
            src_ref[pos_ref[base + t]] = t
            return carry

        lax.fori_loop(0, tokens_per_sample, invert, 0, unroll=8)
        fetch.wait()

    tm = o_ref.shape[0]
    _gather_token_tiles(src_ref, pl.program_id(1) * tm, h_ref, buf_ref, tm)
    o_ref[...] = _load_token_tiles(buf_ref, tm).astype(o_ref.dtype)


def _dispatch(pos, h2t, *, n_batch, tokens_per_sample, tm):
    n_tiles = tokens_per_sample // tm
    return pl.pallas_call(
        functools.partial(_dispatch_body, tokens_per_sample=tokens_per_sample),
        grid_spec=pltpu.PrefetchScalarGridSpec(
            num_scalar_prefetch=1,
            grid=(n_batch, n_tiles),
            in_specs=[pl.BlockSpec(memory_space=pl.ANY)],
            out_specs=pl.BlockSpec((tm, D_MODEL), lambda b, i, pos: (b * n_tiles + i, 0)),
            scratch_shapes=[pltpu.VMEM((1, tokens_per_sample * SUBLANES, LANES), F32),
                            pltpu.VMEM((tm * SUBLANES, LANES), F32),
                            pltpu.SMEM((tokens_per_sample,), jnp.int32),
                            pltpu.SemaphoreType.DMA(())]),
        out_shape=jax.ShapeDtypeStruct((n_batch * tokens_per_sample, D_MODEL), BF16),
        compiler_params=_params(("arbitrary", "arbitrary")),
        name="moe_dispatch",
    )(pos, h2t.reshape(n_batch, tokens_per_sample * SUBLANES, LANES))


def _moe_body(tile_ref, elo_ref, ehi_ref, lo_ref, hi_ref, valid_ref, first_ref,
              x_ref, rw_ref, wg0_ref, wu0_ref, wd0_ref, wg1_ref, wu1_ref, wd1_ref, o_ref):
    k = pl.program_id(0)

    @pl.when(valid_ref[k] == 1)
    def _():
        @pl.when(first_ref[k] == 1)
        def _():
            o_ref[...] = jnp.zeros(o_ref.shape, o_ref.dtype)

        def visit(row0, n):
            x = x_ref[row0:row0 + n, :]
            aff = _sigmoid(_dot(x, rw_ref[...]))
            lane = lax.broadcasted_iota(jnp.int32, aff.shape, 1)
            a_lo = jnp.sum(jnp.where(lane == elo_ref[k], aff, 0.0), axis=-1, keepdims=True)
            a_hi = jnp.sum(jnp.where(lane == ehi_ref[k], aff, 0.0), axis=-1, keepdims=True)
            den = a_lo + a_hi
            row = row0 + lax.broadcasted_iota(jnp.int32, (n, 1), 0)
            mine = jnp.logical_and(row >= lo_ref[k], row < hi_ref[k])

            def add_expert(a_e, wg, wu, wd):
                a = _dot(x, wg[0])
                he = (a * _sigmoid(a)) * _dot(x, wu[0])
                y = jnp.where(mine, a_e / den, 0.0) * _dot(he.astype(BF16), wd[0])
                _store_token_tiles(o_ref, _load_token_tiles(o_ref, n, row0) + y, row0)

            add_expert(a_lo, wg0_ref, wu0_ref, wd0_ref)
            add_expert(a_hi, wg1_ref, wu1_ref, wd1_ref)

        half = TM // 2
        in_top = hi_ref[k] <= half
        in_bottom = lo_ref[k] >= half

        @pl.when(in_top)
        def _():
            visit(0, half)

        @pl.when(in_bottom)
        def _():
            visit(half, half)

        @pl.when(jnp.logical_not(jnp.logical_or(in_top, in_bottom)))
        def _():
            visit(0, TM)


def _moe_routed(items, xs, rw, w_gate, w_up, w_down, *, expert_base):
    t, d = xs.shape
    _, _, de = w_gate.shape
    n_items = items[0].shape[0]
    xspec = pl.BlockSpec((TM, d), lambda k, tile, *_: (tile[k], 0))
    w_in_spec = lambda which: pl.BlockSpec(
        (1, d, de), lambda k, tile, elo, ehi, *_: (expert_base + (elo, ehi)[which][k], 0, 0))
    w_out_spec = lambda which: pl.BlockSpec(
        (1, de, d), lambda k, tile, elo, ehi, *_: (expert_base + (elo, ehi)[which][k], 0, 0))
    return pl.pallas_call(
        _moe_body,
        grid_spec=pltpu.PrefetchScalarGridSpec(
            num_scalar_prefetch=len(items),
            grid=(n_items,),
            in_specs=[xspec, pl.BlockSpec(rw.shape, lambda k, *_: (0, 0)),
                      w_in_spec(0), w_in_spec(0), w_out_spec(0), w_in_spec(1), w_in_spec(1), w_out_spec(1)],
            out_specs=pl.BlockSpec((TM * SUBLANES, LANES), lambda k, tile, *_: (tile[k], 0))),
        out_shape=jax.ShapeDtypeStruct((t * SUBLANES, LANES), F32),
        compiler_params=_params(("arbitrary",)),
        name="moe_routed",
    )(*items, xs, rw, w_gate, w_up, w_down, w_gate, w_up, w_down)


def _combine_body(pos_ref, y_ref, x_ref, gate_ref, g_ref, o_ref, buf_ref, *, tokens_per_sample):
    base = pl.program_id(0) * tokens_per_sample + pl.program_id(1) * TM
    _gather_token_tiles(pos_ref, base, y_ref, buf_ref, TM)
    o_ref[...] = x_ref[...] + gate_ref[0] * _rms(_load_token_tiles(buf_ref, TM), g_ref[...])


def _combine(pos, ys, x, gate, g, *, n_batch, tokens_per_sample, n_lat_tiles, n_out_tiles):
    n_tiles = tokens_per_sample // TM
    d = x.shape[1]
    return pl.pallas_call(
        functools.partial(_combine_body, tokens_per_sample=tokens_per_sample),
        grid_spec=pltpu.PrefetchScalarGridSpec(
            num_scalar_prefetch=1,
            grid=(n_batch, n_out_tiles),
            in_specs=[pl.BlockSpec((1, tokens_per_sample * SUBLANES, LANES), lambda b, i, pos: (b, 0, 0),
                                   pipeline_mode=pl.Buffered(1)),
                      pl.BlockSpec((TM, d), lambda b, i, pos: (b * n_tiles + i, 0)),
                      pl.BlockSpec((1, 1, d), lambda b, i, pos: (jnp.where(i >= n_lat_tiles, n_batch, b), 0, 0)),
                      pl.BlockSpec(g.shape, lambda b, i, pos: (0, 0))],
            out_specs=pl.BlockSpec((TM, d), lambda b, i, pos: (b * n_out_tiles + i, 0)),
            scratch_shapes=[pltpu.VMEM((TM * SUBLANES, LANES), F32)]),
        out_shape=jax.ShapeDtypeStruct((n_batch * n_out_tiles * TM, d), F32),
        compiler_params=_params(("arbitrary", "arbitrary")),
        name="moe_combine",
    )(pos, ys.reshape(n_batch, tokens_per_sample * SUBLANES, LANES), x, gate, g)


def _routing_plan(cls, *, n_tiles):
    b, l = cls.shape
    n_items = n_tiles + N_CLASSES - 1
    onehot = (cls[..., None] == jnp.arange(N_CLASSES, dtype=jnp.int32)).astype(jnp.int32)
    csum = jnp.cumsum(onehot, axis=1)
    counts = csum[:, -1]
    starts = jnp.cumsum(counts, axis=1) - counts
    pos = jnp.sum(onehot * (starts[:, None, :] + csum - 1), axis=-1)
    bi = jnp.arange(b, dtype=jnp.int32)[:, None]

    first_tile = starts // TM
    last_tile = (starts + counts - 1) // TM
    m = jnp.where(counts > 0, last_tile - first_tile + 1, 0)
    o_end = jnp.cumsum(m, axis=1)
    o_start = o_end - m
    total = o_end[:, -1:]
    k = jnp.arange(n_items, dtype=jnp.int32)[None, :]
    kk = jnp.minimum(k, total - 1)
    c_k = jnp.sum((kk[:, :, None] >= o_end[:, None, :]).astype(jnp.int32), axis=-1)
    take = lambda a: jnp.take_along_axis(a, c_k, axis=1)
    tile_k = take(first_tile) + kk - take(o_start)
    valid = (k < total).astype(jnp.int32)
    lo = jnp.maximum(take(starts), tile_k * TM) - tile_k * TM
    hi = jnp.minimum(take(starts + counts), (tile_k + 1) * TM) - tile_k * TM
    prev_tile = jnp.concatenate([jnp.full((b, 1), -1, jnp.int32), tile_k[:, :-1]], axis=1)
    first = (tile_k != prev_tile).astype(jnp.int32)
    pair_lo = jnp.asarray([p[0] for p in _PAIRS], jnp.int32)
    pair_hi = jnp.asarray([p[1] for p in _PAIRS], jnp.int32)
    e_lo = (c_k // N_PAIRS) * EXPERTS_PER_GROUP + pair_lo[c_k % N_PAIRS]
    e_hi = (c_k // N_PAIRS) * EXPERTS_PER_GROUP + pair_hi[c_k % N_PAIRS]
    tile_g = tile_k + bi * n_tiles
    items = tuple(a.reshape(-1).astype(jnp.int32) for a in (tile_g, e_lo, e_hi, lo, hi, valid, first))
    return pos.reshape(-1).astype(jnp.int32), items


def _rope_tables(seq, ctx):
    t = np.arange(seq)
    axis_dim = QK_ROPE // 2
    inv = np.float32(ROPE_BASE) ** (-np.arange(0, axis_dim, 2, dtype=np.float32) / np.float32(axis_dim))
    ang_r = (t // GRID_W).astype(np.float32)[:, None] * inv
    ang_c = (t % GRID_W).astype(np.float32)[:, None] * inv
    cos = jnp.asarray(np.concatenate([np.cos(ang_r), np.cos(ang_r), np.cos(ang_c), np.cos(ang_c)], axis=-1))
    sin = jnp.asarray(np.concatenate([-np.sin(ang_r), np.sin(ang_r), -np.sin(ang_c), np.sin(ang_c)], axis=-1))
    pad = HEAD_PAD - QK_NOPE - QK_ROPE
    cos = jnp.concatenate([jnp.ones((seq, QK_NOPE), F32), cos, jnp.ones((seq, pad), F32)], axis=-1)
    sin = jnp.concatenate([jnp.zeros((seq, QK_NOPE), F32), sin, jnp.zeros((seq, pad), F32)], axis=-1)
    cos = jnp.concatenate([cos, jnp.ones((ctx, HEAD_PAD), F32)], axis=0)
    sin = jnp.concatenate([sin, jnp.zeros((ctx, HEAD_PAD), F32)], axis=0)
    return cos, sin


_ROPE_SWAP = np.concatenate([np.arange(8, 16), np.arange(0, 8), np.arange(24, 32), np.arange(16, 24)])


def _layout_w_in(w_in):
    d = w_in.shape[0]
    z = lambda n: jnp.zeros((d, n), w_in.dtype)
    kr = w_in[:, _C_KR:_C_KR + QK_ROPE]
    pad = HEAD_PAD - QK_NOPE - QK_ROPE
    return jnp.concatenate([w_in[:, :_C_KR], z(QK_NOPE), kr, z(pad), z(QK_NOPE), kr[:, _ROPE_SWAP], z(pad)],
                           axis=1).astype(BF16)


def _layout_w_q(w_q):
    r = w_q.shape[0]
    wh = w_q.reshape(r, MLA_HEADS, QK_NOPE + QK_ROPE)
    pad = HEAD_PAD - QK_NOPE - QK_ROPE
    full = jnp.concatenate([wh, jnp.zeros((r, MLA_HEADS, pad), w_q.dtype)], axis=-1)
    swap = jnp.concatenate([jnp.zeros((r, MLA_HEADS, QK_NOPE), w_q.dtype), wh[:, :, QK_NOPE + _ROPE_SWAP],
                            jnp.zeros((r, MLA_HEADS, pad), w_q.dtype)], axis=-1)
    return jnp.concatenate([full.reshape(r, -1), swap.reshape(r, -1)], axis=1).astype(BF16)


def _layout_w_kv(w_kv):
    r = w_kv.shape[0]
    wh = w_kv.reshape(r, MLA_HEADS, QK_NOPE + V_HEAD)
    zk = jnp.zeros((r, MLA_HEADS, HEAD_PAD - QK_NOPE), w_kv.dtype)
    zv = jnp.zeros((r, MLA_HEADS, HEAD_PAD - V_HEAD), w_kv.dtype)
    k = jnp.concatenate([wh[:, :, :QK_NOPE], zk], axis=-1).reshape(r, -1)
    v = jnp.concatenate([wh[:, :, QK_NOPE:], zv], axis=-1).reshape(r, -1)
    return jnp.concatenate([k, v], axis=1).astype(BF16)


def _block_diag(w):
    nd, nb, k, _ = w.shape
    eye = jnp.eye(nb, dtype=w.dtype)
    return (w[:, :, :, None, :] * eye[None, :, None, :, None]).reshape(nd, nb * k, nb * k).astype(BF16)


def _pick_chunk(n, candidates):
    for c in candidates:
        if n % c == 0:
            return c
    raise ValueError(f"no chunk size for {n}")


def kernel(x, c, ctx, c_ctx, w_ada, b_ada, g_norm, w_in, w_out, na_rpb, conv_w, conv_b, lru_w_a, lru_b_a, lru_w_i,
           lru_b_i, lru_lam, mla_g_q, mla_w_q, mla_g_kv, mla_w_kv, router_w, router_b, exp_w_gate, exp_w_up,
           exp_w_down):
    b, seq, d = x.shape
    n_ctx = ctx.shape[1]
    depth = w_ada.shape[0]
    lt = seq + n_ctx
    assert d == D_MODEL == SUBLANES * LANES and n_ctx == TM and seq % TM == 0 and seq // GRID_W >= WIN_ROWS
    assert b + 1 <= 8
    tiles_per_sample = lt // TM
    n_lat_tiles = seq // TM
    rt = _RowTiling(seq, lt, b, _pick_chunk(lt, (PROJ_TM, TM)))

    xa = jnp.concatenate([x, ctx], axis=1).reshape(b * lt, d)

    c_rows = jnp.concatenate([c, c_ctx[None, :], jnp.zeros((8 - b - 1, d), F32)], axis=0)
    mod = _ada_tables(c_rows, w_ada, b_ada).reshape(depth, 8, N_MOD, 1, d)

    cos_t, sin_t = _rope_tables(seq, n_ctx)
    ones_row = jnp.zeros((MLA_HEADS, HEAD_PAD), F32).at[:, V_HEAD].set(1.0).reshape(1, -1)
    rw = jnp.concatenate([router_w, jnp.zeros((d, LANES - N_EXPERTS), F32)], axis=1).astype(BF16)
    wg_all, wu_all, wd_all = (w.astype(BF16).reshape((depth * N_EXPERTS,) + w.shape[2:])
                              for w in (exp_w_gate, exp_w_up, exp_w_down))
    rwt = router_w.T.astype(BF16)
    rb = router_b[:, None]
    tk = _pick_chunk(lt, (768, 512, 256))
    tq = _pick_chunk(seq, (2048, 1024, 512, 256))

    for l in range(depth):
        last = l == depth - 1
        m = lambda k: mod[l, :b + 1, k]
        g = g_norm[l]
        qkv, u, gg, mq, mk, mv = _proj_in(
            xa, m(0), m(1), g[0:1], _layout_w_in(w_in[l]), mla_g_q[l][None, :], _layout_w_q(mla_w_q[l]),
            mla_g_kv[l][None, :], _layout_w_kv(mla_w_kv[l]), cos_t, sin_t, ones_row, rt=rt)

        ya = _na_attention(qkv.reshape(b, lt, -1), _na_bias_tables(na_rpb[l]), seq=seq, ctx=n_ctx)
        hscan = _lru_scan(u.reshape(b, lt, -1), conv_w[l], conv_b[l][None, :], _block_diag(lru_w_a[l]),
                          lru_b_a[l][:, None, :], _block_diag(lru_w_i[l]), lru_b_i[l][:, None, :],
                          lru_lam[l][:, None, :], seq=seq, ctx=n_ctx)
        mq3, mk3, mv3 = (a.reshape(b, lt, -1) for a in (mq, mk, mv))
        yc = _mla_attention(mq3, mk3, mv3, tq=tq, q_block0=0, n_q=seq // tq, k_block=lt, k_block0=0, tk=tk)
        if last:
            yc = jnp.concatenate([yc, jnp.zeros((b, n_ctx, MLA_W), BF16)], axis=1)
        else:
            yc_ctx = _mla_attention(mq3, mk3, mv3, tq=n_ctx, q_block0=seq // n_ctx, n_q=1, k_block=n_ctx,
                                    k_block0=seq // n_ctx, tk=n_ctx)
            yc = jnp.concatenate([yc, yc_ctx], axis=1)

        wo = w_out[l].astype(BF16)
        xa, h2t, cls = _proj_out(
            ya.reshape(b * lt, -1), hscan, gg, yc.reshape(b * lt, -1), xa,
            wo[:NA_W], wo[NA_W:NA_W + LRU_W], wo[NA_W + LRU_W:], g[1:2], g[2:3], m(2), m(3), m(4), rwt, rb, rt=rt)

        pos, items = _routing_plan(cls.reshape(b, lt), n_tiles=tiles_per_sample)
        xs = _dispatch(pos, h2t, n_batch=b, tokens_per_sample=lt, tm=rt.tm)
        ys = _moe_routed(items, xs, rw, wg_all, wu_all, wd_all, expert_base=l * N_EXPERTS)
        xa = _combine(pos, ys, xa, m(5), g[3:4], n_batch=b, tokens_per_sample=lt, n_lat_tiles=n_lat_tiles,
                      n_out_tiles=n_lat_tiles if last else tiles_per_sample)

    return xa.reshape(b, seq, d)
```
